```python
import math
import jax
import jax.numpy as jnp
from jax import lax
import numpy as np

D_MODEL = 1024
BATCH = 32
SEQ = 256
DEPTH = 2
DEC_BATCH = 4
DEC_SEQ = 2048
PAST_LEN = 256

GRID_W = 64
ROPE_BASE = 10000.0
NORM_EPS = 1e-6
SCAN_CHUNK = 128
Q_BLOCK = 128
CONV_W = 3
N_BRANCH = 4
SSD_HEADS = 8
SSD_HEAD_DIM = 64
SSD_INNER = SSD_HEADS * SSD_HEAD_DIM
SSD_GROUPS = 2
SSD_STATE = 64
SSD_CONV_CH = SSD_INNER + 2 * SSD_GROUPS * SSD_STATE
RET_HEADS = 4
RET_QK = 64
RET_V = 128
ATT_HEADS = 8
ATT_KV_HEADS = 2
ATT_HEAD_DIM = 64
MLA_HEADS = 8
MLA_Q_RANK = 384
MLA_KV_RANK = 256
MLA_NOPE = 64
MLA_ROPE = 32
MLA_V = 64
D_FF = 2816
IN_WIDTHS = (SSD_INNER, SSD_CONV_CH, 2 * SSD_HEADS,
             RET_HEADS * RET_QK, RET_HEADS * RET_QK, RET_HEADS * RET_V, RET_HEADS * RET_V,
             ATT_HEADS * ATT_HEAD_DIM, ATT_KV_HEADS * ATT_HEAD_DIM, ATT_KV_HEADS * ATT_HEAD_DIM,
             MLA_Q_RANK, MLA_KV_RANK + MLA_ROPE)
IN_W = sum(IN_WIDTHS)

kernel_name = 'hybrid_prefix_diffusion_step'


def _rmsnorm(x, w):
    xf = x.astype(jnp.float32)
    y = xf * lax.rsqrt(jnp.mean(xf * xf, axis=-1, keepdims=True) + NORM_EPS)
    return y.astype(x.dtype) * w


def _head_layernorm(x):
    xf = x.astype(jnp.float32)
    xc = xf - jnp.mean(xf, axis=-1, keepdims=True)
    return xc * lax.rsqrt(jnp.mean(xc * xc, axis=-1, keepdims=True) + NORM_EPS)


def _dwconv_centred(x, w, b):
    pad = w.shape[0] // 2
    L = x.shape[1]
    xp = jnp.pad(x, ((0, 0), (pad, pad), (0, 0)))
    y = b
    for j in range(w.shape[0]):
        y = y + xp[:, j:j + L, :] * w[j]
    return y


def _rotate(x, ang):
    x1, x2 = jnp.split(x.astype(jnp.float32), 2, axis=-1)
    cos = jnp.cos(ang)[None, :, None, :]
    sin = jnp.sin(ang)[None, :, None, :]
    return jnp.concatenate([x1 * cos - x2 * sin, x2 * cos + x1 * sin], axis=-1).astype(x.dtype)


def _rope_2d(x):
    L = x.shape[1]
    d_axis = x.shape[-1] // 2
    n_rows = L // GRID_W
    rows = jnp.repeat(jnp.arange(n_rows, dtype=jnp.float32), GRID_W)
    cols = jnp.tile(jnp.arange(GRID_W, dtype=jnp.float32), n_rows)
    inv_freq = ROPE_BASE ** (-jnp.arange(0, d_axis, 2, dtype=jnp.float32) / d_axis)
    xr = _rotate(x[..., :d_axis], rows[:, None] * inv_freq[None, :])
    xc = _rotate(x[..., d_axis:], cols[:, None] * inv_freq[None, :])
    return jnp.concatenate([xr, xc], axis=-1)


def _chunk_scan(q, k, v, log_a, s0):
    bsz, L, nh, dk = q.shape
    dv = v.shape[-1]
    nc = L // SCAN_CHUNK
    T = SCAN_CHUNK
    f32 = jnp.float32
    qc = q.astype(f32).reshape(bsz, nc, T, nh, dk)
    kc = k.astype(f32).reshape(bsz, nc, T, nh, dk)
    vc = v.astype(f32).reshape(bsz, nc, T, nh, dv)
    acum = jnp.cumsum(log_a.astype(f32).reshape(bsz, nc, T, nh), axis=2)
    diff = acum[:, :, :, None, :] - acum[:, :, None, :, :]
    lower = jnp.tril(jnp.ones((T, T), dtype=bool))[None, None, :, :, None]
    decay = jnp.exp(jnp.where(lower, diff, -jnp.inf))
    scores = jnp.einsum('bcihd,bcjhd->bcijh', qc, kc) * decay
    o_intra = jnp.einsum('bcijh,bcjhe->bcihe', scores, vc)
    w_end = jnp.exp(acum[:, :, -1:, :] - acum)
    chunk_states = jnp.einsum('bcjhd,bcjh,bcjhe->bchde', kc, w_end, vc)
    chunk_decay = jnp.exp(acum[:, :, -1, :])

    def step(s, inp):
        cs, cd = inp
        return s * cd[..., None, None] + cs, s

    s_final, s_prev = lax.scan(step, s0.astype(f32),
                               (chunk_states.transpose(1, 0, 2, 3, 4), chunk_decay.transpose(1, 0, 2)))
    s_prev = s_prev.transpose(1, 0, 2, 3, 4)
    o_inter = jnp.einsum('bcihd,bchde->bcihe', qc * jnp.exp(acum)[..., None], s_prev)
    return (o_intra + o_inter).reshape(bsz, L, nh, dv), s_final


def _bidir_scan(q, k_f, k_b, v, la_f, la_b, s0_f, s0_b):
    o_f, s_f = _chunk_scan(q, k_f, v, la_f, s0_f)
    o_b, s_b = _chunk_scan(jnp.flip(q, 1), jnp.flip(k_b, 1), jnp.flip(v, 1), jnp.flip(la_b, 1), s0_b)
    return o_f + jnp.flip(o_b, 1), s_f, s_b


def _attention(q, k, v, scale):
    bsz, lq, n_h, dk = q.shape
    n_g = k.shape[2]
    rep = n_h // n_g
    nb = lq // Q_BLOCK
    qb = q.reshape(bsz, nb, Q_BLOCK, n_g, rep, dk).transpose(1, 0, 2, 3, 4, 5)

    def block(q_blk):
        s = jnp.einsum('bqgrd,bkgd->bgrqk', q_blk, k, preferred_element_type=jnp.float32) * scale
        p = jax.nn.softmax(s, axis=-1).astype(v.dtype)
        return jnp.einsum('bgrqk,bkge->bqgre', p, v)

    o = lax.map(block, qb)
    return o.transpose(1, 0, 2, 3, 4, 5).reshape(bsz, lq, n_h, v.shape[-1])


def _token_mixer(h, lw, ctx):
    bsz, L, _ = h.shape
    latent = ctx is not None
    f32 = jnp.float32
    split_at = np.cumsum(IN_WIDTHS)[:-1].tolist()
    (z, xbc, dt_raw, rq, rk, rv, rg, aq, ak, av, mcq, mckv) = jnp.split(h @ lw['w_in'], split_at, axis=-1)

    xbc = jax.nn.silu(_dwconv_centred(xbc, lw['ssd_conv_w'], lw['ssd_conv_b']))
    xs, bm, cm = jnp.split(xbc, [SSD_INNER, SSD_INNER + SSD_GROUPS * SSD_STATE], axis=-1)
    xs = xs.reshape(bsz, L, SSD_HEADS, SSD_HEAD_DIM)
    rep = SSD_HEADS // SSD_GROUPS
    bh = jnp.repeat(bm.reshape(bsz, L, SSD_GROUPS, SSD_STATE), rep, axis=2)
    ch = jnp.repeat(cm.reshape(bsz, L, SSD_GROUPS, SSD_STATE), rep, axis=2)
    dt = jax.nn.softplus(dt_raw.reshape(bsz, L, 2, SSD_HEADS).astype(f32) + lw['ssd_dt_bias'].astype(f32))
    a = -jnp.exp(lw['ssd_a_log'].astype(f32))
    s0 = ctx['ssd'] if latent else jnp.zeros((bsz, 2, SSD_HEADS, SSD_STATE, SSD_HEAD_DIM), f32)
    y, ssd_f, ssd_b = _bidir_scan(ch, bh * dt[:, :, 0, :, None], bh * dt[:, :, 1, :, None], xs,
                                  dt[:, :, 0] * a[0], dt[:, :, 1] * a[1], s0[:, 0], s0[:, 1])
    y = y.astype(h.dtype) + lw['ssd_d'][:, None] * xs
    o_ssd = _rmsnorm(y.reshape(bsz, L, SSD_INNER) * jax.nn.silu(z), lw['ssd_norm_w'])

    rq = rq.reshape(bsz, L, RET_HEADS, RET_QK)
    rk = rk.reshape(bsz, L, RET_HEADS, RET_QK) * RET_QK ** -0.5
    rv = rv.reshape(bsz, L, RET_HEADS, RET_V)
    if latent:
        rq = _rope_2d(rq)
        rk = _rope_2d(rk)
    la = jax.nn.log_sigmoid(lw['ret_decay_logit'].astype(f32))
    la_f = jnp.broadcast_to(la[0], (bsz, L, RET_HEADS))
    la_b = jnp.broadcast_to(la[1], (bsz, L, RET_HEADS))
    r0 = ctx['ret'] if latent else jnp.zeros((bsz, 2, RET_HEADS, RET_QK, RET_V), f32)
    o, ret_f, ret_b = _bidir_scan(rq, rk, rk, rv, la_f, la_b, r0[:, 0], r0[:, 1])
    o_ret = (_head_layernorm(o).astype(h.dtype).reshape(bsz, L, RET_HEADS * RET_V)
             * lw['ret_gn_w'] * jax.nn.silu(rg))

    aq = _rmsnorm(aq.reshape(bsz, L, ATT_HEADS, ATT_HEAD_DIM), lw['att_q_norm'])
    ak = _rmsnorm(ak.reshape(bsz, L, ATT_KV_HEADS, ATT_HEAD_DIM), lw['att_k_norm'])
    av = av.reshape(bsz, L, ATT_KV_HEADS, ATT_HEAD_DIM)
    if latent:
        keys = jnp.concatenate([ctx['att_k'], _rope_2d(ak)], axis=1)
        vals = jnp.concatenate([ctx['att_v'], av], axis=1)
        q_att = _rope_2d(aq)
    else:
        keys, vals, q_att = ak, av, aq
    o_att = _attention(q_att, keys, vals, ATT_HEAD_DIM ** -0.5).reshape(bsz, L, ATT_HEADS * ATT_HEAD_DIM)

    qm = (_rmsnorm(mcq, lw['mla_q_norm']) @ lw['mla_w_uq']).reshape(bsz, L, MLA_HEADS, MLA_NOPE + MLA_ROPE)
    q_nope, q_rope = jnp.split(qm, [MLA_NOPE], axis=-1)
    ckv = _rmsnorm(mckv[..., :MLA_KV_RANK], lw['mla_kv_norm'])
    krope = mckv[..., MLA_KV_RANK:]
    if latent:
        q_rope = _rope_2d(q_rope)
        ckv_keys = jnp.concatenate([ctx['mla_ckv'], ckv], axis=1)
        kr_keys = jnp.concatenate([ctx['mla_krope'], _rope_2d(krope[:, :, None, :])[:, :, 0, :]], axis=1)
    else:
        ckv_keys, kr_keys = ckv, krope
    lk = ckv_keys.shape[1]
    kv = (ckv_keys @ lw['mla_w_ukv']).reshape(bsz, lk, MLA_HEADS, MLA_NOPE + MLA_V)
    k_nope, v_m = jnp.split(kv, [MLA_NOPE], axis=-1)
    k_m = jnp.concatenate([k_nope, jnp.broadcast_to(kr_keys[:, :, None, :], (bsz, lk, MLA_HEADS, MLA_ROPE))], axis=-1)
    q_m = jnp.concatenate([q_nope, q_rope], axis=-1)
    o_mla = _attention(q_m, k_m, v_m, (MLA_NOPE + MLA_ROPE) ** -0.5).reshape(bsz, L, MLA_HEADS * MLA_V)

    g = jax.nn.sigmoid((h @ lw['w_merge'] + lw['b_merge']).astype(f32)).astype(h.dtype)
    g = g.reshape(bsz, L, N_BRANCH, D_MODEL)
    merged = (g[:, :, 0] * (o_ssd @ lw['w_br_ssd']) + g[:, :, 1] * (o_ret @ lw['w_br_ret'])
              + g[:, :, 2] * (o_att @ lw['w_br_att']) + g[:, :, 3] * (o_mla @ lw['w_br_mla']))
    out = merged @ lw['w_out']
    if latent:
        return out, None
    cache = {
        'ssd': jnp.stack([ssd_f, ssd_b], axis=1).astype(h.dtype),
        'ret': jnp.stack([ret_f, ret_b], axis=1).astype(h.dtype),
        'att_k': ak, 'att_v': av, 'mla_ckv': ckv, 'mla_krope': krope,
    }
    return out, cache


def _conv_ffn(h, lw):
    u = _dwconv_centred(h @ lw['w_ffn_up'], lw['ffn_conv_w'], lw['ffn_conv_b'])
    up, gate = jnp.split(u, 2, axis=-1)
    return (jax.nn.silu(gate) * up) @ lw['w_ffn_down']


def _trunk_layer(x, mod, lw, ctx):
    sh_a, sc_a, g_a, sh_f, sc_f, g_f = jnp.split(mod[:, None, :], 6, axis=-1)
    h = _rmsnorm(x, lw['g_pre_mix']) * (1.0 + sc_a) + sh_a
    m, cache = _token_mixer(h, lw, ctx)
    x = x + g_a * _rmsnorm(m, lw['g_post_mix'])
    h = _rmsnorm(x, lw['g_pre_ffn']) * (1.0 + sc_f) + sh_f
    x = x + g_f * _rmsnorm(_conv_ffn(h, lw), lw['g_post_ffn'])
    return x, cache


def setup_inputs(seed: int = 0) -> dict:
    key = jax.random.key(seed)
    ks = jax.random.split(key, 48)
    f32 = jnp.float32

    def nrm(i, shape, scale=1.0):
        return jax.random.normal(ks[i], shape, f32) * scale

    def gain(i, shape):
        return 1.0 + 0.02 * jax.random.normal(ks[i], shape, f32)

    a_log = jnp.log(jax.random.uniform(ks[10], (DEPTH, 2, SSD_HEADS), f32, 1.0, 16.0))
    dt0 = jnp.exp(jax.random.uniform(ks[11], (DEPTH, 2, SSD_HEADS), f32, math.log(1e-3), math.log(1e-1)))
    dt_bias = dt0 + jnp.log(-jnp.expm1(-dt0))
    gamma = 1.0 - 2.0 ** (-5.0 - jnp.arange(RET_HEADS, dtype=f32))
    ret_logit = (jnp.log(gamma) - jnp.log1p(-gamma))[None, None, :] + nrm(12, (DEPTH, 2, RET_HEADS), 0.1)
    return {
        'x_prompt': nrm(0, (BATCH, SEQ, D_MODEL)),
        'x_sample': nrm(1, (DEC_BATCH, DEC_SEQ, D_MODEL)),
        'state_ssd': nrm(2, (DEC_BATCH, DEPTH, 2, SSD_HEADS, SSD_STATE, SSD_HEAD_DIM), 0.5),
        'state_ret': nrm(3, (DEC_BATCH, DEPTH, 2, RET_HEADS, RET_QK, RET_V), 0.5),
        'cache_att_k': nrm(4, (DEC_BATCH, DEPTH, PAST_LEN, ATT_KV_HEADS, ATT_HEAD_DIM)),
        'cache_att_v': nrm(5, (DEC_BATCH, DEPTH, PAST_LEN, ATT_KV_HEADS, ATT_HEAD_DIM)),
        'cache_mla_ckv': nrm(6, (DEC_BATCH, DEPTH, PAST_LEN, MLA_KV_RANK)),
        'cache_mla_krope': nrm(7, (DEC_BATCH, DEPTH, PAST_LEN, MLA_ROPE)),
        'c': nrm(8, (DEC_BATCH, D_MODEL)),
        'c_ctx': nrm(9, (D_MODEL,)),
        'w_mod': nrm(13, (DEPTH, D_MODEL, 6 * D_MODEL), 0.5 * D_MODEL ** -0.5),
        'b_mod': nrm(14, (DEPTH, 6 * D_MODEL), 0.01),
        'g_pre_mix': gain(15, (DEPTH, D_MODEL)),
        'g_post_mix': gain(16, (DEPTH, D_MODEL)),
        'g_pre_ffn': gain(17, (DEPTH, D_MODEL)),
        'g_post_ffn': gain(18, (DEPTH, D_MODEL)),
        'w_in': nrm(19, (DEPTH, D_MODEL, IN_W), D_MODEL ** -0.5),
        'ssd_conv_w': nrm(20, (DEPTH, CONV_W, SSD_CONV_CH), CONV_W ** -0.5),
        'ssd_conv_b': nrm(21, (DEPTH, SSD_CONV_CH), 0.01),
        'ssd_dt_bias': dt_bias,
        'ssd_a_log': a_log,
        'ssd_d': gain(22, (DEPTH, SSD_HEADS)),
        'ssd_norm_w': gain(23, (DEPTH, SSD_INNER)),
        'ret_decay_logit': ret_logit,
        'ret_gn_w': gain(24, (DEPTH, RET_HEADS * RET_V)),
        'att_q_norm': gain(25, (DEPTH, ATT_HEAD_DIM)),
        'att_k_norm': gain(26, (DEPTH, ATT_HEAD_DIM)),
        'mla_q_norm': gain(27, (DEPTH, MLA_Q_RANK)),
        'mla_w_uq': nrm(28, (DEPTH, MLA_Q_RANK, MLA_HEADS * (MLA_NOPE + MLA_ROPE)), MLA_Q_RANK ** -0.5),
        'mla_kv_norm': gain(29, (DEPTH, MLA_KV_RANK)),
        'mla_w_ukv': nrm(30, (DEPTH, MLA_KV_RANK, MLA_HEADS * (MLA_NOPE + MLA_V)), MLA_KV_RANK ** -0.5),
        'w_br_ssd': nrm(31, (DEPTH, SSD_INNER, D_MODEL), SSD_INNER ** -0.5),
        'w_br_ret': nrm(32, (DEPTH, RET_HEADS * RET_V, D_MODEL), (RET_HEADS * RET_V) ** -0.5),
        'w_br_att': nrm(33, (DEPTH, ATT_HEADS * ATT_HEAD_DIM, D_MODEL), (ATT_HEADS * ATT_HEAD_DIM) ** -0.5),
        'w_br_mla': nrm(34, (DEPTH, MLA_HEADS * MLA_V, D_MODEL), (MLA_HEADS * MLA_V) ** -0.5),
        'w_merge': nrm(35, (DEPTH, D_MODEL, N_BRANCH * D_MODEL), D_MODEL ** -0.5),
        'b_merge': nrm(36, (DEPTH, N_BRANCH * D_MODEL), 0.01),
        'w_out': nrm(37, (DEPTH, D_MODEL, D_MODEL), D_MODEL ** -0.5),
        'w_ffn_up': nrm(38, (DEPTH, D_MODEL, 2 * D_FF), D_MODEL ** -0.5),
        'ffn_conv_w': nrm(39, (DEPTH, CONV_W, 2 * D_FF), CONV_W ** -0.5),
        'ffn_conv_b': nrm(40, (DEPTH, 2 * D_FF), 0.01),
        'w_ffn_down': nrm(41, (DEPTH, D_FF, D_MODEL), D_FF ** -0.5),
    }


def reference(x_prompt, x_sample, state_ssd, state_ret, cache_att_k, cache_att_v, cache_mla_ckv,
              cache_mla_krope, c, c_ctx, w_mod, b_mod, g_pre_mix, g_post_mix, g_pre_ffn, g_post_ffn,
              w_in, ssd_conv_w, ssd_conv_b, ssd_dt_bias, ssd_a_log, ssd_d, ssd_norm_w, ret_decay_logit,
              ret_gn_w, att_q_norm, att_k_norm, mla_q_norm, mla_w_uq, mla_kv_norm, mla_w_ukv, w_br_ssd,
              w_br_ret, w_br_att, w_br_mla, w_merge, b_merge, w_out, w_ffn_up, ffn_conv_w, ffn_conv_b,
              w_ffn_down):
    y_prompt = x_prompt
    y_sample = x_sample
    st_ssd, st_ret, ck_k, ck_v, ck_c, ck_r = [], [], [], [], [], []
    for i in range(DEPTH):
        lw = {
            'g_pre_mix': g_pre_mix[i], 'g_post_mix': g_post_mix[i],
            'g_pre_ffn': g_pre_ffn[i], 'g_post_ffn': g_post_ffn[i],
            'w_in': w_in[i], 'ssd_conv_w': ssd_conv_w[i], 'ssd_conv_b': ssd_conv_b[i],
            'ssd_dt_bias': ssd_dt_bias[i], 'ssd_a_log': ssd_a_log[i], 'ssd_d': ssd_d[i],
            'ssd_norm_w': ssd_norm_w[i], 'ret_decay_logit': ret_decay_logit[i], 'ret_gn_w': ret_gn_w[i],
            'att_q_norm': att_q_norm[i], 'att_k_norm': att_k_norm[i],
            'mla_q_norm': mla_q_norm[i], 'mla_w_uq': mla_w_uq[i],
            'mla_kv_norm': mla_kv_norm[i], 'mla_w_ukv': mla_w_ukv[i],
            'w_br_ssd': w_br_ssd[i], 'w_br_ret': w_br_ret[i], 'w_br_att': w_br_att[i], 'w_br_mla': w_br_mla[i],
            'w_merge': w_merge[i], 'b_merge': b_merge[i], 'w_out': w_out[i],
            'w_ffn_up': w_ffn_up[i], 'ffn_conv_w': ffn_conv_w[i], 'ffn_conv_b': ffn_conv_b[i],
            'w_ffn_down': w_ffn_down[i],
        }
        mod_ctx = (jax.nn.silu(c_ctx) @ w_mod[i] + b_mod[i])[None, :]
        y_prompt, cache = _trunk_layer(y_prompt, mod_ctx, lw, None)
        st_ssd.append(cache['ssd'])
        st_ret.append(cache['ret'])
        ck_k.append(cache['att_k'])
        ck_v.append(cache['att_v'])
        ck_c.append(cache['mla_ckv'])
        ck_r.append(cache['mla_krope'])
        mod_lat = jax.nn.silu(c) @ w_mod[i] + b_mod[i]
        ctx = {
            'ssd': state_ssd[:, i], 'ret': state_ret[:, i],
            'att_k': cache_att_k[:, i], 'att_v': cache_att_v[:, i],
            'mla_ckv': cache_mla_ckv[:, i], 'mla_krope': cache_mla_krope[:, i],
        }
        y_sample, _ = _trunk_layer(y_sample, mod_lat, lw, ctx)
    new_state_ssd = jnp.stack(st_ssd, axis=1)
    new_state_ret = jnp.stack(st_ret, axis=1)
    new_cache_att_k = jnp.stack(ck_k, axis=1)
    new_cache_att_v = jnp.stack(ck_v, axis=1)
    new_cache_mla_ckv = jnp.stack(ck_c, axis=1)
    new_cache_mla_krope = jnp.stack(ck_r, axis=1)
    return (y_prompt, y_sample, new_state_ssd, new_state_ret, new_cache_att_k, new_cache_att_v,
            new_cache_mla_ckv, new_cache_mla_krope)
```

```python
import functools

import numpy as np
import jax
import jax.numpy as jnp
from jax import lax
from jax.experimental import pallas as pl
from jax.experimental.pallas import tpu as pltpu

D_MODEL = 1024
BATCH = 32
SEQ = 256
DEPTH = 2
DEC_BATCH = 4
DEC_SEQ = 2048
PAST_LEN = 256
GRID_W = 64
ROPE_BASE = 10000.0
NORM_EPS = 1e-6
SCAN_CHUNK = 128
SSD_HEADS = 8
SSD_HEAD_DIM = 64
SSD_INNER = 512
SSD_GROUPS = 2
SSD_STATE = 64
SSD_CONV_CH = 768
RET_HEADS = 4
RET_QK = 64
RET_V = 128
ATT_HEADS = 8
ATT_KV_HEADS = 2
ATT_HEAD_DIM = 64
MLA_HEADS = 8
MLA_Q_RANK = 384
MLA_KV_RANK = 256
MLA_NOPE = 64
MLA_ROPE = 32
MLA_V = 64
D_FF = 2816

F32 = jnp.float32
BF16 = jnp.bfloat16
LANES = 128
SUBLANES = 8
VMEM_LIMIT = 56 * 1024 * 1024
NEG_BIG = -1e30

SSD_W = 1408
RET_W = 1536
TM_ROWS = 512
FF_CHUNK = 256


def _cparams(n_axes):
    return pltpu.CompilerParams(dimension_semantics=("arbitrary",) * n_axes,
                                vmem_limit_bytes=VMEM_LIMIT)


def _silu(x):
    return x * jax.nn.sigmoid(x)


def _softplus(x):
    return jnp.maximum(x, 0.0) + jnp.log1p(jnp.exp(-jnp.abs(x)))


def _rms(x, w):
    ms = jnp.mean(x * x, axis=-1, keepdims=True)
    return x * lax.rsqrt(ms + NORM_EPS) * w


def _bdot(a, b):
    return jnp.dot(a.astype(BF16), b.astype(BF16), preferred_element_type=F32)


def _bdot_nt(a, b):
    return lax.dot_general(a.astype(BF16), b.astype(BF16), (((1,), (1,)), ((), ())),
                           preferred_element_type=F32)


def _bdot_tn(a, b):
    return lax.dot_general(a.astype(BF16), b.astype(BF16), (((0,), (0,)), ((), ())),
                           preferred_element_type=F32)


def _split_dot(a, b_bf16):
    hi = a.astype(BF16)
    lo = (a - hi.astype(F32)).astype(BF16)
    return (jnp.dot(hi, b_bf16, preferred_element_type=F32)
            + jnp.dot(lo, b_bf16, preferred_element_type=F32))


def _split_dot_left(a_bf16, b):
    hi = b.astype(BF16)
    lo = (b - hi.astype(F32)).astype(BF16)
    return (jnp.dot(a_bf16, hi, preferred_element_type=F32)
            + jnp.dot(a_bf16, lo, preferred_element_type=F32))


def _lane_iota(shape):
    return lax.broadcasted_iota(jnp.int32, shape, len(shape) - 1)


def _row_iota(shape):
    return lax.broadcasted_iota(jnp.int32, shape, 0)


def _rope(xb, cos, sin_up, sin_dn, half):
    return (xb * cos + pltpu.roll(xb, half, 1) * sin_up
            + pltpu.roll(xb, LANES - half, 1) * sin_dn)


def _half_rms(xb, w_row):
    lo = _lane_iota(xb.shape) < 64
    sq = xb * xb
    s_lo = jnp.sum(jnp.where(lo, sq, 0.0), axis=-1, keepdims=True)
    s_hi = jnp.sum(jnp.where(lo, 0.0, sq), axis=-1, keepdims=True)
    ms = jnp.where(lo, s_lo, s_hi) * (1.0 / 64.0)
    return xb * lax.rsqrt(ms + NORM_EPS) * w_row


def _mod_kernel(c_ref, w_ref, b_ref, o_ref):
    c = c_ref[...]
    o_ref[...] = _bdot(_silu(c), w_ref[...]) + b_ref[...]


def _modulation(cvec, w_mod, b_mod):
    tn = 1536
    n_out = 6 * D_MODEL
    return pl.pallas_call(
        _mod_kernel,
        grid=(DEPTH, n_out // tn),
        in_specs=[pl.BlockSpec((SUBLANES, D_MODEL), lambda l, j: (0, 0)),
                  pl.BlockSpec((None, D_MODEL, tn), lambda l, j: (l, 0, j)),
                  pl.BlockSpec((None, 1, tn), lambda l, j: (l, 0, j))],
        out_specs=pl.BlockSpec((None, SUBLANES, tn), lambda l, j: (l, 0, j)),
        out_shape=jax.ShapeDtypeStruct((DEPTH, SUBLANES, n_out), F32),
        compiler_params=_cparams(2),
        name="modulation",
    )(cvec, w_mod, b_mod.reshape(DEPTH, 1, n_out))


def _mod_spec(layer, latent, seq_len, tm, k):
    tiles_per_seq = seq_len // tm

    def index_map(i):
        row = 1 + i // tiles_per_seq if latent else 0
        return (layer, row, k, 0, 0)

    return pl.BlockSpec((None, None, None, 1, D_MODEL), index_map)


def _inproj_kernel(latent, *refs):
    if latent:
        (x_ref, sh_ref, sc_ref, g_ref, w_ref, qn_ref, kn_ref, cn_ref,
         c64_ref, su64_ref, sd64_ref, cm_ref, sum_ref, sdm_ref,
         ssd_ref, ret_ref, aq_ref, ak_ref, av_ref, mq_ref, ckv_ref, kr_ref) = refs
    else:
        (x_ref, sh_ref, sc_ref, g_ref, w_ref, qn_ref, kn_ref, cn_ref,
         ssd_ref, ret_ref, aq_ref, ak_ref, av_ref, mq_ref, ckv_ref, kr_ref) = refs

    h = _rms(x_ref[...], g_ref[...]) * (1.0 + sc_ref[...]) + sh_ref[...]
    hb = h.astype(BF16)

    def proj(c0, width):
        return jnp.dot(hb, w_ref[:, c0:c0 + width], preferred_element_type=F32)

    def rope64(xb):
        if not latent:
            return xb
        return _rope(xb, c64_ref[...], su64_ref[...], sd64_ref[...], 16)

    ssd_ref[...] = proj(0, SSD_W)

    base = SSD_W
    for j in range(2):
        ret_ref[:, j * LANES:(j + 1) * LANES] = rope64(proj(base + j * LANES, LANES))
    for j in range(2, 4):
        ret_ref[:, j * LANES:(j + 1) * LANES] = rope64(proj(base + j * LANES, LANES) * 0.125)
    ret_ref[:, 512:RET_W] = proj(base + 512, 1024)

    base += RET_W
    for j in range(4):
        q = _half_rms(proj(base + j * LANES, LANES), qn_ref[...])
        aq_ref[:, j * LANES:(j + 1) * LANES] = rope64(q)
    ak_ref[...] = rope64(_half_rms(proj(base + 512, LANES), kn_ref[...]))
    av_ref[...] = proj(base + 640, LANES)

    base += 768
    mq_ref[...] = proj(base, MLA_Q_RANK)
    ckv_ref[...] = _rms(proj(base + MLA_Q_RANK, MLA_KV_RANK), cn_ref[...])
    kr = proj(base + MLA_Q_RANK + MLA_KV_RANK, LANES)
    if latent:
        kr = _rope(kr, cm_ref[...], sum_ref[...], sdm_ref[...], 8)
    kr_ref[...] = kr


IN_TOTAL = SSD_W + RET_W + 768 + 768


def _inproj(x2d, mod5, layer, latent, seq_len, g_pre, w_in_p, qn, kn, cn, tabs):
    n = x2d.shape[0]
    tm = TM_ROWS if seq_len % TM_ROWS == 0 else seq_len
    tiles_per_seq = seq_len // tm
    row = lambda i: (i, 0)
    const = lambda i: (0, 0)
    in_specs = [pl.BlockSpec((tm, D_MODEL), row),
                _mod_spec(layer, latent, seq_len, tm, 0),
                _mod_spec(layer, latent, seq_len, tm, 1),
                pl.BlockSpec((1, D_MODEL), const),
                pl.BlockSpec((D_MODEL, IN_TOTAL), const),
                pl.BlockSpec((1, LANES), const),
                pl.BlockSpec((1, LANES), const),
                pl.BlockSpec((1, MLA_KV_RANK), const)]
    args = [x2d, mod5, mod5, g_pre, w_in_p, qn, kn, cn]
    if latent:
        tab = pl.BlockSpec((tm, LANES), lambda i: (i % tiles_per_seq, 0))
        in_specs += [tab] * 6
        args += list(tabs)
    widths = (SSD_W, RET_W, 512, LANES, LANES, MLA_Q_RANK, MLA_KV_RANK, LANES)
    return pl.pallas_call(
        functools.partial(_inproj_kernel, latent),
        grid=(n // tm,),
        in_specs=in_specs,
        out_specs=[pl.BlockSpec((tm, w), row) for w in widths],
        out_shape=[jax.ShapeDtypeStruct((n, w), F32) for w in widths],
        compiler_params=_cparams(1),
        name="inproj_lat" if latent else "inproj_ctx",
    )(*args)


def _ssd_kernel(latent, seq_len, *refs):
    if latent:
        (in_ref, cw_ref, cb_ref, dtb_ref, alog_ref, dsk_ref, nw_ref, ex_ref, s0_ref,
         o_ref, xc_ref, dt_ref, cum_ref, sf_ref, sb_ref, cdb_ref, run_ref) = refs
    else:
        (in_ref, cw_ref, cb_ref, dtb_ref, alog_ref, dsk_ref, nw_ref, ex_ref,
         o_ref, st_ref, xc_ref, dt_ref, cum_ref, sf_ref, sb_ref, cdb_ref, run_ref) = refs
    T = SCAN_CHUNK
    nc = seq_len // T
    ii = _row_iota((T, T))
    jj = _lane_iota((T, T))
    tri_lo = (jj <= ii).astype(BF16)
    tri_up = (jj >= ii).astype(BF16)
    lane_c = _lane_iota((T, LANES))
    row_c = _row_iota((T, SSD_CONV_CH))
    fwd_lane = lane_c < SSD_HEADS
    a_row = -jnp.exp(alog_ref[...])
    expand = ex_ref[...]

    if latent:
        run_ref[...] = s0_ref[...]
    else:
        run_ref[...] = jnp.zeros(run_ref.shape, F32)

    def phase0(c, carry):
        r0 = pl.multiple_of(c * T, T)
        x_cur = in_ref[pl.ds(r0, T), 512:1280]
        prev_blk = in_ref[pl.ds(pl.multiple_of(jnp.maximum(r0 - SUBLANES, 0), SUBLANES), SUBLANES), 512:1280]
        next_blk = in_ref[pl.ds(pl.multiple_of(jnp.minimum(r0 + T, seq_len - SUBLANES), SUBLANES), SUBLANES), 512:1280]
        prev_row = jnp.where(c > 0, prev_blk[SUBLANES - 1:SUBLANES, :], 0.0)
        next_row = jnp.where(c < nc - 1, next_blk[0:1, :], 0.0)
        x_prev = jnp.where(row_c == 0, prev_row, pltpu.roll(x_cur, 1, 0))
        x_next = jnp.where(row_c == T - 1, next_row, pltpu.roll(x_cur, T - 1, 0))
        xc = _silu(cb_ref[...] + x_prev * cw_ref[0:1, :] + x_cur * cw_ref[1:2, :] + x_next * cw_ref[2:3, :])
        xc_ref[pl.ds(r0, T), :] = xc

        dt = _softplus(in_ref[pl.ds(r0, T), 1280:SSD_W] + dtb_ref[...])
        la = dt * a_row
        cum = jnp.where(fwd_lane, _split_dot_left(tri_lo, la), _split_dot_left(tri_up, la))
        tot = jnp.where(fwd_lane[0:1], cum[T - 1:T, :], cum[0:1, :])
        dt_ref[pl.ds(r0, T), :] = dt
        cum_ref[pl.ds(r0, T), :] = cum
        w = dt * jnp.exp(tot - cum)
        wcd = jnp.concatenate([w, jnp.broadcast_to(jnp.exp(tot), (SUBLANES, LANES))], axis=0)
        wcd_x = _split_dot(wcd, expand)
        w_x = wcd_x[0:T]
        cd_x = wcd_x[T:T + 1]
        cdb_ref[c] = jnp.broadcast_to(cd_x[:, 512:1024], (SUBLANES, 512))
        xs = xc[:, 0:512]
        bm = xc[:, 512:640].astype(BF16)
        for p in range(4):
            g = p // 2
            sl = slice(p * LANES, (p + 1) * LANES)
            vw = jnp.concatenate([xs[:, sl] * w_x[:, sl], xs[:, sl] * w_x[:, 512 + p * LANES:512 + (p + 1) * LANES]],
                                 axis=1)
            cs = _bdot_tn(bm, vw)[g * 64:(g + 1) * 64]
            s_run = run_ref[p]
            sf_ref[c, p] = s_run[:, 0:LANES]
            sb_ref[c, p] = cs[:, LANES:2 * LANES]
            new_f = s_run[:, 0:LANES] * cd_x[:, sl] + cs[:, 0:LANES]
            run_ref[p] = jnp.concatenate([new_f, s_run[:, LANES:2 * LANES]], axis=1)
        return carry

    lax.fori_loop(0, nc, phase0, 0)

    def bwd_states(k, carry):
        c = nc - 1 - k
        cd = cdb_ref[c]
        for p in range(4):
            s_run = run_ref[p]
            cs_b = sb_ref[c, p]
            sb_ref[c, p] = s_run[:, LANES:2 * LANES]
            new_b = s_run[:, LANES:2 * LANES] * cd[0:1, p * LANES:(p + 1) * LANES] + cs_b
            run_ref[p] = jnp.concatenate([s_run[:, 0:LANES], new_b], axis=1)
        return carry

    lax.fori_loop(0, nc, bwd_states, 0)
    if not latent:
        st_ref[...] = run_ref[...]

    lower = jj <= ii
    upper = jj >= ii
    zeros64 = jnp.zeros((64, 2 * LANES), F32)

    def phase1(c, carry):
        r0 = pl.multiple_of(c * T, T)
        xc = xc_ref[pl.ds(r0, T), :]
        xs = xc[:, 0:512]
        bm = xc[:, 512:640].astype(BF16)
        cm = xc[:, 640:768]
        dt = dt_ref[pl.ds(r0, T), :]
        cum = cum_ref[pl.ds(r0, T), :]
        cum_t = cum.T
        dt_t = dt.T
        e_x = _split_dot(jnp.exp(cum), expand)
        gmat = [_bdot_nt(jnp.where((lane_c < 64) == (g == 0), cm, 0.0), bm) for g in range(2)]
        cmb = cm.astype(BF16)
        outs = []
        for p in range(4):
            g = p // 2
            sl = slice(p * LANES, (p + 1) * LANES)
            xs_p = xs[:, sl].astype(BF16)
            halves = []
            for h in (2 * p, 2 * p + 1):
                df = jnp.exp(jnp.where(lower, cum[:, h:h + 1] - cum_t[h:h + 1, :], NEG_BIG)) * dt_t[h:h + 1, :]
                hb = SSD_HEADS + h
                db = jnp.exp(jnp.where(upper, cum[:, hb:hb + 1] - cum_t[hb:hb + 1, :], NEG_BIG)) * dt_t[hb:hb + 1, :]
                m_h = (gmat[g] * (df + db)).astype(BF16)
                halves.append(jnp.dot(m_h, xs_p, preferred_element_type=F32))
            o_p = jnp.where(lane_c < 64, halves[0], halves[1])
            s_in = jnp.concatenate([sf_ref[c, p], sb_ref[c, p]], axis=1)
            s_pad = jnp.concatenate([s_in, zeros64] if g == 0 else [zeros64, s_in], axis=0)
            oi = jnp.dot(cmb, s_pad.astype(BF16), preferred_element_type=F32)
            o_p = o_p + oi[:, 0:LANES] * e_x[:, sl] + oi[:, LANES:2 * LANES] * e_x[:, 512 + p * LANES:512 + (p + 1) * LANES]
            outs.append(o_p)
        y = jnp.concatenate(outs, axis=1) + dsk_ref[...] * xs
        z = in_ref[pl.ds(r0, T), 0:512]
        o_ref[pl.ds(r0, T), :] = _rms(y * _silu(z), nw_ref[...])
        return carry

    lax.fori_loop(0, nc, phase1, 0)


def _ssd_mixer(ssd_in, bsz, seq_len, latent, cw, cb, dtb, alog, dsk, nw, expand, s0):
    nc = seq_len // SCAN_CHUNK
    const2 = lambda b: (0, 0)
    per_b3 = lambda b: (b, 0, 0)
    per_b4 = lambda b: (b, 0, 0, 0)
    in_specs = [pl.BlockSpec((None, seq_len, SSD_W), per_b3),
                pl.BlockSpec((3, SSD_CONV_CH), const2),
                pl.BlockSpec((1, SSD_CONV_CH), const2),
                pl.BlockSpec((1, LANES), const2),
                pl.BlockSpec((1, LANES), const2),
                pl.BlockSpec((1, SSD_INNER), const2),
                pl.BlockSpec((1, SSD_INNER), const2),
                pl.BlockSpec((LANES, 1024), const2)]
    args = [ssd_in.reshape(bsz, seq_len, SSD_W), cw, cb, dtb, alog, dsk, nw, expand]
    out_specs = [pl.BlockSpec((None, seq_len, SSD_INNER), per_b3)]
    out_shape = [jax.ShapeDtypeStruct((bsz, seq_len, SSD_INNER), F32)]
    if latent:
        in_specs.append(pl.BlockSpec((None, 4, 64, 2 * LANES), per_b4))
        args.append(s0)
    else:
        out_specs.append(pl.BlockSpec((None, 4, 64, 2 * LANES), per_b4))
        out_shape.append(jax.ShapeDtypeStruct((bsz, 4, 64, 2 * LANES), F32))
    scratch = [pltpu.VMEM((seq_len, SSD_CONV_CH), F32),
               pltpu.VMEM((seq_len, LANES), F32),
               pltpu.VMEM((seq_len, LANES), F32),
               pltpu.VMEM((nc, 4, 64, LANES), F32),
               pltpu.VMEM((nc, 4, 64, LANES), F32),
               pltpu.VMEM((nc, SUBLANES, 512), F32),
               pltpu.VMEM((4, 64, 2 * LANES), F32)]
    res = pl.pallas_call(
        functools.partial(_ssd_kernel, latent, seq_len),
        grid=(bsz,),
        in_specs=in_specs, out_specs=out_specs, out_shape=out_shape,
        scratch_shapes=scratch,
        compiler_params=_cparams(1),
        name="ssd_lat" if latent else "ssd_ctx",
    )(*args)
    return res


def _ret_kernel(latent, seq_len, *refs):
    if latent:
        (in_ref, lg_ref, gn_ref, s0_ref,
         o_ref, dc_ref, we_ref, cd_ref, sf_ref, sb_ref, run_ref) = refs
    else:
        (in_ref, lg_ref, gn_ref,
         o_ref, st_ref, dc_ref, we_ref, cd_ref, sf_ref, sb_ref, run_ref) = refs
    T = SCAN_CHUNK
    nc = seq_len // T

    @pl.when(pl.program_id(0) == 0)
    def _tables():
        la = -_softplus(-lg_ref[...])
        ii = _row_iota((T, T)).astype(F32)
        jj = _lane_iota((T, T)).astype(F32)
        for h in range(RET_HEADS):
            la_f = la[h:h + 1, :]
            la_b = la[RET_HEADS + h:RET_HEADS + h + 1, :]
            dc_ref[h] = (jnp.exp(jnp.where(jj <= ii, (ii - jj) * la_f, NEG_BIG))
                         + jnp.exp(jnp.where(jj >= ii, (jj - ii) * la_b, NEG_BIG)))
            we_ref[h, 0] = jnp.exp((T - 1.0 - ii) * la_f)
            we_ref[h, 1] = jnp.exp(ii * la_b)
            we_ref[h, 2] = jnp.exp((ii + 1.0) * la_f)
            we_ref[h, 3] = jnp.exp((T - ii) * la_b)
            cd_ref[h] = jnp.concatenate([jnp.broadcast_to(jnp.exp(T * la_f), (SUBLANES, LANES)),
                                         jnp.broadcast_to(jnp.exp(T * la_b), (SUBLANES, LANES))], axis=1)

    if latent:
        run_ref[...] = s0_ref[...]
    else:
        run_ref[...] = jnp.zeros(run_ref.shape, F32)

    def phase0(c, carry):
        r0 = pl.multiple_of(c * T, T)
        for h in range(RET_HEADS):
            blk = h // 2
            k_blk = in_ref[pl.ds(r0, T), 256 + blk * LANES:256 + (blk + 1) * LANES].astype(BF16)
            v_h = in_ref[pl.ds(r0, T), 512 + h * LANES:512 + (h + 1) * LANES]
            vw = jnp.concatenate([v_h * we_ref[h, 0], v_h * we_ref[h, 1]], axis=1)
            cs = _bdot_tn(k_blk, vw)[(h % 2) * 64:(h % 2 + 1) * 64]
            s_run = run_ref[h]
            sf_ref[c, h] = s_run[:, 0:LANES]
            sb_ref[c, h] = cs[:, LANES:2 * LANES]
            new_f = s_run[:, 0:LANES] * cd_ref[h, 0:1, 0:LANES] + cs[:, 0:LANES]
            run_ref[h] = jnp.concatenate([new_f, s_run[:, LANES:2 * LANES]], axis=1)
        return carry

    lax.fori_loop(0, nc, phase0, 0)

    def bwd_states(k, carry):
        c = nc - 1 - k
        for h in range(RET_HEADS):
            s_run = run_ref[h]
            cs_b = sb_ref[c, h]
            sb_ref[c, h] = s_run[:, LANES:2 * LANES]
            new_b = s_run[:, LANES:2 * LANES] * cd_ref[h, 0:1, LANES:2 * LANES] + cs_b
            run_ref[h] = jnp.concatenate([s_run[:, 0:LANES], new_b], axis=1)
        return carry

    lax.fori_loop(0, nc, bwd_states, 0)
    if not latent:
        st_ref[...] = run_ref[...]

    lane_c = _lane_iota((T, LANES))
    zeros64 = jnp.zeros((64, 2 * LANES), F32)

    def phase1(c, carry):
        r0 = pl.multiple_of(c * T, T)
        for h in range(RET_HEADS):
            blk = h // 2
            q_blk = in_ref[pl.ds(r0, T), blk * LANES:(blk + 1) * LANES]
            k_blk = in_ref[pl.ds(r0, T), 256 + blk * LANES:256 + (blk + 1) * LANES].astype(BF16)
            v_h = in_ref[pl.ds(r0, T), 512 + h * LANES:512 + (h + 1) * LANES].astype(BF16)
            g_h = in_ref[pl.ds(r0, T), 1024 + h * LANES:1024 + (h + 1) * LANES]
            q_m = jnp.where((lane_c < 64) == (h % 2 == 0), q_blk, 0.0)
            m_h = (_bdot_nt(q_m, k_blk) * dc_ref[h]).astype(BF16)
            o = jnp.dot(m_h, v_h, preferred_element_type=F32)
            s_in = jnp.concatenate([sf_ref[c, h], sb_ref[c, h]], axis=1)
            s_pad = jnp.concatenate([s_in, zeros64] if h % 2 == 0 else [zeros64, s_in], axis=0)
            oi = _bdot(q_blk, s_pad)
            o = o + oi[:, 0:LANES] * we_ref[h, 2] + oi[:, LANES:2 * LANES] * we_ref[h, 3]
            oc = o - jnp.mean(o, axis=-1, keepdims=True)
            on = oc * lax.rsqrt(jnp.mean(oc * oc, axis=-1, keepdims=True) + NORM_EPS)
            o_ref[pl.ds(r0, T), h * LANES:(h + 1) * LANES] = (
                on * gn_ref[:, h * LANES:(h + 1) * LANES] * _silu(g_h))
        return carry

    lax.fori_loop(0, nc, phase1, 0)


def _ret_mixer(ret_in, bsz, seq_len, latent, logit_rows, gn_w, s0):
    nc = seq_len // SCAN_CHUNK
    T = SCAN_CHUNK
    const2 = lambda b: (0, 0)
    per_b3 = lambda b: (b, 0, 0)
    per_b4 = lambda b: (b, 0, 0, 0)
    in_specs = [pl.BlockSpec((None, seq_len, RET_W), per_b3),
                pl.BlockSpec((SUBLANES, LANES), const2),
                pl.BlockSpec((1, 512), const2)]
    args = [ret_in.reshape(bsz, seq_len, RET_W), logit_rows, gn_w]
    out_specs = [pl.BlockSpec((None, seq_len, 512), per_b3)]
    out_shape = [jax.ShapeDtypeStruct((bsz, seq_len, 512), F32)]
    if latent:
        in_specs.append(pl.BlockSpec((None, RET_HEADS, 64, 2 * LANES), per_b4))
        args.append(s0)
    else:
        out_specs.append(pl.BlockSpec((None, RET_HEADS, 64, 2 * LANES), per_b4))
        out_shape.append(jax.ShapeDtypeStruct((bsz, RET_HEADS, 64, 2 * LANES), F32))
    scratch = [pltpu.VMEM((RET_HEADS, T, T), F32),
               pltpu.VMEM((RET_HEADS, 4, T, LANES), F32),
               pltpu.VMEM((RET_HEADS, SUBLANES, 2 * LANES), F32),
               pltpu.VMEM((nc, RET_HEADS, 64, LANES), F32),
               pltpu.VMEM((nc, RET_HEADS, 64, LANES), F32),
               pltpu.VMEM((RET_HEADS, 64, 2 * LANES), F32)]
    return pl.pallas_call(
        functools.partial(_ret_kernel, latent, seq_len),
        grid=(bsz,),
        in_specs=in_specs, out_specs=out_specs, out_shape=out_shape,
        scratch_shapes=scratch,
        compiler_params=_cparams(1),
        name="ret_lat" if latent else "ret_ctx",
    )(*args)


def _softmax_pv(s, v):
    m = jnp.max(s, axis=-1, keepdims=True)
    p = jnp.exp(s - m)
    l = jnp.sum(p, axis=-1, keepdims=True)
    return jnp.dot(p.astype(BF16), v, preferred_element_type=F32) / l


def _att_kernel(latent, tq, *refs):
    if latent:
        q_ref, kn_ref, vn_ref, kc_ref, vc_ref, o_ref, kb_ref, vb_ref = refs
    else:
        q_ref, kn_ref, vn_ref, o_ref, kb_ref, vb_ref = refs

    @pl.when(pl.program_id(1) == 0)
    def _fill():
        if latent:
            kb_ref[0:PAST_LEN, :] = kc_ref[...].astype(BF16)
            vb_ref[0:PAST_LEN, :] = vc_ref[...].astype(BF16)
            kb_ref[PAST_LEN:, :] = kn_ref[...].astype(BF16)
            vb_ref[PAST_LEN:, :] = vn_ref[...].astype(BF16)
        else:
            kb_ref[...] = kn_ref[...].astype(BF16)
            vb_ref[...] = vn_ref[...].astype(BF16)

    lane_q = _lane_iota((tq, LANES))
    kb = kb_ref[...]
    vb = vb_ref[...]
    acc = []
    for g in range(ATT_KV_HEADS):
        qs = jnp.concatenate(
            [jnp.where((lane_q < 64) == (g == 0), q_ref[:, j * LANES:(j + 1) * LANES] * 0.125, 0.0).astype(BF16)
             for j in range(4)], axis=0)
        s = lax.dot_general(qs, kb, (((1,), (1,)), ((), ())), preferred_element_type=F32)
        acc.append(_softmax_pv(s, vb))
    for j in range(4):
        o_ref[:, j * LANES:(j + 1) * LANES] = jnp.where(lane_q < 64, acc[0][j * tq:(j + 1) * tq],
                                                        acc[1][j * tq:(j + 1) * tq])


def _att_mixer(aq, ak, av, bsz, seq_len, latent, k_ctx, v_ctx):
    tq = 128 if latent else 256
    lk = seq_len + (PAST_LEN if latent else 0)
    qmap = lambda b, i: (b, i, 0)
    bmap = lambda b, i: (b, 0, 0)
    in_specs = [pl.BlockSpec((None, tq, 512), qmap),
                pl.BlockSpec((None, seq_len, LANES), bmap),
                pl.BlockSpec((None, seq_len, LANES), bmap)]
    args = [aq.reshape(bsz, seq_len, 512), ak.reshape(bsz, seq_len, LANES), av.reshape(bsz, seq_len, LANES)]
    if latent:
        in_specs += [pl.BlockSpec((None, PAST_LEN, LANES), bmap)] * 2
        args += [k_ctx, v_ctx]
    return pl.pallas_call(
        functools.partial(_att_kernel, latent, tq),
        grid=(bsz, seq_len // tq),
        in_specs=in_specs,
        out_specs=pl.BlockSpec((None, tq, 512), qmap),
        out_shape=jax.ShapeDtypeStruct((bsz, seq_len, 512), F32),
        scratch_shapes=[pltpu.VMEM((lk, LANES), BF16), pltpu.VMEM((lk, LANES), BF16)],
        compiler_params=_cparams(2),
        name="att_lat" if latent else "att_ctx",
    )(*args)


def _mla_kernel(latent, tq, *refs):
    if latent:
        (mq_ref, cn_ref, rn_ref, cc_ref, rc_ref, qn_ref, wq_ref, wk_ref, wv_ref,
         cos_ref, su_ref, sd_ref, o_ref, kb_ref, vb_ref) = refs
    else:
        (mq_ref, cn_ref, rn_ref, qn_ref, wq_ref, wk_ref, wv_ref, o_ref, kb_ref, vb_ref) = refs

    def fill(r0, ckv, kr):
        cb = ckv.astype(BF16)
        n = ckv.shape[0]
        kn = jnp.dot(cb, wk_ref[...], preferred_element_type=F32)
        for h in range(MLA_HEADS):
            kb_ref[r0:r0 + n, h * LANES:(h + 1) * LANES] = (kn[:, h * LANES:(h + 1) * LANES] + kr).astype(BF16)
        vb_ref[r0:r0 + n, :] = jnp.dot(cb, wv_ref[...], preferred_element_type=F32).astype(BF16)

    @pl.when(pl.program_id(1) == 0)
    def _fill():
        if latent:
            fill(0, cc_ref[...], rc_ref[...])
            fill(PAST_LEN, cn_ref[...], rn_ref[...])
        else:
            fill(0, cn_ref[...], rn_ref[...])

    q = _bdot(_rms(mq_ref[...], qn_ref[...]), wq_ref[...])
    lane_q = _lane_iota((tq, LANES))
    scale = (MLA_NOPE + MLA_ROPE) ** -0.5
    for p in range(4):
        halves = []
        for h in (2 * p, 2 * p + 1):
            qh = q[:, h * LANES:(h + 1) * LANES]
            if latent:
                qh = _rope(qh, cos_ref[...], su_ref[...], sd_ref[...], 8)
            s = _bdot_nt(qh, kb_ref[:, h * LANES:(h + 1) * LANES]) * scale
            halves.append(_softmax_pv(s, vb_ref[:, p * LANES:(p + 1) * LANES]))
        o_ref[:, p * LANES:(p + 1) * LANES] = jnp.where(lane_q < 64, halves[0], halves[1])


def _mla_mixer(mq, ckv, kr, bsz, seq_len, latent, ckv_ctx, kr_ctx, qn, wq, wk, wv, tabs):
    tq = 256
    lk = seq_len + (PAST_LEN if latent else 0)
    qmap = lambda b, i: (b, i, 0)
    bmap = lambda b, i: (b, 0, 0)
    const = lambda b, i: (0, 0)
    in_specs = [pl.BlockSpec((None, tq, MLA_Q_RANK), qmap),
                pl.BlockSpec((None, seq_len, MLA_KV_RANK), bmap),
                pl.BlockSpec((None, seq_len, LANES), bmap)]
    args = [mq.reshape(bsz, seq_len, MLA_Q_RANK), ckv.reshape(bsz, seq_len, MLA_KV_RANK),
            kr.reshape(bsz, seq_len, LANES)]
    if latent:
        in_specs += [pl.BlockSpec((None, PAST_LEN, MLA_KV_RANK), bmap),
                     pl.BlockSpec((None, PAST_LEN, LANES), bmap)]
        args += [ckv_ctx, kr_ctx]
    in_specs += [pl.BlockSpec((1, MLA_Q_RANK), const),
                 pl.BlockSpec((MLA_Q_RANK, 1024), const),
                 pl.BlockSpec((MLA_KV_RANK, 1024), const),
                 pl.BlockSpec((MLA_KV_RANK, 512), const)]
    args += [qn, wq, wk, wv]
    if latent:
        in_specs += [pl.BlockSpec((tq, LANES), lambda b, i: (i, 0))] * 3
        args += list(tabs)
    return pl.pallas_call(
        functools.partial(_mla_kernel, latent, tq),
        grid=(bsz, seq_len // tq),
        in_specs=in_specs,
        out_specs=pl.BlockSpec((None, tq, 512), qmap),
        out_shape=jax.ShapeDtypeStruct((bsz, seq_len, 512), F32),
        scratch_shapes=[pltpu.VMEM((lk, 1024), BF16), pltpu.VMEM((lk, 512), BF16)],
        compiler_params=_cparams(2),
        name="mla_lat" if latent else "mla_ctx",
    )(*args)


def _merge_kernel(x_ref, sh_ref, sc_ref, ga_ref, gpre_ref, gpost_ref, o1_ref, o2_ref, o3_ref, o4_ref,
                  wm_ref, bm_ref, wb_ref, wo_ref, y_ref):
    x = x_ref[...]
    hb = (_rms(x, gpre_ref[...]) * (1.0 + sc_ref[...]) + sh_ref[...]).astype(BF16)
    merged = None
    for k, o_ref in enumerate((o1_ref, o2_ref, o3_ref, o4_ref)):
        cols = slice(k * D_MODEL, (k + 1) * D_MODEL)
        gate = jax.nn.sigmoid(jnp.dot(hb, wm_ref[:, cols], preferred_element_type=F32) + bm_ref[:, cols])
        term = gate * _bdot(o_ref[...], wb_ref[k])
        merged = term if merged is None else merged + term
    m = _bdot(merged, wo_ref[...])
    y_ref[...] = x + ga_ref[...] * _rms(m, gpost_ref[...])


def _merge(x2d, mod5, layer, latent, seq_len, g_pre, g_post, branches, w_merge, b_merge, w_br, w_out):
    n = x2d.shape[0]
    tm = TM_ROWS if seq_len % TM_ROWS == 0 else seq_len
    row = lambda i: (i, 0)
    const = lambda i: (0, 0)
    in_specs = ([pl.BlockSpec((tm, D_MODEL), row)]
                + [_mod_spec(layer, latent, seq_len, tm, k) for k in (0, 1, 2)]
                + [pl.BlockSpec((1, D_MODEL), const)] * 2
                + [pl.BlockSpec((tm, 512), row)] * 4
                + [pl.BlockSpec((D_MODEL, 4 * D_MODEL), const),
                   pl.BlockSpec((1, 4 * D_MODEL), const),
                   pl.BlockSpec((4, 512, D_MODEL), lambda i: (0, 0, 0)),
                   pl.BlockSpec((D_MODEL, D_MODEL), const)])
    return pl.pallas_call(
        _merge_kernel,
        grid=(n // tm,),
        in_specs=in_specs,
        out_specs=pl.BlockSpec((tm, D_MODEL), row),
        out_shape=jax.ShapeDtypeStruct((n, D_MODEL), F32),
        compiler_params=_cparams(1),
        name="merge_lat" if latent else "merge_ctx",
    )(x2d, mod5, mod5, mod5, g_pre, g_post, *[b.reshape(n, 512) for b in branches],
      w_merge, b_merge, w_br, w_out)


def _ffn_kernel(tm, tiles_per_seq, x_ref, xp_ref, xn_ref, sh_ref, sc_ref, gf_ref, gpre_ref, gpost_ref,
                wu_ref, cw_ref, cb_ref, wd_ref, y_ref):
    i = pl.program_id(0)
    first = (i % tiles_per_seq) == 0
    last = (i % tiles_per_seq) == tiles_per_seq - 1

    def nm(x):
        return _rms(x, gpre_ref[...]) * (1.0 + sc_ref[...]) + sh_ref[...]

    x = x_ref[...]
    h_prev = jnp.where(first, 0.0, nm(xp_ref[...]))
    h_next = jnp.where(last, 0.0, nm(xn_ref[...]))
    hb = jnp.concatenate([h_prev, nm(x), h_next], axis=0).astype(BF16)
    rows = tm + 2 * SUBLANES

    def conv_cols(c0):
        p = jnp.dot(hb, wu_ref[:, c0:c0 + FF_CHUNK], preferred_element_type=F32)
        w = cw_ref[:, c0:c0 + FF_CHUNK]
        u = (pltpu.roll(p, 1, 0) * w[0:1] + p * w[1:2] + pltpu.roll(p, rows - 1, 0) * w[2:3])
        return u[SUBLANES:SUBLANES + tm] + cb_ref[:, c0:c0 + FF_CHUNK]

    acc = None
    for j in range(D_FF // FF_CHUNK):
        up = conv_cols(j * FF_CHUNK)
        gate = conv_cols(D_FF + j * FF_CHUNK)
        part = _bdot(_silu(gate) * up, wd_ref[j * FF_CHUNK:(j + 1) * FF_CHUNK, :])
        acc = part if acc is None else acc + part
    y_ref[...] = x + gf_ref[...] * _rms(acc, gpost_ref[...])


def _ffn(x2d, mod5, layer, latent, seq_len, g_pre, g_post, w_up, conv_w, conv_b, w_down):
    n = x2d.shape[0]
    tm = TM_ROWS if seq_len % TM_ROWS == 0 else seq_len
    tiles_per_seq = seq_len // tm
    hb = tm // SUBLANES
    n_hb = n // SUBLANES
    row = lambda i: (i, 0)
    const = lambda i: (0, 0)
    in_specs = ([pl.BlockSpec((tm, D_MODEL), row),
                 pl.BlockSpec((SUBLANES, D_MODEL), lambda i: (jnp.maximum(i * hb - 1, 0), 0)),
                 pl.BlockSpec((SUBLANES, D_MODEL), lambda i: (jnp.minimum((i + 1) * hb, n_hb - 1), 0))]
                + [_mod_spec(layer, latent, seq_len, tm, k) for k in (3, 4, 5)]
                + [pl.BlockSpec((1, D_MODEL), const)] * 2
                + [pl.BlockSpec((D_MODEL, 2 * D_FF), const),
                   pl.BlockSpec((3, 2 * D_FF), const),
                   pl.BlockSpec((1, 2 * D_FF), const),
                   pl.BlockSpec((D_FF, D_MODEL), const)])
    return pl.pallas_call(
        functools.partial(_ffn_kernel, tm, tiles_per_seq),
        grid=(n // tm,),
        in_specs=in_specs,
        out_specs=pl.BlockSpec((tm, D_MODEL), row),
        out_shape=jax.ShapeDtypeStruct((n, D_MODEL), F32),
        compiler_params=_cparams(1),
        name="ffn_lat" if latent else "ffn_ctx",
    )(x2d, x2d, x2d, mod5, mod5, mod5, g_pre, g_post, w_up, conv_w, conv_b, w_down)


def _rope_tables(seq_len, head_dim, lane_of_dim):
    d_axis = head_dim // 2
    t = np.arange(seq_len)
    pos = np.stack([(t // GRID_W).astype(np.float32), (t % GRID_W).astype(np.float32)], axis=0)
    inv_freq = (np.float32(ROPE_BASE) ** (-np.arange(0, d_axis, 2, dtype=np.float32) / np.float32(d_axis))).astype(np.float32)
    cos = np.ones((seq_len, LANES), np.float32)
    s_up = np.zeros((seq_len, LANES), np.float32)
    s_dn = np.zeros((seq_len, LANES), np.float32)
    for lane, d in enumerate(lane_of_dim):
        if d < 0:
            continue
        axis, j = divmod(d, d_axis)
        second = j >= d_axis // 2
        ang = (pos[axis] * inv_freq[j % (d_axis // 2)]).astype(np.float32)
        cos[:, lane] = np.cos(ang)
        if second:
            s_up[:, lane] = np.sin(ang)
        else:
            s_dn[:, lane] = -np.sin(ang)
    return jnp.asarray(cos), jnp.asarray(s_up), jnp.asarray(s_dn)


def _ssd_expand_matrix():
    e = np.zeros((LANES, 1024), np.float32)
    for d in range(2):
        for h in range(SSD_HEADS):
            e[d * SSD_HEADS + h, d * 512 + h * 64:d * 512 + (h + 1) * 64] = 1.0
    return jnp.asarray(e, dtype=BF16)


def _layer_params(i, w_in, ssd_conv_w, ssd_conv_b, ssd_dt_bias, ssd_a_log, ssd_d, ssd_norm_w,
                  ret_decay_logit, ret_gn_w, att_q_norm, att_k_norm, mla_q_norm, mla_w_uq, mla_kv_norm,
                  mla_w_ukv, w_br_ssd, w_br_ret, w_br_att, w_br_mla, w_merge, b_merge, w_out,
                  w_ffn_up, ffn_conv_w, ffn_conv_b, w_ffn_down, g_pre_mix, g_post_mix, g_pre_ffn,
                  g_post_ffn):
    w = w_in[i]
    off = np.cumsum([0, 512, 768, 16, 256, 256, 512, 512, 512, 128, 128, 384, 288])
    col = lambda k: w[:, off[k]:off[k + 1]]
    zeros = lambda n: jnp.zeros((D_MODEL, n), F32)
    aq = col(7)
    aq_pairs = [jnp.concatenate([aq[:, j * 64:(j + 1) * 64], aq[:, (j + 4) * 64:(j + 5) * 64]], axis=1)
                for j in range(4)]
    mckv = col(11)
    w_in_p = jnp.concatenate(
        [col(0), col(1), col(2), zeros(112),
         col(3), col(4), col(5), col(6),
         *aq_pairs, col(8), col(9),
         col(10), mckv[:, :MLA_KV_RANK], zeros(64), mckv[:, MLA_KV_RANK:], zeros(32)],
        axis=1).astype(BF16)

    uq = mla_w_uq[i].reshape(MLA_Q_RANK, MLA_HEADS, MLA_NOPE + MLA_ROPE)
    wq = jnp.pad(uq, ((0, 0), (0, 0), (0, 32))).reshape(MLA_Q_RANK, 1024).astype(BF16)
    ukv = mla_w_ukv[i].reshape(MLA_KV_RANK, MLA_HEADS, MLA_NOPE + MLA_V)
    wk = jnp.pad(ukv[:, :, :MLA_NOPE], ((0, 0), (0, 0), (0, 64))).reshape(MLA_KV_RANK, 1024).astype(BF16)
    wv = ukv[:, :, MLA_NOPE:].reshape(MLA_KV_RANK, 512).astype(BF16)

    perm = np.concatenate([np.r_[j * 64:(j + 1) * 64, (j + 4) * 64:(j + 5) * 64] for j in range(4)])
    w_br = jnp.stack([w_br_ssd[i], w_br_ret[i], w_br_att[i][perm], w_br_mla[i]], axis=0).astype(BF16)

    pad_row = lambda v: jnp.pad(v.reshape(1, -1), ((0, 0), (0, LANES - v.size)))
    return dict(
        w_in=w_in_p,
        qn=jnp.tile(att_q_norm[i], 2).reshape(1, LANES),
        kn=jnp.tile(att_k_norm[i], 2).reshape(1, LANES),
        cn=mla_kv_norm[i].reshape(1, MLA_KV_RANK),
        ssd_cw=ssd_conv_w[i], ssd_cb=ssd_conv_b[i].reshape(1, SSD_CONV_CH),
        ssd_dtb=pad_row(ssd_dt_bias[i]), ssd_alog=pad_row(ssd_a_log[i]),
        ssd_dsk=jnp.repeat(ssd_d[i], SSD_HEAD_DIM).reshape(1, SSD_INNER),
        ssd_nw=ssd_norm_w[i].reshape(1, SSD_INNER),
        ret_logit=jnp.broadcast_to(ret_decay_logit[i].reshape(SUBLANES, 1), (SUBLANES, LANES)),
        ret_gn=ret_gn_w[i].reshape(1, 512),
        mla_qn=mla_q_norm[i].reshape(1, MLA_Q_RANK), wq=wq, wk=wk, wv=wv,
        w_merge=w_merge[i].astype(BF16), b_merge=b_merge[i].reshape(1, 4 * D_MODEL),
        w_br=w_br, w_out=w_out[i].astype(BF16),
        w_up=w_ffn_up[i].astype(BF16), ffn_cw=ffn_conv_w[i], ffn_cb=ffn_conv_b[i].reshape(1, 2 * D_FF),
        w_down=w_ffn_down[i].astype(BF16),
        g_pre_mix=g_pre_mix[i].reshape(1, D_MODEL), g_post_mix=g_post_mix[i].reshape(1, D_MODEL),
        g_pre_ffn=g_pre_ffn[i].reshape(1, D_MODEL), g_post_ffn=g_post_ffn[i].reshape(1, D_MODEL),
    )


def _ssd_state_to_pairs(s):
    b = s.shape[0]
    s = s.reshape(b, 2, 4, 2, 64, 64).transpose(0, 2, 4, 1, 3, 5)
    return s.reshape(b, 4, 64, 256)


def _ssd_pairs_to_state(s):
    b = s.shape[0]
    return s.reshape(b, 4, 64, 2, 2, 64).transpose(0, 3, 1, 4, 2, 5).reshape(b, 2, SSD_HEADS, 64, 64)


def _ret_state_to_lanes(s):
    b = s.shape[0]
    return s.transpose(0, 2, 3, 1, 4).reshape(b, RET_HEADS, 64, 256)


def _ret_lanes_to_state(s):
    b = s.shape[0]
    return s.reshape(b, RET_HEADS, 64, 2, RET_V).transpose(0, 3, 1, 2, 4)


def _trunk_pass(x2d, mod5, layer, latent, bsz, seq_len, lp, tabs64, tabs_mla, expand, ctx):
    (ssd_in, ret_in, aq, ak, av, mq, ckv, kr) = _inproj(
        x2d, mod5, layer, latent, seq_len, lp['g_pre_mix'], lp['w_in'], lp['qn'], lp['kn'], lp['cn'],
        (tabs64 + tabs_mla) if latent else None)
    ssd_res = _ssd_mixer(ssd_in, bsz, seq_len, latent, lp['ssd_cw'], lp['ssd_cb'], lp['ssd_dtb'],
                         lp['ssd_alog'], lp['ssd_dsk'], lp['ssd_nw'], expand,
                         ctx['ssd'] if latent else None)
    ret_res = _ret_mixer(ret_in, bsz, seq_len, latent, lp['ret_logit'], lp['ret_gn'],
                         ctx['ret'] if latent else None)
    o_att = _att_mixer(aq, ak, av, bsz, seq_len, latent,
                       ctx['att_k'] if latent else None, ctx['att_v'] if latent else None)
    o_mla = _mla_mixer(mq, ckv, kr, bsz, seq_len, latent,
                       ctx['mla_ckv'] if latent else None, ctx['mla_kr'] if latent else None,
                       lp['mla_qn'], lp['wq'], lp['wk'], lp['wv'], tabs_mla)
    x2d = _merge(x2d, mod5, layer, latent, seq_len, lp['g_pre_mix'], lp['g_post_mix'],
                 (ssd_res[0], ret_res[0], o_att, o_mla), lp['w_merge'], lp['b_merge'], lp['w_br'], lp['w_out'])
    x2d = _ffn(x2d, mod5, layer, latent, seq_len, lp['g_pre_ffn'], lp['g_post_ffn'],
               lp['w_up'], lp['ffn_cw'], lp['ffn_cb'], lp['w_down'])
    cache = None
    if not latent:
        cache = dict(ssd=_ssd_pairs_to_state(ssd_res[1]), ret=_ret_lanes_to_state(ret_res[1]),
                     att_k=ak.reshape(bsz, seq_len, ATT_KV_HEADS, ATT_HEAD_DIM),
                     att_v=av.reshape(bsz, seq_len, ATT_KV_HEADS, ATT_HEAD_DIM),
                     mla_ckv=ckv.reshape(bsz, seq_len, MLA_KV_RANK),
                     mla_krope=kr.reshape(bsz, seq_len, LANES)[:, :, 64:64 + MLA_ROPE])
    return x2d, cache


def kernel(x_prompt, x_sample, state_ssd, state_ret, cache_att_k, cache_att_v, cache_mla_ckv, cache_mla_krope, c, c_ctx, w_mod, b_mod, g_pre_mix, g_post_mix, g_pre_ffn, g_post_ffn, w_in, ssd_conv_w, ssd_conv_b, ssd_dt_bias, ssd_a_log, ssd_d, ssd_norm_w, ret_decay_logit, ret_gn_w, att_q_norm, att_k_norm, mla_q_norm, mla_w_uq, mla_kv_norm, mla_w_ukv, w_br_ssd, w_br_ret, w_br_att, w_br_mla, w_merge, b_merge, w_out, w_ffn_up, ffn_conv_w, ffn_conv_b, w_ffn_down):
    cvec = jnp.concatenate([c_ctx[None, :], c, jnp.zeros((SUBLANES - 1 - DEC_BATCH, D_MODEL), F32)], axis=0)
    mod5 = _modulation(cvec, w_mod, b_mod).reshape(DEPTH, SUBLANES, 6, 1, D_MODEL)

    lane_dims_64 = [d % 64 for d in range(LANES)]
    lane_dims_mla = [d - 64 if 64 <= d < 96 else -1 for d in range(LANES)]
    tabs64 = _rope_tables(DEC_SEQ, ATT_HEAD_DIM, lane_dims_64)
    tabs_mla = _rope_tables(DEC_SEQ, MLA_ROPE, lane_dims_mla)
    expand = _ssd_expand_matrix()

    xp = x_prompt.reshape(BATCH * SEQ, D_MODEL)
    xs = x_sample.reshape(DEC_BATCH * DEC_SEQ, D_MODEL)
    caches = []
    for i in range(DEPTH):
        lp = _layer_params(i, w_in, ssd_conv_w, ssd_conv_b, ssd_dt_bias, ssd_a_log, ssd_d, ssd_norm_w,
                           ret_decay_logit, ret_gn_w, att_q_norm, att_k_norm, mla_q_norm, mla_w_uq,
                           mla_kv_norm, mla_w_ukv, w_br_ssd, w_br_ret, w_br_att, w_br_mla, w_merge, b_merge,
                           w_out, w_ffn_up, ffn_conv_w, ffn_conv_b, w_ffn_down, g_pre_mix, g_post_mix,
                           g_pre_ffn, g_post_ffn)
        xp, cache = _trunk_pass(xp, mod5, i, False, BATCH, SEQ, lp, tabs64, tabs_mla, expand, None)
        caches.append(cache)
        ctx = dict(ssd=_ssd_state_to_pairs(state_ssd[:, i]), ret=_ret_state_to_lanes(state_ret[:, i]),
                   att_k=cache_att_k[:, i].reshape(DEC_BATCH, PAST_LEN, LANES),
                   att_v=cache_att_v[:, i].reshape(DEC_BATCH, PAST_LEN, LANES),
                   mla_ckv=cache_mla_ckv[:, i],
                   mla_kr=jnp.pad(cache_mla_krope[:, i], ((0, 0), (0, 0), (64, 32))))
        xs, _ = _trunk_pass(xs, mod5, i, True, DEC_BATCH, DEC_SEQ, lp, tabs64, tabs_mla, expand, ctx)

    stack = lambda k: jnp.stack([cc[k] for cc in caches], axis=1)
    return (xp.reshape(BATCH, SEQ, D_MODEL), xs.reshape(DEC_BATCH, DEC_SEQ, D_MODEL),
            stack('ssd'), stack('ret'), stack('att_k'), stack('att_v'), stack('mla_ckv'), stack('mla_krope'))
```

```python
import functools

import numpy as np
import jax
import jax.numpy as jnp
from jax import lax
from jax.experimental import pallas as pl
from jax.experimental.pallas import tpu as pltpu

D_MODEL = 1024
BATCH = 32
SEQ = 256
DEPTH = 2
DEC_BATCH = 4
DEC_SEQ = 2048
PAST_LEN = 256
GRID_W = 64
ROPE_BASE = 10000.0
NORM_EPS = 1e-6
SCAN_CHUNK = 128
SSD_HEADS = 8
SSD_HEAD_DIM = 64
SSD_INNER = 512
SSD_GROUPS = 2
SSD_STATE = 64
SSD_CONV_CH = 768
RET_HEADS = 4
RET_QK = 64
RET_V = 128
ATT_HEADS = 8
ATT_KV_HEADS = 2
ATT_HEAD_DIM = 64
MLA_HEADS = 8
MLA_Q_RANK = 384
MLA_KV_RANK = 256
MLA_NOPE = 64
MLA_ROPE = 32
MLA_V = 64
D_FF = 2816

F32 = jnp.float32
BF16 = jnp.bfloat16
LANES = 128
SUBLANES = 8
VMEM_LIMIT = 56 * 1024 * 1024
NEG_BIG = -1e30

SSD_W = 1408
RET_W = 1536
TM_ROWS = 512
ATT_SUB_ROWS = 128
FF_HALF = 1408


def _cparams(n_axes):
    return pltpu.CompilerParams(dimension_semantics=("arbitrary",) * n_axes,
                                vmem_limit_bytes=VMEM_LIMIT)


def _silu(x):
    return x * jax.nn.sigmoid(x)


def _softplus(x):
    return jnp.maximum(x, 0.0) + jnp.log1p(jnp.exp(-jnp.abs(x)))


def _rms(x, w):
    ms = jnp.mean(x * x, axis=-1, keepdims=True)
    return x * lax.rsqrt(ms + NORM_EPS) * w


def _bdot(a, b):
    return jnp.dot(a.astype(BF16), b.astype(BF16), preferred_element_type=F32)


def _bdot_nt(a, b):
    return lax.dot_general(a.astype(BF16), b.astype(BF16), (((1,), (1,)), ((), ())),
                           preferred_element_type=F32)


def _bdot_tn(a, b):
    return lax.dot_general(a.astype(BF16), b.astype(BF16), (((0,), (0,)), ((), ())),
                           preferred_element_type=F32)


def _split_dot(a, b_bf16):
    hi = a.astype(BF16)
    lo = (a - hi.astype(F32)).astype(BF16)
    return (jnp.dot(hi, b_bf16, preferred_element_type=F32)
            + jnp.dot(lo, b_bf16, preferred_element_type=F32))


def _split_dot_left(a_bf16, b):
    hi = b.astype(BF16)
    lo = (b - hi.astype(F32)).astype(BF16)
    return (jnp.dot(a_bf16, hi, preferred_element_type=F32)
            + jnp.dot(a_bf16, lo, preferred_element_type=F32))


def _lane_iota(shape):
    return lax.broadcasted_iota(jnp.int32, shape, len(shape) - 1)


def _row_iota(shape):
    return lax.broadcasted_iota(jnp.int32, shape, 0)


def _rope(xb, cos, sin_up, sin_dn, half):
    return (xb * cos + pltpu.roll(xb, half, 1) * sin_up
            + pltpu.roll(xb, LANES - half, 1) * sin_dn)


def _half_rms(xb, w_row):
    lo = _lane_iota(xb.shape) < 64
    sq = xb * xb
    s_lo = jnp.sum(jnp.where(lo, sq, 0.0), axis=-1, keepdims=True)
    s_hi = jnp.sum(jnp.where(lo, 0.0, sq), axis=-1, keepdims=True)
    ms = jnp.where(lo, s_lo, s_hi) * (1.0 / 64.0)
    return xb * lax.rsqrt(ms + NORM_EPS) * w_row


def _mod_kernel(c_ref, w_ref, b_ref, o_ref):
    c = c_ref[...]
    o_ref[...] = _bdot(_silu(c), w_ref[...]) + b_ref[...]


def _modulation(cvec, w_mod, b_mod):
    tn = 1536
    n_out = 6 * D_MODEL
    return pl.pallas_call(
        _mod_kernel,
        grid=(DEPTH, n_out // tn),
        in_specs=[pl.BlockSpec((SUBLANES, D_MODEL), lambda l, j: (0, 0)),
                  pl.BlockSpec((None, D_MODEL, tn), lambda l, j: (l, 0, j)),
                  pl.BlockSpec((None, 1, tn), lambda l, j: (l, 0, j))],
        out_specs=pl.BlockSpec((None, SUBLANES, tn), lambda l, j: (l, 0, j)),
        out_shape=jax.ShapeDtypeStruct((DEPTH, SUBLANES, n_out), F32),
        compiler_params=_cparams(2),
        name="modulation",
    )(cvec, w_mod, b_mod.reshape(DEPTH, 1, n_out))


def _mod_spec(layer, latent, seq_len, tm, k):
    if latent:
        assert seq_len % tm == 0
    tiles_per_seq = max(seq_len // tm, 1)

    def index_map(i):
        row = 1 + i // tiles_per_seq if latent else 0
        return (layer, row, k, 0, 0)

    return pl.BlockSpec((None, None, None, 1, D_MODEL), index_map)


def _resident(block_shape, index_map):
    return pl.BlockSpec(block_shape, index_map, pipeline_mode=pl.Buffered(1))


def _inproj_kernel(latent, *refs):
    if latent:
        (x_ref, sh_ref, sc_ref, g_ref, w_ref, qn_ref, kn_ref, cn_ref,
         c64_ref, su64_ref, sd64_ref, cm_ref, sum_ref, sdm_ref,
         ssd_ref, ret_ref, aq_ref, ak_ref, av_ref, mq_ref, ckv_ref, kr_ref) = refs
    else:
        (x_ref, sh_ref, sc_ref, g_ref, w_ref, qn_ref, kn_ref, cn_ref,
         ssd_ref, ret_ref, aq_ref, ak_ref, av_ref, mq_ref, ckv_ref, kr_ref) = refs

    half = x_ref.shape[0] // 2
    for r in range(2):
        rows = slice(r * half, (r + 1) * half)
        h = _rms(x_ref[rows, :], g_ref[...]) * (1.0 + sc_ref[...]) + sh_ref[...]
        hb = h.astype(BF16)

        def proj(c0, width):
            return jnp.dot(hb, w_ref[:, c0:c0 + width], preferred_element_type=F32)

        def rope64(xb):
            if not latent:
                return xb
            return _rope(xb, c64_ref[rows, :], su64_ref[rows, :], sd64_ref[rows, :], 16)

        pa = proj(SSD_W + RET_W, 1536)
        for j in range(4):
            q = _half_rms(pa[:, j * LANES:(j + 1) * LANES], qn_ref[...])
            aq_ref[rows, j * LANES:(j + 1) * LANES] = rope64(q)
        ak_ref[rows, :] = rope64(_half_rms(pa[:, 512:640], kn_ref[...]))
        av_ref[rows, :] = pa[:, 640:768]
        mq_ref[rows, :] = pa[:, 768:768 + MLA_Q_RANK]
        ckv_ref[rows, :] = _rms(pa[:, 768 + MLA_Q_RANK:768 + MLA_Q_RANK + MLA_KV_RANK], cn_ref[...])
        kr = pa[:, 1408:1536]
        if latent:
            kr = _rope(kr, cm_ref[rows, :], sum_ref[rows, :], sdm_ref[rows, :], 8)
        kr_ref[rows, :] = kr

        pr = proj(SSD_W, RET_W)
        for j in range(2):
            ret_ref[rows, j * LANES:(j + 1) * LANES] = rope64(pr[:, j * LANES:(j + 1) * LANES])
        for j in range(2, 4):
            ret_ref[rows, j * LANES:(j + 1) * LANES] = rope64(pr[:, j * LANES:(j + 1) * LANES] * 0.125)
        ret_ref[rows, 512:RET_W] = pr[:, 512:RET_W]

        ssd_ref[rows, :] = proj(0, SSD_W)


IN_TOTAL = SSD_W + RET_W + 768 + 768


def _inproj(x2d, mod5, layer, latent, seq_len, g_pre, w_in_p, qn, kn, cn, tabs):
    n = x2d.shape[0]
    tm = TM_ROWS
    tiles_per_seq = max(seq_len // tm, 1)
    row = lambda i: (i, 0)
    const = lambda i: (0, 0)
    in_specs = [pl.BlockSpec((tm, D_MODEL), row),
                _mod_spec(layer, latent, seq_len, tm, 0),
                _mod_spec(layer, latent, seq_len, tm, 1),
                pl.BlockSpec((1, D_MODEL), const),
                _resident((None, D_MODEL, IN_TOTAL), lambda i: (layer, 0, 0)),
                pl.BlockSpec((1, LANES), const),
                pl.BlockSpec((1, LANES), const),
                pl.BlockSpec((1, MLA_KV_RANK), const)]
    args = [x2d, mod5, mod5, g_pre, w_in_p, qn, kn, cn]
    if latent:
        tab = pl.BlockSpec((tm, LANES), lambda i: (i % tiles_per_seq, 0))
        in_specs += [tab] * 6
        args += list(tabs)
    widths = (SSD_W, RET_W, 512, LANES, LANES, MLA_Q_RANK, MLA_KV_RANK, LANES)
    return pl.pallas_call(
        functools.partial(_inproj_kernel, latent),
        grid=(n // tm,),
        in_specs=in_specs,
        out_specs=[pl.BlockSpec((tm, w), row) for w in widths],
        out_shape=[jax.ShapeDtypeStruct((n, w), F32) for w in widths],
        compiler_params=_cparams(1),
        name="inproj_lat" if latent else "inproj_ctx",
    )(*args)


def _ssd_kernel(latent, seq_len, *refs):
    if latent:
        (in_ref, cw_ref, cb_ref, dtb_ref, alog_ref, dsk_ref, nw_ref, ex_ref, s0_ref,
         o_ref, xc_ref, dt_ref, cum_ref, sf_ref, sb_ref, cdb_ref, run_ref) = refs
    else:
        (in_ref, cw_ref, cb_ref, dtb_ref, alog_ref, dsk_ref, nw_ref, ex_ref,
         o_ref, st_ref, xc_ref, dt_ref, cum_ref, sf_ref, sb_ref, cdb_ref, run_ref) = refs
    T = SCAN_CHUNK
    nc = seq_len // T
    ii = _row_iota((T, T))
    jj = _lane_iota((T, T))
    tri_lo = (jj <= ii).astype(BF16)
    tri_up = (jj >= ii).astype(BF16)
    lane_c = _lane_iota((T, LANES))
    row_c = _row_iota((T, SSD_CONV_CH))
    fwd_lane = lane_c < SSD_HEADS
    a_row = -jnp.exp(alog_ref[...])
    expand = ex_ref[...]

    if latent:
        run_ref[...] = s0_ref[...]
    else:
        run_ref[...] = jnp.zeros(run_ref.shape, F32)

    def phase0(c, carry):
        r0 = pl.multiple_of(c * T, T)
        x_cur = in_ref[pl.ds(r0, T), 512:1280]
        prev_blk = in_ref[pl.ds(pl.multiple_of(jnp.maximum(r0 - SUBLANES, 0), SUBLANES), SUBLANES), 512:1280]
        next_blk = in_ref[pl.ds(pl.multiple_of(jnp.minimum(r0 + T, seq_len - SUBLANES), SUBLANES), SUBLANES), 512:1280]
        prev_row = jnp.where(c > 0, prev_blk[SUBLANES - 1:SUBLANES, :], 0.0)
        next_row = jnp.where(c < nc - 1, next_blk[0:1, :], 0.0)
        x_prev = jnp.where(row_c == 0, prev_row, pltpu.roll(x_cur, 1, 0))
        x_next = jnp.where(row_c == T - 1, next_row, pltpu.roll(x_cur, T - 1, 0))
        xc = _silu(cb_ref[...] + x_prev * cw_ref[0:1, :] + x_cur * cw_ref[1:2, :] + x_next * cw_ref[2:3, :])
        xc_ref[pl.ds(r0, T), :] = xc

        dt = _softplus(in_ref[pl.ds(r0, T), 1280:SSD_W] + dtb_ref[...])
        la = dt * a_row
        cum = jnp.where(fwd_lane, _split_dot_left(tri_lo, la), _split_dot_left(tri_up, la))
        tot = jnp.where(fwd_lane[0:1], cum[T - 1:T, :], cum[0:1, :])
        dt_ref[pl.ds(r0, T), :] = dt
        cum_ref[pl.ds(r0, T), :] = cum
        w = dt * jnp.exp(tot - cum)
        wcd = jnp.concatenate([w, jnp.broadcast_to(jnp.exp(tot), (SUBLANES, LANES))], axis=0)
        wcd_x = _split_dot(wcd, expand)
        w_x = wcd_x[0:T]
        cd_x = wcd_x[T:T + 1]
        cdb_ref[c] = jnp.broadcast_to(cd_x[:, 512:1024], (SUBLANES, 512))
        xs = xc[:, 0:512]
        bm = xc[:, 512:640].astype(BF16)
        for p in range(4):
            g = p // 2
            sl = slice(p * LANES, (p + 1) * LANES)
            vw = jnp.concatenate([xs[:, sl] * w_x[:, sl], xs[:, sl] * w_x[:, 512 + p * LANES:512 + (p + 1) * LANES]],
                                 axis=1)
            cs = _bdot_tn(bm, vw)[g * 64:(g + 1) * 64]
            s_run = run_ref[p]
            sf_ref[c, p] = s_run[:, 0:LANES]
            sb_ref[c, p] = cs[:, LANES:2 * LANES]
            new_f = s_run[:, 0:LANES] * cd_x[:, sl] + cs[:, 0:LANES]
            run_ref[p] = jnp.concatenate([new_f, s_run[:, LANES:2 * LANES]], axis=1)
        return carry

    lax.fori_loop(0, nc, phase0, 0)

    def bwd_states(k, carry):
        c = nc - 1 - k
        cd = cdb_ref[c]
        for p in range(4):
            s_run = run_ref[p]
            cs_b = sb_ref[c, p]
            sb_ref[c, p] = s_run[:, LANES:2 * LANES]
            new_b = s_run[:, LANES:2 * LANES] * cd[0:1, p * LANES:(p + 1) * LANES] + cs_b
            run_ref[p] = jnp.concatenate([s_run[:, 0:LANES], new_b], axis=1)
        return carry

    lax.fori_loop(0, nc, bwd_states, 0)
    if not latent:
        for p in range(4):
            s_fin = run_ref[p]
            for d in range(2):
                for hh in range(2):
                    c0 = d * LANES + hh * 64
                    st_ref[d, 2 * p + hh] = s_fin[:, c0:c0 + 64]

    lower = jj <= ii
    upper = jj >= ii
    zeros64 = jnp.zeros((64, 2 * LANES), F32)

    def phase1(c, carry):
        r0 = pl.multiple_of(c * T, T)
        xc = xc_ref[pl.ds(r0, T), :]
        xs = xc[:, 0:512]
        bm = xc[:, 512:640].astype(BF16)
        cm = xc[:, 640:768]
        dt = dt_ref[pl.ds(r0, T), :]
        cum = cum_ref[pl.ds(r0, T), :]
        cum_t = cum.T
        dt_t = dt.T
        e_x = _split_dot(jnp.exp(cum), expand)
        gmat = [_bdot_nt(jnp.where((lane_c < 64) == (g == 0), cm, 0.0), bm) for g in range(2)]
        cmb = cm.astype(BF16)
        outs = []
        for p in range(4):
            g = p // 2
            sl = slice(p * LANES, (p + 1) * LANES)
            xs_p = xs[:, sl].astype(BF16)
            halves = []
            for h in (2 * p, 2 * p + 1):
                df = jnp.exp(jnp.where(lower, cum[:, h:h + 1] - cum_t[h:h + 1, :], NEG_BIG)) * dt_t[h:h + 1, :]
                hb = SSD_HEADS + h
                db = jnp.exp(jnp.where(upper, cum[:, hb:hb + 1] - cum_t[hb:hb + 1, :], NEG_BIG)) * dt_t[hb:hb + 1, :]
                m_h = (gmat[g] * (df + db)).astype(BF16)
                halves.append(jnp.dot(m_h, xs_p, preferred_element_type=F32))
            o_p = jnp.where(lane_c < 64, halves[0], halves[1])
            s_in = jnp.concatenate([sf_ref[c, p], sb_ref[c, p]], axis=1)
            s_pad = jnp.concatenate([s_in, zeros64] if g == 0 else [zeros64, s_in], axis=0)
            oi = jnp.dot(cmb, s_pad.astype(BF16), preferred_element_type=F32)
            o_p = o_p + oi[:, 0:LANES] * e_x[:, sl] + oi[:, LANES:2 * LANES] * e_x[:, 512 + p * LANES:512 + (p + 1) * LANES]
            outs.append(o_p)
        y = jnp.concatenate(outs, axis=1) + dsk_ref[...] * xs
        z = in_ref[pl.ds(r0, T), 0:512]
        o_ref[pl.ds(r0, T), :] = _rms(y * _silu(z), nw_ref[...])
        return carry

    lax.fori_loop(0, nc, phase1, 0)


def _ssd_mixer(ssd_in, bsz, seq_len, latent, cw, cb, dtb, alog, dsk, nw, expand, s0):
    nc = seq_len // SCAN_CHUNK
    const2 = lambda b: (0, 0)
    per_b3 = lambda b: (b, 0, 0)
    per_b4 = lambda b: (b, 0, 0, 0)
    in_specs = [pl.BlockSpec((None, seq_len, SSD_W), per_b3),
                pl.BlockSpec((3, SSD_CONV_CH), const2),
                pl.BlockSpec((1, SSD_CONV_CH), const2),
                pl.BlockSpec((1, LANES), const2),
                pl.BlockSpec((1, LANES), const2),
                pl.BlockSpec((1, SSD_INNER), const2),
                pl.BlockSpec((1, SSD_INNER), const2),
                pl.BlockSpec((LANES, 1024), const2)]
    args = [ssd_in.reshape(bsz, seq_len, SSD_W), cw, cb, dtb, alog, dsk, nw, expand]
    out_specs = [pl.BlockSpec((None, seq_len, SSD_INNER), per_b3)]
    out_shape = [jax.ShapeDtypeStruct((bsz, seq_len, SSD_INNER), F32)]
    if latent:
        in_specs.append(pl.BlockSpec((None, 4, 64, 2 * LANES), per_b4))
        args.append(s0)
    else:
        out_specs.append(pl.BlockSpec((None, 2, SSD_HEADS, SSD_STATE, SSD_HEAD_DIM), lambda b: (b, 0, 0, 0, 0)))
        out_shape.append(jax.ShapeDtypeStruct((bsz, 2, SSD_HEADS, SSD_STATE, SSD_HEAD_DIM), F32))
    scratch = [pltpu.VMEM((seq_len, SSD_CONV_CH), F32),
               pltpu.VMEM((seq_len, LANES), F32),
               pltpu.VMEM((seq_len, LANES), F32),
               pltpu.VMEM((nc, 4, 64, LANES), F32),
               pltpu.VMEM((nc, 4, 64, LANES), F32),
               pltpu.VMEM((nc, SUBLANES, 512), F32),
               pltpu.VMEM((4, 64, 2 * LANES), F32)]
    res = pl.pallas_call(
        functools.partial(_ssd_kernel, latent, seq_len),
        grid=(bsz,),
        in_specs=in_specs, out_specs=out_specs, out_shape=out_shape,
        scratch_shapes=scratch,
        compiler_params=_cparams(1),
        name="ssd_lat" if latent else "ssd_ctx",
    )(*args)
    return res


def _ret_kernel(latent, seq_len, *refs):
    if latent:
        (in_ref, lg_ref, gn_ref, s0_ref,
         o_ref, dc_ref, we_ref, cd_ref, sf_ref, sb_ref, run_ref) = refs
    else:
        (in_ref, lg_ref, gn_ref,
         o_ref, st_ref, dc_ref, we_ref, cd_ref, sf_ref, sb_ref, run_ref) = refs
    T = SCAN_CHUNK
    nc = seq_len // T

    @pl.when(pl.program_id(0) == 0)
    def _tables():
        la = -_softplus(-lg_ref[...])
        ii = _row_iota((T, T)).astype(F32)
        jj = _lane_iota((T, T)).astype(F32)
        for h in range(RET_HEADS):
            la_f = la[h:h + 1, :]
            la_b = la[RET_HEADS + h:RET_HEADS + h + 1, :]
            dc_ref[h] = (jnp.exp(jnp.where(jj <= ii, (ii - jj) * la_f, NEG_BIG))
                         + jnp.exp(jnp.where(jj >= ii, (jj - ii) * la_b, NEG_BIG)))
            we_ref[h, 0] = jnp.exp((T - 1.0 - ii) * la_f)
            we_ref[h, 1] = jnp.exp(ii * la_b)
            we_ref[h, 2] = jnp.exp((ii + 1.0) * la_f)
            we_ref[h, 3] = jnp.exp((T - ii) * la_b)
            cd_ref[h] = jnp.concatenate([jnp.broadcast_to(jnp.exp(T * la_f), (SUBLANES, LANES)),
                                         jnp.broadcast_to(jnp.exp(T * la_b), (SUBLANES, LANES))], axis=1)

    if latent:
        run_ref[...] = s0_ref[...]
    else:
        run_ref[...] = jnp.zeros(run_ref.shape, F32)

    def phase0(c, carry):
        r0 = pl.multiple_of(c * T, T)
        for h in range(RET_HEADS):
            blk = h // 2
            k_blk = in_ref[pl.ds(r0, T), 256 + blk * LANES:256 + (blk + 1) * LANES].astype(BF16)
            v_h = in_ref[pl.ds(r0, T), 512 + h * LANES:512 + (h + 1) * LANES]
            vw = jnp.concatenate([v_h * we_ref[h, 0], v_h * we_ref[h, 1]], axis=1)
            cs = _bdot_tn(k_blk, vw)[(h % 2) * 64:(h % 2 + 1) * 64]
            s_run = run_ref[h]
            sf_ref[c, h] = s_run[:, 0:LANES]
            sb_ref[c, h] = cs[:, LANES:2 * LANES]
            new_f = s_run[:, 0:LANES] * cd_ref[h, 0:1, 0:LANES] + cs[:, 0:LANES]
            run_ref[h] = jnp.concatenate([new_f, s_run[:, LANES:2 * LANES]], axis=1)
        return carry

    lax.fori_loop(0, nc, phase0, 0)

    def bwd_states(k, carry):
        c = nc - 1 - k
        for h in range(RET_HEADS):
            s_run = run_ref[h]
            cs_b = sb_ref[c, h]
            sb_ref[c, h] = s_run[:, LANES:2 * LANES]
            new_b = s_run[:, LANES:2 * LANES] * cd_ref[h, 0:1, LANES:2 * LANES] + cs_b
            run_ref[h] = jnp.concatenate([s_run[:, 0:LANES], new_b], axis=1)
        return carry

    lax.fori_loop(0, nc, bwd_states, 0)
    if not latent:
        for h in range(RET_HEADS):
            for d in range(2):
                st_ref[d, h] = run_ref[h, :, d * LANES:(d + 1) * LANES]

    lane_c = _lane_iota((T, LANES))
    zeros64 = jnp.zeros((64, 2 * LANES), F32)

    def phase1(c, carry):
        r0 = pl.multiple_of(c * T, T)
        for h in range(RET_HEADS):
            blk = h // 2
            q_blk = in_ref[pl.ds(r0, T), blk * LANES:(blk + 1) * LANES]
            k_blk = in_ref[pl.ds(r0, T), 256 + blk * LANES:256 + (blk + 1) * LANES].astype(BF16)
            v_h = in_ref[pl.ds(r0, T), 512 + h * LANES:512 + (h + 1) * LANES].astype(BF16)
            g_h = in_ref[pl.ds(r0, T), 1024 + h * LANES:1024 + (h + 1) * LANES]
            q_m = jnp.where((lane_c < 64) == (h % 2 == 0), q_blk, 0.0)
            m_h = (_bdot_nt(q_m, k_blk) * dc_ref[h]).astype(BF16)
            o = jnp.dot(m_h, v_h, preferred_element_type=F32)
            s_in = jnp.concatenate([sf_ref[c, h], sb_ref[c, h]], axis=1)
            s_pad = jnp.concatenate([s_in, zeros64] if h % 2 == 0 else [zeros64, s_in], axis=0)
            oi = _bdot(q_blk, s_pad)
            o = o + oi[:, 0:LANES] * we_ref[h, 2] + oi[:, LANES:2 * LANES] * we_ref[h, 3]
            oc = o - jnp.mean(o, axis=-1, keepdims=True)
            on = oc * lax.rsqrt(jnp.mean(oc * oc, axis=-1, keepdims=True) + NORM_EPS)
            o_ref[pl.ds(r0, T), h * LANES:(h + 1) * LANES] = (
                on * gn_ref[:, h * LANES:(h + 1) * LANES] * _silu(g_h))
        return carry

    lax.fori_loop(0, nc, phase1, 0)


def _ret_mixer(ret_in, bsz, seq_len, latent, logit_rows, gn_w, s0):
    nc = seq_len // SCAN_CHUNK
    T = SCAN_CHUNK
    const2 = lambda b: (0, 0)
    per_b3 = lambda b: (b, 0, 0)
    per_b4 = lambda b: (b, 0, 0, 0)
    in_specs = [pl.BlockSpec((None, seq_len, RET_W), per_b3),
                pl.BlockSpec((SUBLANES, LANES), const2),
                pl.BlockSpec((1, 512), const2)]
    args = [ret_in.reshape(bsz, seq_len, RET_W), logit_rows, gn_w]
    out_specs = [pl.BlockSpec((None, seq_len, 512), per_b3)]
    out_shape = [jax.ShapeDtypeStruct((bsz, seq_len, 512), F32)]
    if latent:
        in_specs.append(pl.BlockSpec((None, RET_HEADS, 64, 2 * LANES), per_b4))
        args.append(s0)
    else:
        out_specs.append(pl.BlockSpec((None, 2, RET_HEADS, RET_QK, RET_V), lambda b: (b, 0, 0, 0, 0)))
        out_shape.append(jax.ShapeDtypeStruct((bsz, 2, RET_HEADS, RET_QK, RET_V), F32))
    scratch = [pltpu.VMEM((RET_HEADS, T, T), F32),
               pltpu.VMEM((RET_HEADS, 4, T, LANES), F32),
               pltpu.VMEM((RET_HEADS, SUBLANES, 2 * LANES), F32),
               pltpu.VMEM((nc, RET_HEADS, 64, LANES), F32),
               pltpu.VMEM((nc, RET_HEADS, 64, LANES), F32),
               pltpu.VMEM((RET_HEADS, 64, 2 * LANES), F32)]
    return pl.pallas_call(
        functools.partial(_ret_kernel, latent, seq_len),
        grid=(bsz,),
        in_specs=in_specs, out_specs=out_specs, out_shape=out_shape,
        scratch_shapes=scratch,
        compiler_params=_cparams(1),
        name="ret_lat" if latent else "ret_ctx",
    )(*args)


def _softmax_pv(s, v):
    m = jnp.max(s, axis=-1, keepdims=True)
    p = jnp.exp(s - m)
    l = jnp.sum(p, axis=-1, keepdims=True)
    return jnp.dot(p.astype(BF16), v, preferred_element_type=F32) / l


def _att_kernel(latent, tq, *refs):
    if latent:
        q_ref, kn_ref, vn_ref, kc_ref, vc_ref, o_ref, kb_ref, vb_ref = refs
    else:
        q_ref, kn_ref, vn_ref, o_ref, kb_ref, vb_ref = refs

    @pl.when(pl.program_id(1) == 0)
    def _fill():
        if latent:
            kb_ref[0:PAST_LEN, :] = kc_ref[...].astype(BF16)
            vb_ref[0:PAST_LEN, :] = vc_ref[...].astype(BF16)
            kb_ref[PAST_LEN:, :] = kn_ref[...].astype(BF16)
            vb_ref[PAST_LEN:, :] = vn_ref[...].astype(BF16)
        else:
            kb_ref[...] = kn_ref[...].astype(BF16)
            vb_ref[...] = vn_ref[...].astype(BF16)

    sub = min(tq, ATT_SUB_ROWS)
    lane_q = _lane_iota((sub, LANES))
    for i in range(tq // sub):
        rows = slice(i * sub, (i + 1) * sub)
        acc = []
        for g in range(ATT_KV_HEADS):
            qs = jnp.concatenate(
                [jnp.where((lane_q < 64) == (g == 0), q_ref[rows, j * LANES:(j + 1) * LANES] * 0.125, 0.0).astype(BF16)
                 for j in range(4)], axis=0)
            s = lax.dot_general(qs, kb_ref[...], (((1,), (1,)), ((), ())), preferred_element_type=F32)
            acc.append(_softmax_pv(s, vb_ref[...]))
        for j in range(4):
            o_ref[rows, j * LANES:(j + 1) * LANES] = jnp.where(lane_q < 64, acc[0][j * sub:(j + 1) * sub],
                                                               acc[1][j * sub:(j + 1) * sub])


def _att_mixer(aq, ak, av, bsz, seq_len, latent, k_ctx, v_ctx):
    tq = 512 if latent else 256
    lk = seq_len + (PAST_LEN if latent else 0)
    qmap = lambda b, i: (b, i, 0)
    bmap = lambda b, i: (b, 0, 0)
    in_specs = [pl.BlockSpec((None, tq, 512), qmap),
                pl.BlockSpec((None, seq_len, LANES), bmap),
                pl.BlockSpec((None, seq_len, LANES), bmap)]
    args = [aq.reshape(bsz, seq_len, 512), ak.reshape(bsz, seq_len, LANES), av.reshape(bsz, seq_len, LANES)]
    if latent:
        in_specs += [pl.BlockSpec((None, PAST_LEN, LANES), bmap)] * 2
        args += [k_ctx, v_ctx]
    return pl.pallas_call(
        functools.partial(_att_kernel, latent, tq),
        grid=(bsz, seq_len // tq),
        in_specs=in_specs,
        out_specs=pl.BlockSpec((None, tq, 512), qmap),
        out_shape=jax.ShapeDtypeStruct((bsz, seq_len, 512), F32),
        scratch_shapes=[pltpu.VMEM((lk, LANES), BF16), pltpu.VMEM((lk, LANES), BF16)],
        compiler_params=_cparams(2),
        name="att_lat" if latent else "att_ctx",
    )(*args)


def _mla_kernel(latent, tq, *refs):
    if latent:
        (mq_ref, cn_ref, rn_ref, cc_ref, rc_ref, qn_ref, wq_ref, wk_ref, wv_ref,
         cos_ref, su_ref, sd_ref, o_ref, kb_ref, vb_ref) = refs
    else:
        (mq_ref, cn_ref, rn_ref, qn_ref, wq_ref, wk_ref, wv_ref, o_ref, kb_ref, vb_ref) = refs

    def fill(r0, ckv, kr):
        cb = ckv.astype(BF16)
        n = ckv.shape[0]
        kn = jnp.dot(cb, wk_ref[...], preferred_element_type=F32)
        for h in range(MLA_HEADS):
            kb_ref[r0:r0 + n, h * LANES:(h + 1) * LANES] = (kn[:, h * LANES:(h + 1) * LANES] + kr).astype(BF16)
        vb_ref[r0:r0 + n, :] = jnp.dot(cb, wv_ref[...], preferred_element_type=F32).astype(BF16)

    @pl.when(pl.program_id(1) == 0)
    def _fill():
        if latent:
            fill(0, cc_ref[...], rc_ref[...])
            fill(PAST_LEN, cn_ref[...], rn_ref[...])
        else:
            fill(0, cn_ref[...], rn_ref[...])

    q = _bdot(_rms(mq_ref[...], qn_ref[...]), wq_ref[...])
    lane_q = _lane_iota((tq, LANES))
    scale = (MLA_NOPE + MLA_ROPE) ** -0.5
    for p in range(4):
        halves = []
        for h in (2 * p, 2 * p + 1):
            qh = q[:, h * LANES:(h + 1) * LANES]
            if latent:
                qh = _rope(qh, cos_ref[...], su_ref[...], sd_ref[...], 8)
            s = _bdot_nt(qh, kb_ref[:, h * LANES:(h + 1) * LANES]) * scale
            halves.append(_softmax_pv(s, vb_ref[:, p * LANES:(p + 1) * LANES]))
        o_ref[:, p * LANES:(p + 1) * LANES] = jnp.where(lane_q < 64, halves[0], halves[1])


def _mla_mixer(mq, ckv, kr, bsz, seq_len, latent, ckv_ctx, kr_ctx, qn, wq, wk, wv, layer, tabs):
    tq = 256
    lk = seq_len + (PAST_LEN if latent else 0)
    qmap = lambda b, i: (b, i, 0)
    bmap = lambda b, i: (b, 0, 0)
    const = lambda b, i: (0, 0)
    lyr3 = lambda b, i: (layer, 0, 0)
    in_specs = [pl.BlockSpec((None, tq, MLA_Q_RANK), qmap),
                pl.BlockSpec((None, seq_len, MLA_KV_RANK), bmap),
                pl.BlockSpec((None, seq_len, LANES), bmap)]
    args = [mq.reshape(bsz, seq_len, MLA_Q_RANK), ckv.reshape(bsz, seq_len, MLA_KV_RANK),
            kr.reshape(bsz, seq_len, LANES)]
    if latent:
        in_specs += [pl.BlockSpec((None, PAST_LEN, MLA_KV_RANK), bmap),
                     pl.BlockSpec((None, PAST_LEN, LANES), bmap)]
        args += [ckv_ctx, kr_ctx]
    in_specs += [pl.BlockSpec((1, MLA_Q_RANK), const),
                 _resident((None, MLA_Q_RANK, 1024), lyr3),
                 _resident((None, MLA_KV_RANK, 1024), lyr3),
                 _resident((None, MLA_KV_RANK, 512), lyr3)]
    args += [qn, wq, wk, wv]
    if latent:
        in_specs += [pl.BlockSpec((tq, LANES), lambda b, i: (i, 0))] * 3
        args += list(tabs)
    return pl.pallas_call(
        functools.partial(_mla_kernel, latent, tq),
        grid=(bsz, seq_len // tq),
        in_specs=in_specs,
        out_specs=pl.BlockSpec((None, tq, 512), qmap),
        out_shape=jax.ShapeDtypeStruct((bsz, seq_len, 512), F32),
        scratch_shapes=[pltpu.VMEM((lk, 1024), BF16), pltpu.VMEM((lk, 512), BF16)],
        compiler_params=_cparams(2),
        name="mla_lat" if latent else "mla_ctx",
    )(*args)


def _merge_kernel(x_ref, sh_ref, sc_ref, ga_ref, gpre_ref, gpost_ref, o1_ref, o2_ref, o3_ref, o4_ref,
                  wm_ref, bm_ref, wb_ref, wo_ref, y_ref):
    x = x_ref[...]
    hb = (_rms(x, gpre_ref[...]) * (1.0 + sc_ref[...]) + sh_ref[...]).astype(BF16)
    merged = None
    for k, o_ref in enumerate((o1_ref, o2_ref, o3_ref, o4_ref)):
        cols = slice(k * D_MODEL, (k + 1) * D_MODEL)
        gate = jax.nn.sigmoid(jnp.dot(hb, wm_ref[:, cols], preferred_element_type=F32) + bm_ref[:, cols])
        term = gate * _bdot(o_ref[...], wb_ref[k])
        merged = term if merged is None else merged + term
    m = _bdot(merged, wo_ref[...])
    y_ref[...] = x + ga_ref[...] * _rms(m, gpost_ref[...])


def _merge(x2d, mod5, layer, latent, seq_len, g_pre, g_post, branches, w_merge, b_merge, w_br, w_out):
    n = x2d.shape[0]
    tm = TM_ROWS
    row = lambda i: (i, 0)
    const = lambda i: (0, 0)
    in_specs = ([pl.BlockSpec((tm, D_MODEL), row)]
                + [_mod_spec(layer, latent, seq_len, tm, k) for k in (0, 1, 2)]
                + [pl.BlockSpec((1, D_MODEL), const)] * 2
                + [pl.BlockSpec((tm, 512), row)] * 4
                + [_resident((None, D_MODEL, 4 * D_MODEL), lambda i: (layer, 0, 0)),
                   pl.BlockSpec((1, 4 * D_MODEL), const),
                   _resident((None, 4, 512, D_MODEL), lambda i: (layer, 0, 0, 0)),
                   _resident((None, D_MODEL, D_MODEL), lambda i: (layer, 0, 0))])
    return pl.pallas_call(
        _merge_kernel,
        grid=(n // tm,),
        in_specs=in_specs,
        out_specs=pl.BlockSpec((tm, D_MODEL), row),
        out_shape=jax.ShapeDtypeStruct((n, D_MODEL), F32),
        compiler_params=_cparams(1),
        name="merge_lat" if latent else "merge_ctx",
    )(x2d, mod5, mod5, mod5, g_pre, g_post, *[b.reshape(n, 512) for b in branches],
      w_merge, b_merge, w_br, w_out)


def _ffn_kernel(tm, seq_len, x_ref, xp_ref, xn_ref, sh_ref, sc_ref, gf_ref, gpre_ref, gpost_ref,
                wu_ref, cw_ref, cb_ref, wd_ref, y_ref):
    tiles_per_seq = seq_len // tm
    i = pl.program_id(0)
    first = (i % tiles_per_seq) == 0
    last = (i % tiles_per_seq) == tiles_per_seq - 1

    def nm(x):
        return _rms(x, gpre_ref[...]) * (1.0 + sc_ref[...]) + sh_ref[...]

    x = x_ref[...]
    h_prev = jnp.where(first, 0.0, nm(xp_ref[...]))
    h_next = jnp.where(last, 0.0, nm(xn_ref[...]))
    hb = jnp.concatenate([h_prev, nm(x), h_next], axis=0).astype(BF16)
    rows = tm + 2 * SUBLANES

    width = 2 * FF_HALF
    acc = None
    for j in range(D_FF // FF_HALF):
        cols = slice(j * width, (j + 1) * width)
        p = jnp.dot(hb, wu_ref[:, cols], preferred_element_type=F32)
        w = cw_ref[:, cols]
        u = (pltpu.roll(p, 1, 0) * w[0:1] + p * w[1:2] + pltpu.roll(p, rows - 1, 0) * w[2:3])
        u = u[SUBLANES:SUBLANES + tm] + cb_ref[:, cols]
        act = _silu(u[:, FF_HALF:width]) * u[:, 0:FF_HALF]
        part = _bdot(act, wd_ref[j * FF_HALF:(j + 1) * FF_HALF, :])
        acc = part if acc is None else acc + part
    y_ref[...] = x + gf_ref[...] * _rms(acc, gpost_ref[...])


def _ffn(x2d, mod5, layer, latent, seq_len, g_pre, g_post, w_up, conv_w, conv_b, w_down):
    n = x2d.shape[0]
    tm = min(TM_ROWS, seq_len)
    assert seq_len % tm == 0
    hb = tm // SUBLANES
    n_hb = n // SUBLANES
    row = lambda i: (i, 0)
    const = lambda i: (0, 0)
    lyr3 = lambda i: (layer, 0, 0)
    in_specs = ([pl.BlockSpec((tm, D_MODEL), row),
                 pl.BlockSpec((SUBLANES, D_MODEL), lambda i: (jnp.maximum(i * hb - 1, 0), 0)),
                 pl.BlockSpec((SUBLANES, D_MODEL), lambda i: (jnp.minimum((i + 1) * hb, n_hb - 1), 0))]
                + [_mod_spec(layer, latent, seq_len, tm, k) for k in (3, 4, 5)]
                + [pl.BlockSpec((1, D_MODEL), const)] * 2
                + [_resident((None, D_MODEL, 2 * D_FF), lyr3),
                   _resident((None, 3, 2 * D_FF), lyr3),
                   _resident((None, 1, 2 * D_FF), lyr3),
                   _resident((None, D_FF, D_MODEL), lyr3)])
    return pl.pallas_call(
        functools.partial(_ffn_kernel, tm, seq_len),
        grid=(n // tm,),
        in_specs=in_specs,
        out_specs=pl.BlockSpec((tm, D_MODEL), row),
        out_shape=jax.ShapeDtypeStruct((n, D_MODEL), F32),
        compiler_params=_cparams(1),
        name="ffn_lat" if latent else "ffn_ctx",
    )(x2d, x2d, x2d, mod5, mod5, mod5, g_pre, g_post, w_up, conv_w, conv_b, w_down)


def _rope_tables(seq_len, head_dim, lane_of_dim):
    d_axis = head_dim // 2
    t = np.arange(seq_len)
    pos = np.stack([(t // GRID_W).astype(np.float32), (t % GRID_W).astype(np.float32)], axis=0)
    inv_freq = (np.float32(ROPE_BASE) ** (-np.arange(0, d_axis, 2, dtype=np.float32) / np.float32(d_axis))).astype(np.float32)
    cos = np.ones((seq_len, LANES), np.float32)
    s_up = np.zeros((seq_len, LANES), np.float32)
    s_dn = np.zeros((seq_len, LANES), np.float32)
    for lane, d in enumerate(lane_of_dim):
        if d < 0:
            continue
        axis, j = divmod(d, d_axis)
        second = j >= d_axis // 2
        ang = (pos[axis] * inv_freq[j % (d_axis // 2)]).astype(np.float32)
        cos[:, lane] = np.cos(ang)
        if second:
            s_up[:, lane] = np.sin(ang)
        else:
            s_dn[:, lane] = -np.sin(ang)
    return jnp.asarray(cos), jnp.asarray(s_up), jnp.asarray(s_dn)


def _ssd_expand_matrix():
    e = np.zeros((LANES, 1024), np.float32)
    for d in range(2):
        for h in range(SSD_HEADS):
            e[d * SSD_HEADS + h, d * 512 + h * 64:d * 512 + (h + 1) * 64] = 1.0
    return jnp.asarray(e, dtype=BF16)


def _matmul_weights(w_in, mla_w_uq, mla_w_ukv, w_br_ssd, w_br_ret, w_br_att, w_br_mla, w_merge, w_out,
                    w_ffn_up, ffn_conv_w, ffn_conv_b, w_ffn_down):
    off = np.cumsum([0, 512, 768, 16, 256, 256, 512, 512, 512, 128, 128, 384, 288])
    col = lambda k: w_in[:, :, off[k]:off[k + 1]]
    zeros = lambda n: jnp.zeros((DEPTH, D_MODEL, n), F32)
    aq = col(7)
    aq_pairs = [jnp.concatenate([aq[:, :, j * 64:(j + 1) * 64], aq[:, :, (j + 4) * 64:(j + 5) * 64]], axis=2)
                for j in range(4)]
    mckv = col(11)
    w_in_p = jnp.concatenate(
        [col(0), col(1), col(2), zeros(112),
         col(3), col(4), col(5), col(6),
         *aq_pairs, col(8), col(9),
         col(10), mckv[:, :, :MLA_KV_RANK], zeros(64), mckv[:, :, MLA_KV_RANK:], zeros(32)],
        axis=2).astype(BF16)

    uq = mla_w_uq.reshape(DEPTH, MLA_Q_RANK, MLA_HEADS, MLA_NOPE + MLA_ROPE)
    wq = jnp.pad(uq, ((0, 0), (0, 0), (0, 0), (0, 32))).reshape(DEPTH, MLA_Q_RANK, 1024).astype(BF16)
    ukv = mla_w_ukv.reshape(DEPTH, MLA_KV_RANK, MLA_HEADS, MLA_NOPE + MLA_V)
    wk = jnp.pad(ukv[..., :MLA_NOPE], ((0, 0), (0, 0), (0, 0), (0, 64))).reshape(DEPTH, MLA_KV_RANK, 1024).astype(BF16)
    wv = ukv[..., MLA_NOPE:].reshape(DEPTH, MLA_KV_RANK, 512).astype(BF16)

    perm = np.concatenate([np.r_[j * 64:(j + 1) * 64, (j + 4) * 64:(j + 5) * 64] for j in range(4)])
    w_br = jnp.stack([w_br_ssd, w_br_ret, w_br_att[:, perm], w_br_mla], axis=1).astype(BF16)

    n_chunks = D_FF // FF_HALF
    chunked = lambda a: (a.reshape(a.shape[:-1] + (2, n_chunks, FF_HALF)).swapaxes(-3, -2)
                         .reshape(a.shape[:-1] + (2 * D_FF,)))
    return dict(
        w_in=w_in_p, wq=wq, wk=wk, wv=wv, w_br=w_br,
        w_merge=w_merge.astype(BF16), w_out=w_out.astype(BF16),
        w_up=chunked(w_ffn_up).astype(BF16), ffn_cw=chunked(ffn_conv_w),
        ffn_cb=chunked(ffn_conv_b).reshape(DEPTH, 1, 2 * D_FF),
        w_down=w_ffn_down.astype(BF16),
    )


def _layer_params(i, ssd_conv_w, ssd_conv_b, ssd_dt_bias, ssd_a_log, ssd_d, ssd_norm_w,
                  ret_decay_logit, ret_gn_w, att_q_norm, att_k_norm, mla_q_norm, mla_kv_norm,
                  b_merge, g_pre_mix, g_post_mix, g_pre_ffn, g_post_ffn):
    pad_row = lambda v: jnp.pad(v.reshape(1, -1), ((0, 0), (0, LANES - v.size)))
    return dict(
        qn=jnp.tile(att_q_norm[i], 2).reshape(1, LANES),
        kn=jnp.tile(att_k_norm[i], 2).reshape(1, LANES),
        cn=mla_kv_norm[i].reshape(1, MLA_KV_RANK),
        ssd_cw=ssd_conv_w[i], ssd_cb=ssd_conv_b[i].reshape(1, SSD_CONV_CH),
        ssd_dtb=pad_row(ssd_dt_bias[i]), ssd_alog=pad_row(ssd_a_log[i]),
        ssd_dsk=jnp.repeat(ssd_d[i], SSD_HEAD_DIM).reshape(1, SSD_INNER),
        ssd_nw=ssd_norm_w[i].reshape(1, SSD_INNER),
        ret_logit=jnp.broadcast_to(ret_decay_logit[i].reshape(SUBLANES, 1), (SUBLANES, LANES)),
        ret_gn=ret_gn_w[i].reshape(1, 512),
        mla_qn=mla_q_norm[i].reshape(1, MLA_Q_RANK),
        b_merge=b_merge[i].reshape(1, 4 * D_MODEL),
        g_pre_mix=g_pre_mix[i].reshape(1, D_MODEL), g_post_mix=g_post_mix[i].reshape(1, D_MODEL),
        g_pre_ffn=g_pre_ffn[i].reshape(1, D_MODEL), g_post_ffn=g_post_ffn[i].reshape(1, D_MODEL),
    )


def _ssd_state_to_pairs(s):
    b = s.shape[0]
    s = s.reshape(b, 2, 4, 2, 64, 64).transpose(0, 2, 4, 1, 3, 5)
    return s.reshape(b, 4, 64, 256)


def _ret_state_to_lanes(s):
    b = s.shape[0]
    return s.transpose(0, 2, 3, 1, 4).reshape(b, RET_HEADS, 64, 256)


def _trunk_pass(x2d, mod5, layer, latent, bsz, seq_len, lp, mw, tabs64, tabs_mla, expand, ctx):
    (ssd_in, ret_in, aq, ak, av, mq, ckv, kr) = _inproj(
        x2d, mod5, layer, latent, seq_len, lp['g_pre_mix'], mw['w_in'], lp['qn'], lp['kn'], lp['cn'],
        (tabs64 + tabs_mla) if latent else None)
    ssd_res = _ssd_mixer(ssd_in, bsz, seq_len, latent, lp['ssd_cw'], lp['ssd_cb'], lp['ssd_dtb'],
                         lp['ssd_alog'], lp['ssd_dsk'], lp['ssd_nw'], expand,
                         ctx['ssd'] if latent else None)
    ret_res = _ret_mixer(ret_in, bsz, seq_len, latent, lp['ret_logit'], lp['ret_gn'],
                         ctx['ret'] if latent else None)
    o_att = _att_mixer(aq, ak, av, bsz, seq_len, latent,
                       ctx['att_k'] if latent else None, ctx['att_v'] if latent else None)
    o_mla = _mla_mixer(mq, ckv, kr, bsz, seq_len, latent,
                       ctx['mla_ckv'] if latent else None, ctx['mla_kr'] if latent else None,
                       lp['mla_qn'], mw['wq'], mw['wk'], mw['wv'], layer, tabs_mla)
    x2d = _merge(x2d, mod5, layer, latent, seq_len, lp['g_pre_mix'], lp['g_post_mix'],
                 (ssd_res[0], ret_res[0], o_att, o_mla), mw['w_merge'], lp['b_merge'], mw['w_br'], mw['w_out'])
    x2d = _ffn(x2d, mod5, layer, latent, seq_len, lp['g_pre_ffn'], lp['g_post_ffn'],
               mw['w_up'], mw['ffn_cw'], mw['ffn_cb'], mw['w_down'])
    cache = None
    if not latent:
        cache = dict(ssd=ssd_res[1], ret=ret_res[1],
                     att_k=ak.reshape(bsz, seq_len, ATT_KV_HEADS, ATT_HEAD_DIM),
                     att_v=av.reshape(bsz, seq_len, ATT_KV_HEADS, ATT_HEAD_DIM),
                     mla_ckv=ckv.reshape(bsz, seq_len, MLA_KV_RANK),
                     mla_krope=kr.reshape(bsz, seq_len, LANES)[:, :, 64:64 + MLA_ROPE])
    return x2d, cache


def kernel(x_prompt, x_sample, state_ssd, state_ret, cache_att_k, cache_att_v, cache_mla_ckv, cache_mla_krope, c, c_ctx, w_mod, b_mod, g_pre_mix, g_post_mix, g_pre_ffn, g_post_ffn, w_in, ssd_conv_w, ssd_conv_b, ssd_dt_bias, ssd_a_log, ssd_d, ssd_norm_w, ret_decay_logit, ret_gn_w, att_q_norm, att_k_norm, mla_q_norm, mla_w_uq, mla_kv_norm, mla_w_ukv, w_br_ssd, w_br_ret, w_br_att, w_br_mla, w_merge, b_merge, w_out, w_ffn_up, ffn_conv_w, ffn_conv_b, w_ffn_down):
    cvec = jnp.concatenate([c_ctx[None, :], c, jnp.zeros((SUBLANES - 1 - DEC_BATCH, D_MODEL), F32)], axis=0)
    mod5 = _modulation(cvec, w_mod, b_mod).reshape(DEPTH, SUBLANES, 6, 1, D_MODEL)

    lane_dims_64 = [d % 64 for d in range(LANES)]
    lane_dims_mla = [d - 64 if 64 <= d < 96 else -1 for d in range(LANES)]
    tabs64 = _rope_tables(DEC_SEQ, ATT_HEAD_DIM, lane_dims_64)
    tabs_mla = _rope_tables(DEC_SEQ, MLA_ROPE, lane_dims_mla)
    expand = _ssd_expand_matrix()

    xp = x_prompt.reshape(BATCH * SEQ, D_MODEL)
    xs = x_sample.reshape(DEC_BATCH * DEC_SEQ, D_MODEL)
    caches = []
    mw = _matmul_weights(w_in, mla_w_uq, mla_w_ukv, w_br_ssd, w_br_ret, w_br_att, w_br_mla, w_merge, w_out,
                         w_ffn_up, ffn_conv_w, ffn_conv_b, w_ffn_down)
    for i in range(DEPTH):
        lp = _layer_params(i, ssd_conv_w, ssd_conv_b, ssd_dt_bias, ssd_a_log, ssd_d, ssd_norm_w,
                           ret_decay_logit, ret_gn_w, att_q_norm, att_k_norm, mla_q_norm, mla_kv_norm,
                           b_merge, g_pre_mix, g_post_mix, g_pre_ffn, g_post_ffn)
        xp, cache = _trunk_pass(xp, mod5, i, False, BATCH, SEQ, lp, mw, tabs64, tabs_mla, expand, None)
        caches.append(cache)
        ctx = dict(ssd=_ssd_state_to_pairs(state_ssd[:, i]), ret=_ret_state_to_lanes(state_ret[:, i]),
                   att_k=cache_att_k[:, i].reshape(DEC_BATCH, PAST_LEN, LANES),
                   att_v=cache_att_v[:, i].reshape(DEC_BATCH, PAST_LEN, LANES),
                   mla_ckv=cache_mla_ckv[:, i],
                   mla_kr=jnp.pad(cache_mla_krope[:, i], ((0, 0), (0, 0), (64, 32))))
        xs, _ = _trunk_pass(xs, mod5, i, True, DEC_BATCH, DEC_SEQ, lp, mw, tabs64, tabs_mla, expand, ctx)

    stack = lambda k: jnp.stack([cc[k] for cc in caches], axis=1)
    return (xp.reshape(BATCH, SEQ, D_MODEL), xs.reshape(DEC_BATCH, DEC_SEQ, D_MODEL),
            stack('ssd'), stack('ret'), stack('att_k'), stack('att_v'), stack('mla_ckv'), stack('mla_krope'))
```

```python
import functools

import numpy as np
import jax
import jax.numpy as jnp
from jax import lax
from jax.experimental import pallas as pl
from jax.experimental.pallas import tpu as pltpu

D_MODEL = 1024
BATCH = 32
SEQ = 256
DEPTH = 2
DEC_BATCH = 4
DEC_SEQ = 2048
PAST_LEN = 256
GRID_W = 64
ROPE_BASE = 10000.0
NORM_EPS = 1e-6
SCAN_CHUNK = 128
SSD_HEADS = 8
SSD_HEAD_DIM = 64
SSD_INNER = 512
SSD_GROUPS = 2
SSD_STATE = 64
SSD_CONV_CH = 768
RET_HEADS = 4
RET_QK = 64
RET_V = 128
ATT_HEADS = 8
ATT_KV_HEADS = 2
ATT_HEAD_DIM = 64
MLA_HEADS = 8
MLA_Q_RANK = 384
MLA_KV_RANK = 256
MLA_NOPE = 64
MLA_ROPE = 32
MLA_V = 64
D_FF = 2816

F32 = jnp.float32
BF16 = jnp.bfloat16
LANES = 128
SUBLANES = 8
VMEM_LIMIT = 56 * 1024 * 1024
NEG_BIG = -1e30

SSD_W = 1408
RET_W = 1536
TM_ROWS = 512
ATT_SUB_ROWS = 128
FF_HALF = D_FF


def _cparams(n_axes):
    return pltpu.CompilerParams(dimension_semantics=("arbitrary",) * n_axes,
                                vmem_limit_bytes=VMEM_LIMIT)


def _silu(x):
    return x * jax.nn.sigmoid(x)


def _softplus(x):
    return jnp.maximum(x, 0.0) + jnp.log1p(jnp.exp(-jnp.abs(x)))


def _rms(x, w):
    ms = jnp.mean(x * x, axis=-1, keepdims=True)
    return x * lax.rsqrt(ms + NORM_EPS) * w


def _bdot(a, b):
    return jnp.dot(a.astype(BF16), b.astype(BF16), preferred_element_type=F32)


def _bdot_nt(a, b):
    return lax.dot_general(a.astype(BF16), b.astype(BF16), (((1,), (1,)), ((), ())),
                           preferred_element_type=F32)


def _bdot_tn(a, b):
    return lax.dot_general(a.astype(BF16), b.astype(BF16), (((0,), (0,)), ((), ())),
                           preferred_element_type=F32)


def _split_dot(a, b_bf16):
    hi = a.astype(BF16)
    lo = (a - hi.astype(F32)).astype(BF16)
    return (jnp.dot(hi, b_bf16, preferred_element_type=F32)
            + jnp.dot(lo, b_bf16, preferred_element_type=F32))


def _split_dot_left(a_bf16, b):
    hi = b.astype(BF16)
    lo = (b - hi.astype(F32)).astype(BF16)
    return (jnp.dot(a_bf16, hi, preferred_element_type=F32)
            + jnp.dot(a_bf16, lo, preferred_element_type=F32))


def _lane_iota(shape):
    return lax.broadcasted_iota(jnp.int32, shape, len(shape) - 1)


def _row_iota(shape):
    return lax.broadcasted_iota(jnp.int32, shape, 0)


def _rope(xb, cos, sin_up, sin_dn, half):
    return (xb * cos + pltpu.roll(xb, half, 1) * sin_up
            + pltpu.roll(xb, LANES - half, 1) * sin_dn)


def _half_rms(xb, w_row):
    lo = _lane_iota(xb.shape) < 64
    sq = xb * xb
    s_lo = jnp.sum(jnp.where(lo, sq, 0.0), axis=-1, keepdims=True)
    s_hi = jnp.sum(jnp.where(lo, 0.0, sq), axis=-1, keepdims=True)
    ms = jnp.where(lo, s_lo, s_hi) * (1.0 / 64.0)
    return xb * lax.rsqrt(ms + NORM_EPS) * w_row


def _mod_kernel(c_ref, w_ref, b_ref, o_ref):
    c = c_ref[...]
    o_ref[...] = _bdot(_silu(c), w_ref[...]) + b_ref[...]


def _modulation(cvec, w_mod, b_mod):
    tn = 1536
    n_out = 6 * D_MODEL
    return pl.pallas_call(
        _mod_kernel,
        grid=(DEPTH, n_out // tn),
        in_specs=[pl.BlockSpec((SUBLANES, D_MODEL), lambda l, j: (0, 0)),
                  pl.BlockSpec((None, D_MODEL, tn), lambda l, j: (l, 0, j)),
                  pl.BlockSpec((None, 1, tn), lambda l, j: (l, 0, j))],
        out_specs=pl.BlockSpec((None, SUBLANES, tn), lambda l, j: (l, 0, j)),
        out_shape=jax.ShapeDtypeStruct((DEPTH, SUBLANES, n_out), F32),
        compiler_params=_cparams(2),
        name="modulation",
    )(cvec, w_mod, b_mod.reshape(DEPTH, 1, n_out))


def _mod_spec(layer, latent, seq_len, tm, k):
    if latent:
        assert seq_len % tm == 0
    tiles_per_seq = max(seq_len // tm, 1)

    def index_map(i):
        row = 1 + i // tiles_per_seq if latent else 0
        return (layer, row, k, 0, 0)

    return pl.BlockSpec((None, None, None, 1, D_MODEL), index_map)


def _resident(block_shape, index_map):
    return pl.BlockSpec(block_shape, index_map, pipeline_mode=pl.Buffered(1))


def _inproj_kernel(latent, *refs):
    if latent:
        (x_ref, sh_ref, sc_ref, g_ref, w_ref, qn_ref, kn_ref, cn_ref,
         c64_ref, su64_ref, sd64_ref, cm_ref, sum_ref, sdm_ref,
         ssd_ref, ret_ref, aq_ref, ak_ref, av_ref, mq_ref, ckv_ref, kr_ref) = refs
    else:
        (x_ref, sh_ref, sc_ref, g_ref, w_ref, qn_ref, kn_ref, cn_ref,
         ssd_ref, ret_ref, aq_ref, ak_ref, av_ref, mq_ref, ckv_ref, kr_ref) = refs

    half = x_ref.shape[0] // 2
    for r in range(2):
        rows = slice(r * half, (r + 1) * half)
        h = _rms(x_ref[rows, :], g_ref[...]) * (1.0 + sc_ref[...]) + sh_ref[...]
        hb = h.astype(BF16)

        def proj(c0, width):
            return jnp.dot(hb, w_ref[:, c0:c0 + width], preferred_element_type=F32)

        def rope64(xb):
            if not latent:
                return xb
            return _rope(xb, c64_ref[rows, :], su64_ref[rows, :], sd64_ref[rows, :], 16)

        pa = proj(SSD_W + RET_W, 1536)
        for j in range(4):
            q = _half_rms(pa[:, j * LANES:(j + 1) * LANES], qn_ref[...])
            aq_ref[rows, j * LANES:(j + 1) * LANES] = rope64(q)
        ak_ref[rows, :] = rope64(_half_rms(pa[:, 512:640], kn_ref[...]))
        av_ref[rows, :] = pa[:, 640:768]
        mq_ref[rows, :] = pa[:, 768:768 + MLA_Q_RANK]
        ckv_ref[rows, :] = _rms(pa[:, 768 + MLA_Q_RANK:768 + MLA_Q_RANK + MLA_KV_RANK], cn_ref[...])
        kr = pa[:, 1408:1536]
        if latent:
            kr = _rope(kr, cm_ref[rows, :], sum_ref[rows, :], sdm_ref[rows, :], 8)
        kr_ref[rows, :] = kr

        pr = proj(SSD_W, RET_W)
        for j in range(2):
            ret_ref[rows, j * LANES:(j + 1) * LANES] = rope64(pr[:, j * LANES:(j + 1) * LANES])
        for j in range(2, 4):
            ret_ref[rows, j * LANES:(j + 1) * LANES] = rope64(pr[:, j * LANES:(j + 1) * LANES] * 0.125)
        ret_ref[rows, 512:RET_W] = pr[:, 512:RET_W]

        ssd_ref[rows, :] = proj(0, SSD_W)


IN_TOTAL = SSD_W + RET_W + 768 + 768


def _inproj(x2d, mod5, layer, latent, seq_len, g_pre, w_in_p, qn, kn, cn, tabs):
    n = x2d.shape[0]
    tm = TM_ROWS
    tiles_per_seq = max(seq_len // tm, 1)
    row = lambda i: (i, 0)
    const = lambda i: (0, 0)
    in_specs = [pl.BlockSpec((tm, D_MODEL), row),
                _mod_spec(layer, latent, seq_len, tm, 0),
                _mod_spec(layer, latent, seq_len, tm, 1),
                pl.BlockSpec((1, D_MODEL), const),
                _resident((None, D_MODEL, IN_TOTAL), lambda i: (layer, 0, 0)),
                pl.BlockSpec((1, LANES), const),
                pl.BlockSpec((1, LANES), const),
                pl.BlockSpec((1, MLA_KV_RANK), const)]
    args = [x2d, mod5, mod5, g_pre, w_in_p, qn, kn, cn]
    if latent:
        tab = pl.BlockSpec((tm, LANES), lambda i: (i % tiles_per_seq, 0))
        in_specs += [tab] * 6
        args += list(tabs)
    widths = (SSD_W, RET_W, 512, LANES, LANES, MLA_Q_RANK, MLA_KV_RANK, LANES)
    return pl.pallas_call(
        functools.partial(_inproj_kernel, latent),
        grid=(n // tm,),
        in_specs=in_specs,
        out_specs=[pl.BlockSpec((tm, w), row) for w in widths],
        out_shape=[jax.ShapeDtypeStruct((n, w), F32) for w in widths],
        compiler_params=_cparams(1),
        name="inproj_lat" if latent else "inproj_ctx",
    )(*args)


def _ssd_kernel(latent, seq_len, *refs):
    if latent:
        (in_ref, cw_ref, cb_ref, dtb_ref, alog_ref, dsk_ref, nw_ref, ex_ref, s0_ref,
         o_ref, xc_ref, dt_ref, cum_ref, sf_ref, sb_ref, cdb_ref, run_ref) = refs
    else:
        (in_ref, cw_ref, cb_ref, dtb_ref, alog_ref, dsk_ref, nw_ref, ex_ref,
         o_ref, st_ref, xc_ref, dt_ref, cum_ref, sf_ref, sb_ref, cdb_ref, run_ref) = refs
    T = SCAN_CHUNK
    nc = seq_len // T
    ii = _row_iota((T, T))
    jj = _lane_iota((T, T))
    tri_lo = (jj <= ii).astype(BF16)
    tri_up = (jj >= ii).astype(BF16)
    lane_c = _lane_iota((T, LANES))
    row_c = _row_iota((T, SSD_CONV_CH))
    fwd_lane = lane_c < SSD_HEADS
    a_row = -jnp.exp(alog_ref[...])
    expand = ex_ref[...]

    if latent:
        run_ref[...] = s0_ref[...]
    else:
        run_ref[...] = jnp.zeros(run_ref.shape, F32)

    def phase0(c, carry):
        r0 = pl.multiple_of(c * T, T)
        x_cur = in_ref[pl.ds(r0, T), 512:1280]
        prev_blk = in_ref[pl.ds(pl.multiple_of(jnp.maximum(r0 - SUBLANES, 0), SUBLANES), SUBLANES), 512:1280]
        next_blk = in_ref[pl.ds(pl.multiple_of(jnp.minimum(r0 + T, seq_len - SUBLANES), SUBLANES), SUBLANES), 512:1280]
        prev_row = jnp.where(c > 0, prev_blk[SUBLANES - 1:SUBLANES, :], 0.0)
        next_row = jnp.where(c < nc - 1, next_blk[0:1, :], 0.0)
        x_prev = jnp.where(row_c == 0, prev_row, pltpu.roll(x_cur, 1, 0))
        x_next = jnp.where(row_c == T - 1, next_row, pltpu.roll(x_cur, T - 1, 0))
        xc = _silu(cb_ref[...] + x_prev * cw_ref[0:1, :] + x_cur * cw_ref[1:2, :] + x_next * cw_ref[2:3, :])
        xc_ref[pl.ds(r0, T), :] = xc

        dt = _softplus(in_ref[pl.ds(r0, T), 1280:SSD_W] + dtb_ref[...])
        la = dt * a_row
        cum = jnp.where(fwd_lane, _split_dot_left(tri_lo, la), _split_dot_left(tri_up, la))
        tot = jnp.where(fwd_lane[0:1], cum[T - 1:T, :], cum[0:1, :])
        dt_ref[pl.ds(r0, T), :] = dt
        cum_ref[pl.ds(r0, T), :] = cum
        w = dt * jnp.exp(tot - cum)
        wcd = jnp.concatenate([w, jnp.broadcast_to(jnp.exp(tot), (SUBLANES, LANES))], axis=0)
        wcd_x = _split_dot(wcd, expand)
        w_x = wcd_x[0:T]
        cd_x = wcd_x[T:T + 1]
        cdb_ref[c] = jnp.broadcast_to(cd_x[:, 512:1024], (SUBLANES, 512))
        xs = xc[:, 0:512]
        bm = xc[:, 512:640].astype(BF16)
        for p in range(4):
            g = p // 2
            sl = slice(p * LANES, (p + 1) * LANES)
            vw = jnp.concatenate([xs[:, sl] * w_x[:, sl], xs[:, sl] * w_x[:, 512 + p * LANES:512 + (p + 1) * LANES]],
                                 axis=1)
            cs = _bdot_tn(bm, vw)[g * 64:(g + 1) * 64]
            s_run = run_ref[p]
            sf_ref[c, p] = s_run[:, 0:LANES]
            sb_ref[c, p] = cs[:, LANES:2 * LANES]
            new_f = s_run[:, 0:LANES] * cd_x[:, sl] + cs[:, 0:LANES]
            run_ref[p] = jnp.concatenate([new_f, s_run[:, LANES:2 * LANES]], axis=1)
        return carry

    lax.fori_loop(0, nc, phase0, 0)

    def bwd_states(k, carry):
        c = nc - 1 - k
        cd = cdb_ref[c]
        for p in range(4):
            s_run = run_ref[p]
            cs_b = sb_ref[c, p]
            sb_ref[c, p] = s_run[:, LANES:2 * LANES]
            new_b = s_run[:, LANES:2 * LANES] * cd[0:1, p * LANES:(p + 1) * LANES] + cs_b
            run_ref[p] = jnp.concatenate([s_run[:, 0:LANES], new_b], axis=1)
        return carry

    lax.fori_loop(0, nc, bwd_states, 0)
    if not latent:
        for p in range(4):
            s_fin = run_ref[p]
            for d in range(2):
                for hh in range(2):
                    c0 = d * LANES + hh * 64
                    st_ref[d, 2 * p + hh] = s_fin[:, c0:c0 + 64]

    lower = jj <= ii
    upper = jj >= ii
    zeros64 = jnp.zeros((64, 2 * LANES), F32)

    def phase1(c, carry):
        r0 = pl.multiple_of(c * T, T)
        xc = xc_ref[pl.ds(r0, T), :]
        xs = xc[:, 0:512]
        bm = xc[:, 512:640].astype(BF16)
        cm = xc[:, 640:768]
        dt = dt_ref[pl.ds(r0, T), :]
        cum = cum_ref[pl.ds(r0, T), :]
        cum_t = cum.T
        dt_t = dt.T
        e_x = _split_dot(jnp.exp(cum), expand)
        gmat = [_bdot_nt(jnp.where((lane_c < 64) == (g == 0), cm, 0.0), bm) for g in range(2)]
        cmb = cm.astype(BF16)
        outs = []
        for p in range(4):
            g = p // 2
            sl = slice(p * LANES, (p + 1) * LANES)
            xs_p = xs[:, sl].astype(BF16)
            halves = []
            for h in (2 * p, 2 * p + 1):
                df = jnp.exp(jnp.where(lower, cum[:, h:h + 1] - cum_t[h:h + 1, :], NEG_BIG)) * dt_t[h:h + 1, :]
                hb = SSD_HEADS + h
                db = jnp.exp(jnp.where(upper, cum[:, hb:hb + 1] - cum_t[hb:hb + 1, :], NEG_BIG)) * dt_t[hb:hb + 1, :]
                m_h = (gmat[g] * (df + db)).astype(BF16)
                halves.append(jnp.dot(m_h, xs_p, preferred_element_type=F32))
            o_p = jnp.where(lane_c < 64, halves[0], halves[1])
            s_in = jnp.concatenate([sf_ref[c, p], sb_ref[c, p]], axis=1)
            s_pad = jnp.concatenate([s_in, zeros64] if g == 0 else [zeros64, s_in], axis=0)
            oi = jnp.dot(cmb, s_pad.astype(BF16), preferred_element_type=F32)
            o_p = o_p + oi[:, 0:LANES] * e_x[:, sl] + oi[:, LANES:2 * LANES] * e_x[:, 512 + p * LANES:512 + (p + 1) * LANES]
            outs.append(o_p)
        y = jnp.concatenate(outs, axis=1) + dsk_ref[...] * xs
        z = in_ref[pl.ds(r0, T), 0:512]
        o_ref[pl.ds(r0, T), :] = _rms(y * _silu(z), nw_ref[...])
        return carry

    lax.fori_loop(0, nc, phase1, 0)


def _ssd_mixer(ssd_in, bsz, seq_len, latent, cw, cb, dtb, alog, dsk, nw, expand, s0):
    nc = seq_len // SCAN_CHUNK
    const2 = lambda b: (0, 0)
    per_b3 = lambda b: (b, 0, 0)
    per_b4 = lambda b: (b, 0, 0, 0)
    in_specs = [pl.BlockSpec((None, seq_len, SSD_W), per_b3),
                pl.BlockSpec((3, SSD_CONV_CH), const2),
                pl.BlockSpec((1, SSD_CONV_CH), const2),
                pl.BlockSpec((1, LANES), const2),
                pl.BlockSpec((1, LANES), const2),
                pl.BlockSpec((1, SSD_INNER), const2),
                pl.BlockSpec((1, SSD_INNER), const2),
                pl.BlockSpec((LANES, 1024), const2)]
    args = [ssd_in.reshape(bsz, seq_len, SSD_W), cw, cb, dtb, alog, dsk, nw, expand]
    out_specs = [pl.BlockSpec((None, seq_len, SSD_INNER), per_b3)]
    out_shape = [jax.ShapeDtypeStruct((bsz, seq_len, SSD_INNER), F32)]
    if latent:
        in_specs.append(pl.BlockSpec((None, 4, 64, 2 * LANES), per_b4))
        args.append(s0)
    else:
        out_specs.append(pl.BlockSpec((None, 2, SSD_HEADS, SSD_STATE, SSD_HEAD_DIM), lambda b: (b, 0, 0, 0, 0)))
        out_shape.append(jax.ShapeDtypeStruct((bsz, 2, SSD_HEADS, SSD_STATE, SSD_HEAD_DIM), F32))
    scratch = [pltpu.VMEM((seq_len, SSD_CONV_CH), F32),
               pltpu.VMEM((seq_len, LANES), F32),
               pltpu.VMEM((seq_len, LANES), F32),
               pltpu.VMEM((nc, 4, 64, LANES), F32),
               pltpu.VMEM((nc, 4, 64, LANES), F32),
               pltpu.VMEM((nc, SUBLANES, 512), F32),
               pltpu.VMEM((4, 64, 2 * LANES), F32)]
    res = pl.pallas_call(
        functools.partial(_ssd_kernel, latent, seq_len),
        grid=(bsz,),
        in_specs=in_specs, out_specs=out_specs, out_shape=out_shape,
        scratch_shapes=scratch,
        compiler_params=_cparams(1),
        name="ssd_lat" if latent else "ssd_ctx",
    )(*args)
    return res


def _ret_kernel(latent, seq_len, *refs):
    if latent:
        (in_ref, lg_ref, gn_ref, s0_ref,
         o_ref, dc_ref, we_ref, cd_ref, sf_ref, sb_ref, run_ref) = refs
    else:
        (in_ref, lg_ref, gn_ref,
         o_ref, st_ref, dc_ref, we_ref, cd_ref, sf_ref, sb_ref, run_ref) = refs
    T = SCAN_CHUNK
    nc = seq_len // T

    @pl.when(pl.program_id(0) == 0)
    def _tables():
        la = -_softplus(-lg_ref[...])
        ii = _row_iota((T, T)).astype(F32)
        jj = _lane_iota((T, T)).astype(F32)
        for h in range(RET_HEADS):
            la_f = la[h:h + 1, :]
            la_b = la[RET_HEADS + h:RET_HEADS + h + 1, :]
            dc_ref[h] = (jnp.exp(jnp.where(jj <= ii, (ii - jj) * la_f, NEG_BIG))
                         + jnp.exp(jnp.where(jj >= ii, (jj - ii) * la_b, NEG_BIG)))
            we_ref[h, 0] = jnp.exp((T - 1.0 - ii) * la_f)
            we_ref[h, 1] = jnp.exp(ii * la_b)
            we_ref[h, 2] = jnp.exp((ii + 1.0) * la_f)
            we_ref[h, 3] = jnp.exp((T - ii) * la_b)
            cd_ref[h] = jnp.concatenate([jnp.broadcast_to(jnp.exp(T * la_f), (SUBLANES, LANES)),
                                         jnp.broadcast_to(jnp.exp(T * la_b), (SUBLANES, LANES))], axis=1)

    if latent:
        run_ref[...] = s0_ref[...]
    else:
        run_ref[...] = jnp.zeros(run_ref.shape, F32)

    def phase0(c, carry):
        r0 = pl.multiple_of(c * T, T)
        for h in range(RET_HEADS):
            blk = h // 2
            k_blk = in_ref[pl.ds(r0, T), 256 + blk * LANES:256 + (blk + 1) * LANES].astype(BF16)
            v_h = in_ref[pl.ds(r0, T), 512 + h * LANES:512 + (h + 1) * LANES]
            vw = jnp.concatenate([v_h * we_ref[h, 0], v_h * we_ref[h, 1]], axis=1)
            cs = _bdot_tn(k_blk, vw)[(h % 2) * 64:(h % 2 + 1) * 64]
            s_run = run_ref[h]
            sf_ref[c, h] = s_run[:, 0:LANES]
            sb_ref[c, h] = cs[:, LANES:2 * LANES]
            new_f = s_run[:, 0:LANES] * cd_ref[h, 0:1, 0:LANES] + cs[:, 0:LANES]
            run_ref[h] = jnp.concatenate([new_f, s_run[:, LANES:2 * LANES]], axis=1)
        return carry

    lax.fori_loop(0, nc, phase0, 0)

    def bwd_states(k, carry):
        c = nc - 1 - k
        for h in range(RET_HEADS):
            s_run = run_ref[h]
            cs_b = sb_ref[c, h]
            sb_ref[c, h] = s_run[:, LANES:2 * LANES]
            new_b = s_run[:, LANES:2 * LANES] * cd_ref[h, 0:1, LANES:2 * LANES] + cs_b
            run_ref[h] = jnp.concatenate([s_run[:, 0:LANES], new_b], axis=1)
        return carry

    lax.fori_loop(0, nc, bwd_states, 0)
    if not latent:
        for h in range(RET_HEADS):
            for d in range(2):
                st_ref[d, h] = run_ref[h, :, d * LANES:(d + 1) * LANES]

    lane_c = _lane_iota((T, LANES))
    zeros64 = jnp.zeros((64, 2 * LANES), F32)

    def phase1(c, carry):
        r0 = pl.multiple_of(c * T, T)
        for h in range(RET_HEADS):
            blk = h // 2
            q_blk = in_ref[pl.ds(r0, T), blk * LANES:(blk + 1) * LANES]
            k_blk = in_ref[pl.ds(r0, T), 256 + blk * LANES:256 + (blk + 1) * LANES].astype(BF16)
            v_h = in_ref[pl.ds(r0, T), 512 + h * LANES:512 + (h + 1) * LANES].astype(BF16)
            g_h = in_ref[pl.ds(r0, T), 1024 + h * LANES:1024 + (h + 1) * LANES]
            q_m = jnp.where((lane_c < 64) == (h % 2 == 0), q_blk, 0.0)
            m_h = (_bdot_nt(q_m, k_blk) * dc_ref[h]).astype(BF16)
            o = jnp.dot(m_h, v_h, preferred_element_type=F32)
            s_in = jnp.concatenate([sf_ref[c, h], sb_ref[c, h]], axis=1)
            s_pad = jnp.concatenate([s_in, zeros64] if h % 2 == 0 else [zeros64, s_in], axis=0)
            oi = _bdot(q_blk, s_pad)
            o = o + oi[:, 0:LANES] * we_ref[h, 2] + oi[:, LANES:2 * LANES] * we_ref[h, 3]
            oc = o - jnp.mean(o, axis=-1, keepdims=True)
            on = oc * lax.rsqrt(jnp.mean(oc * oc, axis=-1, keepdims=True) + NORM_EPS)
            o_ref[pl.ds(r0, T), h * LANES:(h + 1) * LANES] = (
                on * gn_ref[:, h * LANES:(h + 1) * LANES] * _silu(g_h))
        return carry

    lax.fori_loop(0, nc, phase1, 0)


def _ret_mixer(ret_in, bsz, seq_len, latent, logit_rows, gn_w, s0):
    nc = seq_len // SCAN_CHUNK
    T = SCAN_CHUNK
    const2 = lambda b: (0, 0)
    per_b3 = lambda b: (b, 0, 0)
    per_b4 = lambda b: (b, 0, 0, 0)
    in_specs = [pl.BlockSpec((None, seq_len, RET_W), per_b3),
                pl.BlockSpec((SUBLANES, LANES), const2),
                pl.BlockSpec((1, 512), const2)]
    args = [ret_in.reshape(bsz, seq_len, RET_W), logit_rows, gn_w]
    out_specs = [pl.BlockSpec((None, seq_len, 512), per_b3)]
    out_shape = [jax.ShapeDtypeStruct((bsz, seq_len, 512), F32)]
    if latent:
        in_specs.append(pl.BlockSpec((None, RET_HEADS, 64, 2 * LANES), per_b4))
        args.append(s0)
    else:
        out_specs.append(pl.BlockSpec((None, 2, RET_HEADS, RET_QK, RET_V), lambda b: (b, 0, 0, 0, 0)))
        out_shape.append(jax.ShapeDtypeStruct((bsz, 2, RET_HEADS, RET_QK, RET_V), F32))
    scratch = [pltpu.VMEM((RET_HEADS, T, T), F32),
               pltpu.VMEM((RET_HEADS, 4, T, LANES), F32),
               pltpu.VMEM((RET_HEADS, SUBLANES, 2 * LANES), F32),
               pltpu.VMEM((nc, RET_HEADS, 64, LANES), F32),
               pltpu.VMEM((nc, RET_HEADS, 64, LANES), F32),
               pltpu.VMEM((RET_HEADS, 64, 2 * LANES), F32)]
    return pl.pallas_call(
        functools.partial(_ret_kernel, latent, seq_len),
        grid=(bsz,),
        in_specs=in_specs, out_specs=out_specs, out_shape=out_shape,
        scratch_shapes=scratch,
        compiler_params=_cparams(1),
        name="ret_lat" if latent else "ret_ctx",
    )(*args)


def _softmax_pv(s, v):
    m = jnp.max(s, axis=-1, keepdims=True)
    p = jnp.exp(s - m)
    l = jnp.sum(p, axis=-1, keepdims=True)
    return jnp.dot(p.astype(BF16), v, preferred_element_type=F32) / l


def _att_kernel(latent, tq, *refs):
    if latent:
        q_ref, kn_ref, vn_ref, kc_ref, vc_ref, o_ref, kb_ref, vb_ref = refs
    else:
        q_ref, kn_ref, vn_ref, o_ref, kb_ref, vb_ref = refs

    @pl.when(pl.program_id(1) == 0)
    def _fill():
        if latent:
            kb_ref[0:PAST_LEN, :] = kc_ref[...].astype(BF16)
            vb_ref[0:PAST_LEN, :] = vc_ref[...].astype(BF16)
            kb_ref[PAST_LEN:, :] = kn_ref[...].astype(BF16)
            vb_ref[PAST_LEN:, :] = vn_ref[...].astype(BF16)
        else:
            kb_ref[...] = kn_ref[...].astype(BF16)
            vb_ref[...] = vn_ref[...].astype(BF16)

    sub = min(tq, ATT_SUB_ROWS)
    lane_q = _lane_iota((sub, LANES))
    for i in range(tq // sub):
        rows = slice(i * sub, (i + 1) * sub)
        acc = []
        for g in range(ATT_KV_HEADS):
            qs = jnp.concatenate(
                [jnp.where((lane_q < 64) == (g == 0), q_ref[rows, j * LANES:(j + 1) * LANES] * 0.125, 0.0).astype(BF16)
                 for j in range(4)], axis=0)
            s = lax.dot_general(qs, kb_ref[...], (((1,), (1,)), ((), ())), preferred_element_type=F32)
            acc.append(_softmax_pv(s, vb_ref[...]))
        for j in range(4):
            o_ref[rows, j * LANES:(j + 1) * LANES] = jnp.where(lane_q < 64, acc[0][j * sub:(j + 1) * sub],
                                                               acc[1][j * sub:(j + 1) * sub])


def _att_mixer(aq, ak, av, bsz, seq_len, latent, k_ctx, v_ctx):
    tq = 512 if latent else 256
    lk = seq_len + (PAST_LEN if latent else 0)
    qmap = lambda b, i: (b, i, 0)
    bmap = lambda b, i: (b, 0, 0)
    in_specs = [pl.BlockSpec((None, tq, 512), qmap),
                pl.BlockSpec((None, seq_len, LANES), bmap),
                pl.BlockSpec((None, seq_len, LANES), bmap)]
    args = [aq.reshape(bsz, seq_len, 512), ak.reshape(bsz, seq_len, LANES), av.reshape(bsz, seq_len, LANES)]
    if latent:
        in_specs += [pl.BlockSpec((None, PAST_LEN, LANES), bmap)] * 2
        args += [k_ctx, v_ctx]
    return pl.pallas_call(
        functools.partial(_att_kernel, latent, tq),
        grid=(bsz, seq_len // tq),
        in_specs=in_specs,
        out_specs=pl.BlockSpec((None, tq, 512), qmap),
        out_shape=jax.ShapeDtypeStruct((bsz, seq_len, 512), F32),
        scratch_shapes=[pltpu.VMEM((lk, LANES), BF16), pltpu.VMEM((lk, LANES), BF16)],
        compiler_params=_cparams(2),
        name="att_lat" if latent else "att_ctx",
    )(*args)


def _mla_kernel(latent, tq, *refs):
    if latent:
        (mq_ref, cn_ref, rn_ref, cc_ref, rc_ref, qn_ref, wq_ref, wk_ref, wv_ref,
         cos_ref, su_ref, sd_ref, o_ref, kb_ref, vb_ref) = refs
    else:
        (mq_ref, cn_ref, rn_ref, qn_ref, wq_ref, wk_ref, wv_ref, o_ref, kb_ref, vb_ref) = refs

    def fill(r0, ckv, kr):
        cb = ckv.astype(BF16)
        n = ckv.shape[0]
        kn = jnp.dot(cb, wk_ref[...], preferred_element_type=F32)
        for h in range(MLA_HEADS):
            kb_ref[r0:r0 + n, h * LANES:(h + 1) * LANES] = (kn[:, h * LANES:(h + 1) * LANES] + kr).astype(BF16)
        vb_ref[r0:r0 + n, :] = jnp.dot(cb, wv_ref[...], preferred_element_type=F32).astype(BF16)

    @pl.when(pl.program_id(1) == 0)
    def _fill():
        if latent:
            fill(0, cc_ref[...], rc_ref[...])
            fill(PAST_LEN, cn_ref[...], rn_ref[...])
        else:
            fill(0, cn_ref[...], rn_ref[...])

    q = _bdot(_rms(mq_ref[...], qn_ref[...]), wq_ref[...])
    lane_q = _lane_iota((tq, LANES))
    scale = (MLA_NOPE + MLA_ROPE) ** -0.5
    for p in range(4):
        halves = []
        for h in (2 * p, 2 * p + 1):
            qh = q[:, h * LANES:(h + 1) * LANES]
            if latent:
                qh = _rope(qh, cos_ref[...], su_ref[...], sd_ref[...], 8)
            s = _bdot_nt(qh, kb_ref[:, h * LANES:(h + 1) * LANES]) * scale
            halves.append(_softmax_pv(s, vb_ref[:, p * LANES:(p + 1) * LANES]))
        o_ref[:, p * LANES:(p + 1) * LANES] = jnp.where(lane_q < 64, halves[0], halves[1])


def _mla_mixer(mq, ckv, kr, bsz, seq_len, latent, ckv_ctx, kr_ctx, qn, wq, wk, wv, layer, tabs):
    tq = 512 if latent else 256
    lk = seq_len + (PAST_LEN if latent else 0)
    qmap = lambda b, i: (b, i, 0)
    bmap = lambda b, i: (b, 0, 0)
    const = lambda b, i: (0, 0)
    lyr3 = lambda b, i: (layer, 0, 0)
    in_specs = [pl.BlockSpec((None, tq, MLA_Q_RANK), qmap),
                pl.BlockSpec((None, seq_len, MLA_KV_RANK), bmap),
                pl.BlockSpec((None, seq_len, LANES), bmap)]
    args = [mq.reshape(bsz, seq_len, MLA_Q_RANK), ckv.reshape(bsz, seq_len, MLA_KV_RANK),
            kr.reshape(bsz, seq_len, LANES)]
    if latent:
        in_specs += [pl.BlockSpec((None, PAST_LEN, MLA_KV_RANK), bmap),
                     pl.BlockSpec((None, PAST_LEN, LANES), bmap)]
        args += [ckv_ctx, kr_ctx]
    in_specs += [pl.BlockSpec((1, MLA_Q_RANK), const),
                 _resident((None, MLA_Q_RANK, 1024), lyr3),
                 _resident((None, MLA_KV_RANK, 1024), lyr3),
                 _resident((None, MLA_KV_RANK, 512), lyr3)]
    args += [qn, wq, wk, wv]
    if latent:
        in_specs += [pl.BlockSpec((tq, LANES), lambda b, i: (i, 0))] * 3
        args += list(tabs)
    return pl.pallas_call(
        functools.partial(_mla_kernel, latent, tq),
        grid=(bsz, seq_len // tq),
        in_specs=in_specs,
        out_specs=pl.BlockSpec((None, tq, 512), qmap),
        out_shape=jax.ShapeDtypeStruct((bsz, seq_len, 512), F32),
        scratch_shapes=[pltpu.VMEM((lk, 1024), BF16), pltpu.VMEM((lk, 512), BF16)],
        compiler_params=_cparams(2),
        name="mla_lat" if latent else "mla_ctx",
    )(*args)


def _merge_kernel(x_ref, sh_ref, sc_ref, ga_ref, gpre_ref, gpost_ref, o1_ref, o2_ref, o3_ref, o4_ref,
                  wm_ref, bm_ref, wb_ref, wo_ref, y_ref):
    x = x_ref[...]
    hb = (_rms(x, gpre_ref[...]) * (1.0 + sc_ref[...]) + sh_ref[...]).astype(BF16)
    merged = None
    for k, o_ref in enumerate((o1_ref, o2_ref, o3_ref, o4_ref)):
        cols = slice(k * D_MODEL, (k + 1) * D_MODEL)
        gate = jax.nn.sigmoid(jnp.dot(hb, wm_ref[:, cols], preferred_element_type=F32) + bm_ref[:, cols])
        term = gate * _bdot(o_ref[...], wb_ref[k])
        merged = term if merged is None else merged + term
    m = _bdot(merged, wo_ref[...])
    y_ref[...] = x + ga_ref[...] * _rms(m, gpost_ref[...])


def _merge(x2d, mod5, layer, latent, seq_len, g_pre, g_post, branches, w_merge, b_merge, w_br, w_out):
    n = x2d.shape[0]
    tm = TM_ROWS
    row = lambda i: (i, 0)
    const = lambda i: (0, 0)
    in_specs = ([pl.BlockSpec((tm, D_MODEL), row)]
                + [_mod_spec(layer, latent, seq_len, tm, k) for k in (0, 1, 2)]
                + [pl.BlockSpec((1, D_MODEL), const)] * 2
                + [pl.BlockSpec((tm, 512), row)] * 4
                + [_resident((None, D_MODEL, 4 * D_MODEL), lambda i: (layer, 0, 0)),
                   pl.BlockSpec((1, 4 * D_MODEL), const),
                   _resident((None, 4, 512, D_MODEL), lambda i: (layer, 0, 0, 0)),
                   _resident((None, D_MODEL, D_MODEL), lambda i: (layer, 0, 0))])
    return pl.pallas_call(
        _merge_kernel,
        grid=(n // tm,),
        in_specs=in_specs,
        out_specs=pl.BlockSpec((tm, D_MODEL), row),
        out_shape=jax.ShapeDtypeStruct((n, D_MODEL), F32),
        compiler_params=_cparams(1),
        name="merge_lat" if latent else "merge_ctx",
    )(x2d, mod5, mod5, mod5, g_pre, g_post, *[b.reshape(n, 512) for b in branches],
      w_merge, b_merge, w_br, w_out)


def _ffn_kernel(tm, seq_len, x_ref, xp_ref, xn_ref, sh_ref, sc_ref, gf_ref, gpre_ref, gpost_ref,
                wu_ref, cw_ref, cb_ref, wd_ref, y_ref):
    tiles_per_seq = seq_len // tm
    i = pl.program_id(0)
    first = (i % tiles_per_seq) == 0
    last = (i % tiles_per_seq) == tiles_per_seq - 1

    def nm(x):
        return _rms(x, gpre_ref[...]) * (1.0 + sc_ref[...]) + sh_ref[...]

    x = x_ref[...]
    h_prev = jnp.where(first, 0.0, nm(xp_ref[...]))
    h_next = jnp.where(last, 0.0, nm(xn_ref[...]))
    hb = jnp.concatenate([h_prev, nm(x), h_next], axis=0).astype(BF16)
    rows = tm + 2 * SUBLANES

    def conv_cols(c0):
        cols = slice(c0, c0 + FF_HALF)
        p = jnp.dot(hb, wu_ref[:, cols], preferred_element_type=F32)
        w = cw_ref[:, cols]
        u = (pltpu.roll(p, 1, 0) * w[0:1] + p * w[1:2] + pltpu.roll(p, rows - 1, 0) * w[2:3])
        return u[SUBLANES:SUBLANES + tm] + cb_ref[:, cols]

    acc = None
    for j in range(D_FF // FF_HALF):
        up = conv_cols(j * FF_HALF)
        gate = conv_cols(D_FF + j * FF_HALF)
        part = _bdot(_silu(gate) * up, wd_ref[j * FF_HALF:(j + 1) * FF_HALF, :])
        acc = part if acc is None else acc + part
    y_ref[...] = x + gf_ref[...] * _rms(acc, gpost_ref[...])


def _ffn(x2d, mod5, layer, latent, seq_len, g_pre, g_post, w_up, conv_w, conv_b, w_down):
    n = x2d.shape[0]
    tm = min(TM_ROWS, seq_len)
    assert seq_len % tm == 0
    hb = tm // SUBLANES
    n_hb = n // SUBLANES
    row = lambda i: (i, 0)
    const = lambda i: (0, 0)
    lyr3 = lambda i: (layer, 0, 0)
    in_specs = ([pl.BlockSpec((tm, D_MODEL), row),
                 pl.BlockSpec((SUBLANES, D_MODEL), lambda i: (jnp.maximum(i * hb - 1, 0), 0)),
                 pl.BlockSpec((SUBLANES, D_MODEL), lambda i: (jnp.minimum((i + 1) * hb, n_hb - 1), 0))]
                + [_mod_spec(layer, latent, seq_len, tm, k) for k in (3, 4, 5)]
                + [pl.BlockSpec((1, D_MODEL), const)] * 2
                + [_resident((None, D_MODEL, 2 * D_FF), lyr3),
                   _resident((None, 3, 2 * D_FF), lyr3),
                   _resident((None, 1, 2 * D_FF), lyr3),
                   _resident((None, D_FF, D_MODEL), lyr3)])
    return pl.pallas_call(
        functools.partial(_ffn_kernel, tm, seq_len),
        grid=(n // tm,),
        in_specs=in_specs,
        out_specs=pl.BlockSpec((tm, D_MODEL), row),
        out_shape=jax.ShapeDtypeStruct((n, D_MODEL), F32),
        compiler_params=_cparams(1),
        name="ffn_lat" if latent else "ffn_ctx",
    )(x2d, x2d, x2d, mod5, mod5, mod5, g_pre, g_post, w_up, conv_w, conv_b, w_down)


def _rope_tables(seq_len, head_dim, lane_of_dim):
    d_axis = head_dim // 2
    t = np.arange(seq_len)
    pos = np.stack([(t // GRID_W).astype(np.float32), (t % GRID_W).astype(np.float32)], axis=0)
    inv_freq = (np.float32(ROPE_BASE) ** (-np.arange(0, d_axis, 2, dtype=np.float32) / np.float32(d_axis))).astype(np.float32)
    cos = np.ones((seq_len, LANES), np.float32)
    s_up = np.zeros((seq_len, LANES), np.float32)
    s_dn = np.zeros((seq_len, LANES), np.float32)
    for lane, d in enumerate(lane_of_dim):
        if d < 0:
            continue
        axis, j = divmod(d, d_axis)
        second = j >= d_axis // 2
        ang = (pos[axis] * inv_freq[j % (d_axis // 2)]).astype(np.float32)
        cos[:, lane] = np.cos(ang)
        if second:
            s_up[:, lane] = np.sin(ang)
        else:
            s_dn[:, lane] = -np.sin(ang)
    return jnp.asarray(cos), jnp.asarray(s_up), jnp.asarray(s_dn)


def _ssd_expand_matrix():
    e = np.zeros((LANES, 1024), np.float32)
    for d in range(2):
        for h in range(SSD_HEADS):
            e[d * SSD_HEADS + h, d * 512 + h * 64:d * 512 + (h + 1) * 64] = 1.0
    return jnp.asarray(e, dtype=BF16)


def _matmul_weights(w_in, mla_w_uq, mla_w_ukv, w_br_ssd, w_br_ret, w_br_att, w_br_mla, w_merge, w_out,
                    w_ffn_up, ffn_conv_w, ffn_conv_b, w_ffn_down):
    off = np.cumsum([0, 512, 768, 16, 256, 256, 512, 512, 512, 128, 128, 384, 288])
    col = lambda k: w_in[:, :, off[k]:off[k + 1]]
    zeros = lambda n: jnp.zeros((DEPTH, D_MODEL, n), F32)
    aq = col(7)
    aq_pairs = [jnp.concatenate([aq[:, :, j * 64:(j + 1) * 64], aq[:, :, (j + 4) * 64:(j + 5) * 64]], axis=2)
                for j in range(4)]
    mckv = col(11)
    w_in_p = jnp.concatenate(
        [col(0), col(1), col(2), zeros(112),
         col(3), col(4), col(5), col(6),
         *aq_pairs, col(8), col(9),
         col(10), mckv[:, :, :MLA_KV_RANK], zeros(64), mckv[:, :, MLA_KV_RANK:], zeros(32)],
        axis=2).astype(BF16)

    uq = mla_w_uq.reshape(DEPTH, MLA_Q_RANK, MLA_HEADS, MLA_NOPE + MLA_ROPE)
    wq = jnp.pad(uq, ((0, 0), (0, 0), (0, 0), (0, 32))).reshape(DEPTH, MLA_Q_RANK, 1024).astype(BF16)
    ukv = mla_w_ukv.reshape(DEPTH, MLA_KV_RANK, MLA_HEADS, MLA_NOPE + MLA_V)
    wk = jnp.pad(ukv[..., :MLA_NOPE], ((0, 0), (0, 0), (0, 0), (0, 64))).reshape(DEPTH, MLA_KV_RANK, 1024).astype(BF16)
    wv = ukv[..., MLA_NOPE:].reshape(DEPTH, MLA_KV_RANK, 512).astype(BF16)

    w_att = (w_br_att.reshape(DEPTH, ATT_KV_HEADS, 4, ATT_HEAD_DIM, D_MODEL).swapaxes(1, 2)
             .reshape(DEPTH, 512, D_MODEL))
    w_br = jnp.stack([w_br_ssd, w_br_ret, w_att, w_br_mla], axis=1).astype(BF16)

    return dict(
        w_in=w_in_p, wq=wq, wk=wk, wv=wv, w_br=w_br,
        w_merge=w_merge.astype(BF16), w_out=w_out.astype(BF16),
        w_up=w_ffn_up.astype(BF16), ffn_cw=ffn_conv_w,
        ffn_cb=ffn_conv_b.reshape(DEPTH, 1, 2 * D_FF),
        w_down=w_ffn_down.astype(BF16),
    )


def _layer_params(i, ssd_conv_w, ssd_conv_b, ssd_dt_bias, ssd_a_log, ssd_d, ssd_norm_w,
                  ret_decay_logit, ret_gn_w, att_q_norm, att_k_norm, mla_q_norm, mla_kv_norm,
                  b_merge, g_pre_mix, g_post_mix, g_pre_ffn, g_post_ffn):
    pad_row = lambda v: jnp.pad(v.reshape(1, -1), ((0, 0), (0, LANES - v.size)))
    return dict(
        qn=jnp.tile(att_q_norm[i], 2).reshape(1, LANES),
        kn=jnp.tile(att_k_norm[i], 2).reshape(1, LANES),
        cn=mla_kv_norm[i].reshape(1, MLA_KV_RANK),
        ssd_cw=ssd_conv_w[i], ssd_cb=ssd_conv_b[i].reshape(1, SSD_CONV_CH),
        ssd_dtb=pad_row(ssd_dt_bias[i]), ssd_alog=pad_row(ssd_a_log[i]),
        ssd_dsk=jnp.repeat(ssd_d[i], SSD_HEAD_DIM).reshape(1, SSD_INNER),
        ssd_nw=ssd_norm_w[i].reshape(1, SSD_INNER),
        ret_logit=jnp.broadcast_to(ret_decay_logit[i].reshape(SUBLANES, 1), (SUBLANES, LANES)),
        ret_gn=ret_gn_w[i].reshape(1, 512),
        mla_qn=mla_q_norm[i].reshape(1, MLA_Q_RANK),
        b_merge=b_merge[i].reshape(1, 4 * D_MODEL),
        g_pre_mix=g_pre_mix[i].reshape(1, D_MODEL), g_post_mix=g_post_mix[i].reshape(1, D_MODEL),
        g_pre_ffn=g_pre_ffn[i].reshape(1, D_MODEL), g_post_ffn=g_post_ffn[i].reshape(1, D_MODEL),
    )


def _ssd_state_to_pairs(s):
    b = s.shape[0]
    s = s.reshape(b, 2, 4, 2, 64, 64).transpose(0, 2, 4, 1, 3, 5)
    return s.reshape(b, 4, 64, 256)


def _ret_state_to_lanes(s):
    b = s.shape[0]
    return s.transpose(0, 2, 3, 1, 4).reshape(b, RET_HEADS, 64, 256)


def _trunk_pass(x2d, mod5, layer, latent, bsz, seq_len, lp, mw, tabs64, tabs_mla, expand, ctx):
    (ssd_in, ret_in, aq, ak, av, mq, ckv, kr) = _inproj(
        x2d, mod5, layer, latent, seq_len, lp['g_pre_mix'], mw['w_in'], lp['qn'], lp['kn'], lp['cn'],
        (tabs64 + tabs_mla) if latent else None)
    ssd_res = _ssd_mixer(ssd_in, bsz, seq_len, latent, lp['ssd_cw'], lp['ssd_cb'], lp['ssd_dtb'],
                         lp['ssd_alog'], lp['ssd_dsk'], lp['ssd_nw'], expand,
                         ctx['ssd'] if latent else None)
    ret_res = _ret_mixer(ret_in, bsz, seq_len, latent, lp['ret_logit'], lp['ret_gn'],
                         ctx['ret'] if latent else None)
    o_att = _att_mixer(aq, ak, av, bsz, seq_len, latent,
                       ctx['att_k'] if latent else None, ctx['att_v'] if latent else None)
    o_mla = _mla_mixer(mq, ckv, kr, bsz, seq_len, latent,
                       ctx['mla_ckv'] if latent else None, ctx['mla_kr'] if latent else None,
                       lp['mla_qn'], mw['wq'], mw['wk'], mw['wv'], layer, tabs_mla)
    x2d = _merge(x2d, mod5, layer, latent, seq_len, lp['g_pre_mix'], lp['g_post_mix'],
                 (ssd_res[0], ret_res[0], o_att, o_mla), mw['w_merge'], lp['b_merge'], mw['w_br'], mw['w_out'])
    x2d = _ffn(x2d, mod5, layer, latent, seq_len, lp['g_pre_ffn'], lp['g_post_ffn'],
               mw['w_up'], mw['ffn_cw'], mw['ffn_cb'], mw['w_down'])
    cache = None
    if not latent:
        cache = dict(ssd=ssd_res[1], ret=ret_res[1],
                     att_k=ak.reshape(bsz, seq_len, ATT_KV_HEADS, ATT_HEAD_DIM),
                     att_v=av.reshape(bsz, seq_len, ATT_KV_HEADS, ATT_HEAD_DIM),
                     mla_ckv=ckv.reshape(bsz, seq_len, MLA_KV_RANK),
                     mla_krope=kr.reshape(bsz, seq_len, LANES)[:, :, 64:64 + MLA_ROPE])
    return x2d, cache


def kernel(x_prompt, x_sample, state_ssd, state_ret, cache_att_k, cache_att_v, cache_mla_ckv, cache_mla_krope, c, c_ctx, w_mod, b_mod, g_pre_mix, g_post_mix, g_pre_ffn, g_post_ffn, w_in, ssd_conv_w, ssd_conv_b, ssd_dt_bias, ssd_a_log, ssd_d, ssd_norm_w, ret_decay_logit, ret_gn_w, att_q_norm, att_k_norm, mla_q_norm, mla_w_uq, mla_kv_norm, mla_w_ukv, w_br_ssd, w_br_ret, w_br_att, w_br_mla, w_merge, b_merge, w_out, w_ffn_up, ffn_conv_w, ffn_conv_b, w_ffn_down):
    cvec = jnp.concatenate([c_ctx[None, :], c, jnp.zeros((SUBLANES - 1 - DEC_BATCH, D_MODEL), F32)], axis=0)
    mod5 = _modulation(cvec, w_mod, b_mod).reshape(DEPTH, SUBLANES, 6, 1, D_MODEL)

    lane_dims_64 = [d % 64 for d in range(LANES)]
    lane_dims_mla = [d - 64 if 64 <= d < 96 else -1 for d in range(LANES)]
    tabs64 = _rope_tables(DEC_SEQ, ATT_HEAD_DIM, lane_dims_64)
    tabs_mla = _rope_tables(DEC_SEQ, MLA_ROPE, lane_dims_mla)
    expand = _ssd_expand_matrix()

    xp = x_prompt.reshape(BATCH * SEQ, D_MODEL)
    xs = x_sample.reshape(DEC_BATCH * DEC_SEQ, D_MODEL)
    caches = []
    mw = _matmul_weights(w_in, mla_w_uq, mla_w_ukv, w_br_ssd, w_br_ret, w_br_att, w_br_mla, w_merge, w_out,
                         w_ffn_up, ffn_conv_w, ffn_conv_b, w_ffn_down)
    for i in range(DEPTH):
        lp = _layer_params(i, ssd_conv_w, ssd_conv_b, ssd_dt_bias, ssd_a_log, ssd_d, ssd_norm_w,
                           ret_decay_logit, ret_gn_w, att_q_norm, att_k_norm, mla_q_norm, mla_kv_norm,
                           b_merge, g_pre_mix, g_post_mix, g_pre_ffn, g_post_ffn)
        xp, cache = _trunk_pass(xp, mod5, i, False, BATCH, SEQ, lp, mw, tabs64, tabs_mla, expand, None)
        caches.append(cache)
        ctx = dict(ssd=_ssd_state_to_pairs(state_ssd[:, i]), ret=_ret_state_to_lanes(state_ret[:, i]),
                   att_k=cache_att_k[:, i].reshape(DEC_BATCH, PAST_LEN, LANES),
                   att_v=cache_att_v[:, i].reshape(DEC_BATCH, PAST_LEN, LANES),
                   mla_ckv=cache_mla_ckv[:, i],
                   mla_kr=jnp.pad(cache_mla_krope[:, i], ((0, 0), (0, 0), (64, 32))))
        xs, _ = _trunk_pass(xs, mod5, i, True, DEC_BATCH, DEC_SEQ, lp, mw, tabs64, tabs_mla, expand, ctx)

    stack = lambda k: jnp.stack([cc[k] for cc in caches], axis=1)
    return (xp.reshape(BATCH, SEQ, D_MODEL), xs.reshape(DEC_BATCH, DEC_SEQ, D_MODEL),
            stack('ssd'), stack('ret'), stack('att_k'), stack('att_v'), stack('mla_ckv'), stack('mla_krope'))
```

```python
import functools

import numpy as np
import jax
import jax.numpy as jnp
from jax import lax
from jax.experimental import pallas as pl
from jax.experimental.pallas import tpu as pltpu

D_MODEL = 1024
BATCH = 32
SEQ = 256
DEPTH = 2
DEC_BATCH = 4
DEC_SEQ = 2048
PAST_LEN = 256
GRID_W = 64
ROPE_BASE = 10000.0
NORM_EPS = 1e-6
SCAN_CHUNK = 128
SSD_HEADS = 8
SSD_HEAD_DIM = 64
SSD_INNER = 512
SSD_GROUPS = 2
SSD_STATE = 64
SSD_CONV_CH = 768
RET_HEADS = 4
RET_QK = 64
RET_V = 128
ATT_HEADS = 8
ATT_KV_HEADS = 2
ATT_HEAD_DIM = 64
MLA_HEADS = 8
MLA_Q_RANK = 384
MLA_KV_RANK = 256
MLA_NOPE = 64
MLA_ROPE = 32
MLA_V = 64
D_FF = 2816

F32 = jnp.float32
BF16 = jnp.bfloat16
LANES = 128
SUBLANES = 8
VMEM_LIMIT = 56 * 1024 * 1024
NEG_BIG = -1e30

SSD_W = 1408
RET_W = 1536
TM_ROWS = 512
RET_UNROLL = 4
SSD_UNROLL = 4
ATT_SUB_ROWS = 128
FF_HALF = D_FF


def _cparams(n_axes):
    return pltpu.CompilerParams(dimension_semantics=("arbitrary",) * n_axes,
                                vmem_limit_bytes=VMEM_LIMIT)


def _silu(x):
    return x * jax.nn.sigmoid(x)


def _softplus(x):
    return jnp.maximum(x, 0.0) + jnp.log1p(jnp.exp(-jnp.abs(x)))


def _rms(x, w):
    ms = jnp.mean(x * x, axis=-1, keepdims=True)
    return x * lax.rsqrt(ms + NORM_EPS) * w


def _bdot(a, b):
    return jnp.dot(a.astype(BF16), b.astype(BF16), preferred_element_type=F32)


def _bdot_nt(a, b):
    return lax.dot_general(a.astype(BF16), b.astype(BF16), (((1,), (1,)), ((), ())),
                           preferred_element_type=F32)


def _bdot_tn(a, b):
    return lax.dot_general(a.astype(BF16), b.astype(BF16), (((0,), (0,)), ((), ())),
                           preferred_element_type=F32)


def _split_dot(a, b_bf16):
    hi = a.astype(BF16)
    lo = (a - hi.astype(F32)).astype(BF16)
    return (jnp.dot(hi, b_bf16, preferred_element_type=F32)
            + jnp.dot(lo, b_bf16, preferred_element_type=F32))


def _split_dot_left(a_bf16, b):
    hi = b.astype(BF16)
    lo = (b - hi.astype(F32)).astype(BF16)
    return (jnp.dot(a_bf16, hi, preferred_element_type=F32)
            + jnp.dot(a_bf16, lo, preferred_element_type=F32))


def _lane_iota(shape):
    return lax.broadcasted_iota(jnp.int32, shape, len(shape) - 1)


def _row_iota(shape):
    return lax.broadcasted_iota(jnp.int32, shape, 0)


def _rope(xb, cos, sin_up, sin_dn, half):
    return (xb * cos + pltpu.roll(xb, half, 1) * sin_up
            + pltpu.roll(xb, LANES - half, 1) * sin_dn)


def _half_rms(xb, w_row):
    lo = _lane_iota(xb.shape) < 64
    sq = xb * xb
    s_lo = jnp.sum(jnp.where(lo, sq, 0.0), axis=-1, keepdims=True)
    s_hi = jnp.sum(jnp.where(lo, 0.0, sq), axis=-1, keepdims=True)
    ms = jnp.where(lo, s_lo, s_hi) * (1.0 / 64.0)
    return xb * lax.rsqrt(ms + NORM_EPS) * w_row


def _mod_kernel(c_ref, w_ref, b_ref, o_ref):
    c = c_ref[...]
    o_ref[...] = _bdot(_silu(c), w_ref[...]) + b_ref[...]


def _modulation(cvec, w_mod, b_mod):
    tn = 1536
    n_out = 6 * D_MODEL
    return pl.pallas_call(
        _mod_kernel,
        grid=(DEPTH, n_out // tn),
        in_specs=[pl.BlockSpec((SUBLANES, D_MODEL), lambda l, j: (0, 0)),
                  pl.BlockSpec((None, D_MODEL, tn), lambda l, j: (l, 0, j)),
                  pl.BlockSpec((None, 1, tn), lambda l, j: (l, 0, j))],
        out_specs=pl.BlockSpec((None, SUBLANES, tn), lambda l, j: (l, 0, j)),
        out_shape=jax.ShapeDtypeStruct((DEPTH, SUBLANES, n_out), F32),
        compiler_params=_cparams(2),
        name="modulation",
    )(cvec, w_mod, b_mod.reshape(DEPTH, 1, n_out))


def _mod_spec(layer, latent, seq_len, tm, k):
    if latent:
        assert seq_len % tm == 0
    tiles_per_seq = max(seq_len // tm, 1)

    def index_map(i):
        row = 1 + i // tiles_per_seq if latent else 0
        return (layer, row, k, 0, 0)

    return pl.BlockSpec((None, None, None, 1, D_MODEL), index_map)


def _resident(block_shape, index_map):
    return pl.BlockSpec(block_shape, index_map, pipeline_mode=pl.Buffered(1))


def _inproj_kernel(latent, *refs):
    if latent:
        (x_ref, sh_ref, sc_ref, g_ref, w_ref, qn_ref, kn_ref, cn_ref,
         c64_ref, su64_ref, sd64_ref, cm_ref, sum_ref, sdm_ref,
         ssd_ref, ret_ref, aq_ref, ak_ref, av_ref, mq_ref, ckv_ref, kr_ref) = refs
    else:
        (x_ref, sh_ref, sc_ref, g_ref, w_ref, qn_ref, kn_ref, cn_ref,
         ssd_ref, ret_ref, aq_ref, ak_ref, av_ref, mq_ref, ckv_ref, kr_ref) = refs

    half = x_ref.shape[0] // 2
    for r in range(2):
        rows = slice(r * half, (r + 1) * half)
        h = _rms(x_ref[rows, :], g_ref[...]) * (1.0 + sc_ref[...]) + sh_ref[...]
        hb = h.astype(BF16)

        def proj(c0, width):
            return jnp.dot(hb, w_ref[:, c0:c0 + width], preferred_element_type=F32)

        def rope64(xb):
            if not latent:
                return xb
            return _rope(xb, c64_ref[rows, :], su64_ref[rows, :], sd64_ref[rows, :], 16)

        pa = proj(SSD_W + RET_W, 1536)
        for j in range(4):
            q = _half_rms(pa[:, j * LANES:(j + 1) * LANES], qn_ref[...])
            aq_ref[rows, j * LANES:(j + 1) * LANES] = rope64(q)
        ak_ref[rows, :] = rope64(_half_rms(pa[:, 512:640], kn_ref[...]))
        av_ref[rows, :] = pa[:, 640:768]
        mq_ref[rows, :] = pa[:, 768:768 + MLA_Q_RANK]
        ckv_ref[rows, :] = _rms(pa[:, 768 + MLA_Q_RANK:768 + MLA_Q_RANK + MLA_KV_RANK], cn_ref[...])
        kr = pa[:, 1408:1536]
        if latent:
            kr = _rope(kr, cm_ref[rows, :], sum_ref[rows, :], sdm_ref[rows, :], 8)
        kr_ref[rows, :] = kr

        pr = proj(SSD_W, RET_W)
        for j in range(2):
            ret_ref[rows, j * LANES:(j + 1) * LANES] = rope64(pr[:, j * LANES:(j + 1) * LANES])
        for j in range(2, 4):
            ret_ref[rows, j * LANES:(j + 1) * LANES] = rope64(pr[:, j * LANES:(j + 1) * LANES] * 0.125)
        ret_ref[rows, 512:RET_W] = pr[:, 512:RET_W]

        ssd_ref[rows, :] = proj(0, SSD_W)


IN_TOTAL = SSD_W + RET_W + 768 + 768


def _inproj(x2d, mod5, layer, latent, seq_len, g_pre, w_in_p, qn, kn, cn, tabs):
    n = x2d.shape[0]
    tm = TM_ROWS
    tiles_per_seq = max(seq_len // tm, 1)
    row = lambda i: (i, 0)
    const = lambda i: (0, 0)
    in_specs = [pl.BlockSpec((tm, D_MODEL), row),
                _mod_spec(layer, latent, seq_len, tm, 0),
                _mod_spec(layer, latent, seq_len, tm, 1),
                pl.BlockSpec((1, D_MODEL), const),
                _resident((None, D_MODEL, IN_TOTAL), lambda i: (layer, 0, 0)),
                pl.BlockSpec((1, LANES), const),
                pl.BlockSpec((1, LANES), const),
                pl.BlockSpec((1, MLA_KV_RANK), const)]
    args = [x2d, mod5, mod5, g_pre, w_in_p, qn, kn, cn]
    if latent:
        tab = pl.BlockSpec((tm, LANES), lambda i: (i % tiles_per_seq, 0))
        in_specs += [tab] * 6
        args += list(tabs)
    widths = (SSD_W, RET_W, 512, LANES, LANES, MLA_Q_RANK, MLA_KV_RANK, LANES)
    return pl.pallas_call(
        functools.partial(_inproj_kernel, latent),
        grid=(n // tm,),
        in_specs=in_specs,
        out_specs=[pl.BlockSpec((tm, w), row) for w in widths],
        out_shape=[jax.ShapeDtypeStruct((n, w), F32) for w in widths],
        compiler_params=_cparams(1),
        name="inproj_lat" if latent else "inproj_ctx",
    )(*args)


def _ssd_kernel(latent, seq_len, *refs):
    if latent:
        (in_ref, cw_ref, cb_ref, dtb_ref, alog_ref, dsk_ref, nw_ref, ex_ref, s0_ref,
         o_ref, xc_ref, dt_ref, cum_ref, sf_ref, sb_ref, cdb_ref, run_ref) = refs
    else:
        (in_ref, cw_ref, cb_ref, dtb_ref, alog_ref, dsk_ref, nw_ref, ex_ref,
         o_ref, st_ref, xc_ref, dt_ref, cum_ref, sf_ref, sb_ref, cdb_ref, run_ref) = refs
    T = SCAN_CHUNK
    nc = seq_len // T
    ii = _row_iota((T, T))
    jj = _lane_iota((T, T))
    tri_lo = (jj <= ii).astype(BF16)
    tri_up = (jj >= ii).astype(BF16)
    lane_c = _lane_iota((T, LANES))
    row_c = _row_iota((T, SSD_CONV_CH))
    fwd_lane = lane_c < SSD_HEADS
    a_row = -jnp.exp(alog_ref[...])
    expand = ex_ref[...]

    if latent:
        run_ref[...] = s0_ref[...]
    else:
        run_ref[...] = jnp.zeros(run_ref.shape, F32)

    def phase0(c, carry):
        r0 = pl.multiple_of(c * T, T)
        x_cur = in_ref[pl.ds(r0, T), 512:1280]
        prev_blk = in_ref[pl.ds(pl.multiple_of(jnp.maximum(r0 - SUBLANES, 0), SUBLANES), SUBLANES), 512:1280]
        next_blk = in_ref[pl.ds(pl.multiple_of(jnp.minimum(r0 + T, seq_len - SUBLANES), SUBLANES), SUBLANES), 512:1280]
        prev_row = jnp.where(c > 0, prev_blk[SUBLANES - 1:SUBLANES, :], 0.0)
        next_row = jnp.where(c < nc - 1, next_blk[0:1, :], 0.0)
        x_prev = jnp.where(row_c == 0, prev_row, pltpu.roll(x_cur, 1, 0))
        x_next = jnp.where(row_c == T - 1, next_row, pltpu.roll(x_cur, T - 1, 0))
        xc = _silu(cb_ref[...] + x_prev * cw_ref[0:1, :] + x_cur * cw_ref[1:2, :] + x_next * cw_ref[2:3, :])
        xc_ref[pl.ds(r0, T), :] = xc

        dt = _softplus(in_ref[pl.ds(r0, T), 1280:SSD_W] + dtb_ref[...])
        la = dt * a_row
        cum = jnp.where(fwd_lane, _split_dot_left(tri_lo, la), _split_dot_left(tri_up, la))
        tot = jnp.where(fwd_lane[0:1], cum[T - 1:T, :], cum[0:1, :])
        dt_ref[pl.ds(r0, T), :] = dt
        cum_ref[pl.ds(r0, T), :] = cum
        w = dt * jnp.exp(tot - cum)
        wcd = jnp.concatenate([w, jnp.broadcast_to(jnp.exp(tot), (SUBLANES, LANES))], axis=0)
        wcd_x = _split_dot(wcd, expand)
        w_x = wcd_x[0:T]
        cd_x = wcd_x[T:T + 1]
        cdb_ref[c] = jnp.broadcast_to(cd_x[:, 512:1024], (SUBLANES, 512))
        xs = xc[:, 0:512]
        bm = xc[:, 512:640].astype(BF16)
        for p in range(4):
            g = p // 2
            sl = slice(p * LANES, (p + 1) * LANES)
            vw = jnp.concatenate([xs[:, sl] * w_x[:, sl], xs[:, sl] * w_x[:, 512 + p * LANES:512 + (p + 1) * LANES]],
                                 axis=1)
            cs = _bdot_tn(bm, vw)[g * 64:(g + 1) * 64]
            s_run = run_ref[p]
            sf_ref[c, p] = s_run[:, 0:LANES]
            sb_ref[c, p] = cs[:, LANES:2 * LANES]
            new_f = s_run[:, 0:LANES] * cd_x[:, sl] + cs[:, 0:LANES]
            run_ref[p] = jnp.concatenate([new_f, s_run[:, LANES:2 * LANES]], axis=1)
        return carry

    lax.fori_loop(0, nc, phase0, 0, unroll=min(nc, SSD_UNROLL))

    def bwd_states(k, carry):
        c = nc - 1 - k
        cd = cdb_ref[c]
        for p in range(4):
            s_run = run_ref[p]
            cs_b = sb_ref[c, p]
            sb_ref[c, p] = s_run[:, LANES:2 * LANES]
            new_b = s_run[:, LANES:2 * LANES] * cd[0:1, p * LANES:(p + 1) * LANES] + cs_b
            run_ref[p] = jnp.concatenate([s_run[:, 0:LANES], new_b], axis=1)
        return carry

    lax.fori_loop(0, nc, bwd_states, 0)
    if not latent:
        for p in range(4):
            s_fin = run_ref[p]
            for d in range(2):
                for hh in range(2):
                    c0 = d * LANES + hh * 64
                    st_ref[d, 2 * p + hh] = s_fin[:, c0:c0 + 64]

    lower = jj <= ii
    upper = jj >= ii
    zeros64 = jnp.zeros((64, 2 * LANES), F32)

    def phase1(c, carry):
        r0 = pl.multiple_of(c * T, T)
        xc = xc_ref[pl.ds(r0, T), :]
        xs = xc[:, 0:512]
        bm = xc[:, 512:640].astype(BF16)
        cm = xc[:, 640:768]
        dt = dt_ref[pl.ds(r0, T), :]
        cum = cum_ref[pl.ds(r0, T), :]
        cum_t = cum.T
        dt_t = dt.T
        e_x = _split_dot(jnp.exp(cum), expand)
        gmat = [_bdot_nt(jnp.where((lane_c < 64) == (g == 0), cm, 0.0), bm) for g in range(2)]
        cmb = cm.astype(BF16)
        outs = []
        for p in range(4):
            g = p // 2
            sl = slice(p * LANES, (p + 1) * LANES)
            xs_p = xs[:, sl].astype(BF16)
            halves = []
            for h in (2 * p, 2 * p + 1):
                df = jnp.exp(jnp.where(lower, cum[:, h:h + 1] - cum_t[h:h + 1, :], NEG_BIG)) * dt_t[h:h + 1, :]
                hb = SSD_HEADS + h
                db = jnp.exp(jnp.where(upper, cum[:, hb:hb + 1] - cum_t[hb:hb + 1, :], NEG_BIG)) * dt_t[hb:hb + 1, :]
                m_h = (gmat[g] * (df + db)).astype(BF16)
                halves.append(jnp.dot(m_h, xs_p, preferred_element_type=F32))
            o_p = jnp.where(lane_c < 64, halves[0], halves[1])
            s_in = jnp.concatenate([sf_ref[c, p], sb_ref[c, p]], axis=1)
            s_pad = jnp.concatenate([s_in, zeros64] if g == 0 else [zeros64, s_in], axis=0)
            oi = jnp.dot(cmb, s_pad.astype(BF16), preferred_element_type=F32)
            o_p = o_p + oi[:, 0:LANES] * e_x[:, sl] + oi[:, LANES:2 * LANES] * e_x[:, 512 + p * LANES:512 + (p + 1) * LANES]
            outs.append(o_p)
        y = jnp.concatenate(outs, axis=1) + dsk_ref[...] * xs
        z = in_ref[pl.ds(r0, T), 0:512]
        o_ref[pl.ds(r0, T), :] = _rms(y * _silu(z), nw_ref[...])
        return carry

    lax.fori_loop(0, nc, phase1, 0)


def _ssd_mixer(ssd_in, bsz, seq_len, latent, cw, cb, dtb, alog, dsk, nw, expand, s0):
    nc = seq_len // SCAN_CHUNK
    const2 = lambda b: (0, 0)
    per_b3 = lambda b: (b, 0, 0)
    per_b4 = lambda b: (b, 0, 0, 0)
    in_specs = [pl.BlockSpec((None, seq_len, SSD_W), per_b3),
                pl.BlockSpec((3, SSD_CONV_CH), const2),
                pl.BlockSpec((1, SSD_CONV_CH), const2),
                pl.BlockSpec((1, LANES), const2),
                pl.BlockSpec((1, LANES), const2),
                pl.BlockSpec((1, SSD_INNER), const2),
                pl.BlockSpec((1, SSD_INNER), const2),
                pl.BlockSpec((LANES, 1024), const2)]
    args = [ssd_in.reshape(bsz, seq_len, SSD_W), cw, cb, dtb, alog, dsk, nw, expand]
    out_specs = [pl.BlockSpec((None, seq_len, SSD_INNER), per_b3)]
    out_shape = [jax.ShapeDtypeStruct((bsz, seq_len, SSD_INNER), F32)]
    if latent:
        in_specs.append(pl.BlockSpec((None, 4, 64, 2 * LANES), per_b4))
        args.append(s0)
    else:
        out_specs.append(pl.BlockSpec((None, 2, SSD_HEADS, SSD_STATE, SSD_HEAD_DIM), lambda b: (b, 0, 0, 0, 0)))
        out_shape.append(jax.ShapeDtypeStruct((bsz, 2, SSD_HEADS, SSD_STATE, SSD_HEAD_DIM), F32))
    scratch = [pltpu.VMEM((seq_len, SSD_CONV_CH), F32),
               pltpu.VMEM((seq_len, LANES), F32),
               pltpu.VMEM((seq_len, LANES), F32),
               pltpu.VMEM((nc, 4, 64, LANES), F32),
               pltpu.VMEM((nc, 4, 64, LANES), F32),
               pltpu.VMEM((nc, SUBLANES, 512), F32),
               pltpu.VMEM((4, 64, 2 * LANES), F32)]
    res = pl.pallas_call(
        functools.partial(_ssd_kernel, latent, seq_len),
        grid=(bsz,),
        in_specs=in_specs, out_specs=out_specs, out_shape=out_shape,
        scratch_shapes=scratch,
        compiler_params=_cparams(1),
        name="ssd_lat" if latent else "ssd_ctx",
    )(*args)
    return res


def _ret_kernel(latent, seq_len, *refs):
    if latent:
        (in_ref, lg_ref, gn_ref, s0_ref,
         o_ref, dc_ref, we_ref, cd_ref, sf_ref, sb_ref, run_ref) = refs
    else:
        (in_ref, lg_ref, gn_ref,
         o_ref, st_ref, dc_ref, we_ref, cd_ref, sf_ref, sb_ref, run_ref) = refs
    T = SCAN_CHUNK
    nc = seq_len // T

    @pl.when(pl.program_id(0) == 0)
    def _tables():
        la = -_softplus(-lg_ref[...])
        ii = _row_iota((T, T)).astype(F32)
        jj = _lane_iota((T, T)).astype(F32)
        for h in range(RET_HEADS):
            la_f = la[h:h + 1, :]
            la_b = la[RET_HEADS + h:RET_HEADS + h + 1, :]
            dc_ref[h] = (jnp.exp(jnp.where(jj <= ii, (ii - jj) * la_f, NEG_BIG))
                         + jnp.exp(jnp.where(jj >= ii, (jj - ii) * la_b, NEG_BIG)))
            we_ref[h, 0] = jnp.exp((T - 1.0 - ii) * la_f)
            we_ref[h, 1] = jnp.exp(ii * la_b)
            we_ref[h, 2] = jnp.exp((ii + 1.0) * la_f)
            we_ref[h, 3] = jnp.exp((T - ii) * la_b)
            cd_ref[h] = jnp.concatenate([jnp.broadcast_to(jnp.exp(T * la_f), (SUBLANES, LANES)),
                                         jnp.broadcast_to(jnp.exp(T * la_b), (SUBLANES, LANES))], axis=1)

    if latent:
        run_ref[...] = s0_ref[...]
    else:
        run_ref[...] = jnp.zeros(run_ref.shape, F32)

    def phase0(c, carry):
        r0 = pl.multiple_of(c * T, T)
        for h in range(RET_HEADS):
            blk = h // 2
            k_blk = in_ref[pl.ds(r0, T), 256 + blk * LANES:256 + (blk + 1) * LANES].astype(BF16)
            v_h = in_ref[pl.ds(r0, T), 512 + h * LANES:512 + (h + 1) * LANES]
            vw = jnp.concatenate([v_h * we_ref[h, 0], v_h * we_ref[h, 1]], axis=1)
            cs = _bdot_tn(k_blk, vw)[(h % 2) * 64:(h % 2 + 1) * 64]
            s_run = run_ref[h]
            sf_ref[c, h] = s_run[:, 0:LANES]
            sb_ref[c, h] = cs[:, LANES:2 * LANES]
            new_f = s_run[:, 0:LANES] * cd_ref[h, 0:1, 0:LANES] + cs[:, 0:LANES]
            run_ref[h] = jnp.concatenate([new_f, s_run[:, LANES:2 * LANES]], axis=1)
        return carry

    unroll = min(nc, RET_UNROLL)
    lax.fori_loop(0, nc, phase0, 0, unroll=unroll)

    def bwd_states(k, carry):
        c = nc - 1 - k
        for h in range(RET_HEADS):
            s_run = run_ref[h]
            cs_b = sb_ref[c, h]
            sb_ref[c, h] = s_run[:, LANES:2 * LANES]
            new_b = s_run[:, LANES:2 * LANES] * cd_ref[h, 0:1, LANES:2 * LANES] + cs_b
            run_ref[h] = jnp.concatenate([s_run[:, 0:LANES], new_b], axis=1)
        return carry

    lax.fori_loop(0, nc, bwd_states, 0)
    if not latent:
        for h in range(RET_HEADS):
            for d in range(2):
                st_ref[d, h] = run_ref[h, :, d * LANES:(d + 1) * LANES]

    lane_c = _lane_iota((T, LANES))
    zeros64 = jnp.zeros((64, 2 * LANES), F32)

    def phase1(c, carry):
        r0 = pl.multiple_of(c * T, T)
        for h in range(RET_HEADS):
            blk = h // 2
            q_blk = in_ref[pl.ds(r0, T), blk * LANES:(blk + 1) * LANES]
            k_blk = in_ref[pl.ds(r0, T), 256 + blk * LANES:256 + (blk + 1) * LANES].astype(BF16)
            v_h = in_ref[pl.ds(r0, T), 512 + h * LANES:512 + (h + 1) * LANES].astype(BF16)
            g_h = in_ref[pl.ds(r0, T), 1024 + h * LANES:1024 + (h + 1) * LANES]
            q_m = jnp.where((lane_c < 64) == (h % 2 == 0), q_blk, 0.0)
            m_h = (_bdot_nt(q_m, k_blk) * dc_ref[h]).astype(BF16)
            o = jnp.dot(m_h, v_h, preferred_element_type=F32)
            s_in = jnp.concatenate([sf_ref[c, h], sb_ref[c, h]], axis=1)
            s_pad = jnp.concatenate([s_in, zeros64] if h % 2 == 0 else [zeros64, s_in], axis=0)
            oi = _bdot(q_blk, s_pad)
            o = o + oi[:, 0:LANES] * we_ref[h, 2] + oi[:, LANES:2 * LANES] * we_ref[h, 3]
            oc = o - jnp.mean(o, axis=-1, keepdims=True)
            on = oc * lax.rsqrt(jnp.mean(oc * oc, axis=-1, keepdims=True) + NORM_EPS)
            o_ref[pl.ds(r0, T), h * LANES:(h + 1) * LANES] = (
                on * gn_ref[:, h * LANES:(h + 1) * LANES] * _silu(g_h))
        return carry

    lax.fori_loop(0, nc, phase1, 0, unroll=unroll)


def _ret_mixer(ret_in, bsz, seq_len, latent, logit_rows, gn_w, s0):
    nc = seq_len // SCAN_CHUNK
    T = SCAN_CHUNK
    const2 = lambda b: (0, 0)
    per_b3 = lambda b: (b, 0, 0)
    per_b4 = lambda b: (b, 0, 0, 0)
    in_specs = [pl.BlockSpec((None, seq_len, RET_W), per_b3),
                pl.BlockSpec((SUBLANES, LANES), const2),
                pl.BlockSpec((1, 512), const2)]
    args = [ret_in.reshape(bsz, seq_len, RET_W), logit_rows, gn_w]
    out_specs = [pl.BlockSpec((None, seq_len, 512), per_b3)]
    out_shape = [jax.ShapeDtypeStruct((bsz, seq_len, 512), F32)]
    if latent:
        in_specs.append(pl.BlockSpec((None, RET_HEADS, 64, 2 * LANES), per_b4))
        args.append(s0)
    else:
        out_specs.append(pl.BlockSpec((None, 2, RET_HEADS, RET_QK, RET_V), lambda b: (b, 0, 0, 0, 0)))
        out_shape.append(jax.ShapeDtypeStruct((bsz, 2, RET_HEADS, RET_QK, RET_V), F32))
    scratch = [pltpu.VMEM((RET_HEADS, T, T), F32),
               pltpu.VMEM((RET_HEADS, 4, T, LANES), F32),
               pltpu.VMEM((RET_HEADS, SUBLANES, 2 * LANES), F32),
               pltpu.VMEM((nc, RET_HEADS, 64, LANES), F32),
               pltpu.VMEM((nc, RET_HEADS, 64, LANES), F32),
               pltpu.VMEM((RET_HEADS, 64, 2 * LANES), F32)]
    return pl.pallas_call(
        functools.partial(_ret_kernel, latent, seq_len),
        grid=(bsz,),
        in_specs=in_specs, out_specs=out_specs, out_shape=out_shape,
        scratch_shapes=scratch,
        compiler_params=_cparams(1),
        name="ret_lat" if latent else "ret_ctx",
    )(*args)


LOG2_E = 1.4426950408889634


def _softmax_pv(s, v, scale=1.0):
    m = jnp.max(s, axis=-1, keepdims=True)
    p = jnp.exp2((s - m) * (scale * LOG2_E))
    l = jnp.sum(p, axis=-1, keepdims=True)
    return jnp.dot(p.astype(BF16), v, preferred_element_type=F32) / l


def _att_kernel(latent, tq, *refs):
    if latent:
        q_ref, kn_ref, vn_ref, kc_ref, vc_ref, o_ref, kb_ref, vb_ref = refs
    else:
        q_ref, kn_ref, vn_ref, o_ref, kb_ref, vb_ref = refs

    @pl.when(pl.program_id(1) == 0)
    def _fill():
        if latent:
            kb_ref[0:PAST_LEN, :] = kc_ref[...].astype(BF16)
            vb_ref[0:PAST_LEN, :] = vc_ref[...].astype(BF16)
            kb_ref[PAST_LEN:, :] = kn_ref[...].astype(BF16)
            vb_ref[PAST_LEN:, :] = vn_ref[...].astype(BF16)
        else:
            kb_ref[...] = kn_ref[...].astype(BF16)
            vb_ref[...] = vn_ref[...].astype(BF16)

    sub = min(tq, ATT_SUB_ROWS)
    lane_q = _lane_iota((sub, LANES))
    for i in range(tq // sub):
        rows = slice(i * sub, (i + 1) * sub)
        acc = []
        for g in range(ATT_KV_HEADS):
            qs = jnp.concatenate(
                [jnp.where((lane_q < 64) == (g == 0), q_ref[rows, j * LANES:(j + 1) * LANES] * 0.125, 0.0).astype(BF16)
                 for j in range(4)], axis=0)
            s = lax.dot_general(qs, kb_ref[...], (((1,), (1,)), ((), ())), preferred_element_type=F32)
            acc.append(_softmax_pv(s, vb_ref[...]))
        for j in range(4):
            o_ref[rows, j * LANES:(j + 1) * LANES] = jnp.where(lane_q < 64, acc[0][j * sub:(j + 1) * sub],
                                                               acc[1][j * sub:(j + 1) * sub])


def _att_mixer(aq, ak, av, bsz, seq_len, latent, k_ctx, v_ctx):
    tq = 512 if latent else 256
    lk = seq_len + (PAST_LEN if latent else 0)
    qmap = lambda b, i: (b, i, 0)
    bmap = lambda b, i: (b, 0, 0)
    in_specs = [pl.BlockSpec((None, tq, 512), qmap),
                pl.BlockSpec((None, seq_len, LANES), bmap),
                pl.BlockSpec((None, seq_len, LANES), bmap)]
    args = [aq.reshape(bsz, seq_len, 512), ak.reshape(bsz, seq_len, LANES), av.reshape(bsz, seq_len, LANES)]
    if latent:
        in_specs += [pl.BlockSpec((None, PAST_LEN, LANES), bmap)] * 2
        args += [k_ctx, v_ctx]
    return pl.pallas_call(
        functools.partial(_att_kernel, latent, tq),
        grid=(bsz, seq_len // tq),
        in_specs=in_specs,
        out_specs=pl.BlockSpec((None, tq, 512), qmap),
        out_shape=jax.ShapeDtypeStruct((bsz, seq_len, 512), F32),
        scratch_shapes=[pltpu.VMEM((lk, LANES), BF16), pltpu.VMEM((lk, LANES), BF16)],
        compiler_params=_cparams(2),
        name="att_lat" if latent else "att_ctx",
    )(*args)


def _mla_kernel(latent, tq, *refs):
    if latent:
        (mq_ref, cn_ref, rn_ref, cc_ref, rc_ref, qn_ref, wq_ref, wk_ref, wv_ref,
         cos_ref, su_ref, sd_ref, o_ref, kb_ref, vb_ref) = refs
    else:
        (mq_ref, cn_ref, rn_ref, qn_ref, wq_ref, wk_ref, wv_ref, o_ref, kb_ref, vb_ref) = refs

    def fill(r0, ckv, kr):
        cb = ckv.astype(BF16)
        n = ckv.shape[0]
        kn = jnp.dot(cb, wk_ref[...], preferred_element_type=F32)
        for h in range(MLA_HEADS):
            kb_ref[r0:r0 + n, h * LANES:(h + 1) * LANES] = (kn[:, h * LANES:(h + 1) * LANES] + kr).astype(BF16)
        vb_ref[r0:r0 + n, :] = jnp.dot(cb, wv_ref[...], preferred_element_type=F32).astype(BF16)

    @pl.when(pl.program_id(1) == 0)
    def _fill():
        if latent:
            fill(0, cc_ref[...], rc_ref[...])
            fill(PAST_LEN, cn_ref[...], rn_ref[...])
        else:
            fill(0, cn_ref[...], rn_ref[...])

    q = _bdot(_rms(mq_ref[...], qn_ref[...]), wq_ref[...])
    lane_q = _lane_iota((tq, LANES))
    scale = (MLA_NOPE + MLA_ROPE) ** -0.5
    for p in range(4):
        halves = []
        for h in (2 * p, 2 * p + 1):
            qh = q[:, h * LANES:(h + 1) * LANES]
            if latent:
                qh = _rope(qh, cos_ref[...], su_ref[...], sd_ref[...], 8)
            s = _bdot_nt(qh, kb_ref[:, h * LANES:(h + 1) * LANES])
            halves.append(_softmax_pv(s, vb_ref[:, p * LANES:(p + 1) * LANES], scale))
        o_ref[:, p * LANES:(p + 1) * LANES] = jnp.where(lane_q < 64, halves[0], halves[1])


def _mla_mixer(mq, ckv, kr, bsz, seq_len, latent, ckv_ctx, kr_ctx, qn, wq, wk, wv, layer, tabs):
    tq = 512 if latent else 256
    lk = seq_len + (PAST_LEN if latent else 0)
    qmap = lambda b, i: (b, i, 0)
    bmap = lambda b, i: (b, 0, 0)
    const = lambda b, i: (0, 0)
    lyr3 = lambda b, i: (layer, 0, 0)
    in_specs = [pl.BlockSpec((None, tq, MLA_Q_RANK), qmap),
                pl.BlockSpec((None, seq_len, MLA_KV_RANK), bmap),
                pl.BlockSpec((None, seq_len, LANES), bmap)]
    args = [mq.reshape(bsz, seq_len, MLA_Q_RANK), ckv.reshape(bsz, seq_len, MLA_KV_RANK),
            kr.reshape(bsz, seq_len, LANES)]
    if latent:
        in_specs += [pl.BlockSpec((None, PAST_LEN, MLA_KV_RANK), bmap),
                     pl.BlockSpec((None, PAST_LEN, LANES), bmap)]
        args += [ckv_ctx, kr_ctx]
    in_specs += [pl.BlockSpec((1, MLA_Q_RANK), const),
                 _resident((None, MLA_Q_RANK, 1024), lyr3),
                 _resident((None, MLA_KV_RANK, 1024), lyr3),
                 _resident((None, MLA_KV_RANK, 512), lyr3)]
    args += [qn, wq, wk, wv]
    if latent:
        in_specs += [pl.BlockSpec((tq, LANES), lambda b, i: (i, 0))] * 3
        args += list(tabs)
    return pl.pallas_call(
        functools.partial(_mla_kernel, latent, tq),
        grid=(bsz, seq_len // tq),
        in_specs=in_specs,
        out_specs=pl.BlockSpec((None, tq, 512), qmap),
        out_shape=jax.ShapeDtypeStruct((bsz, seq_len, 512), F32),
        scratch_shapes=[pltpu.VMEM((lk, 1024), BF16), pltpu.VMEM((lk, 512), BF16)],
        compiler_params=_cparams(2),
        name="mla_lat" if latent else "mla_ctx",
    )(*args)


def _merge_kernel(x_ref, sh_ref, sc_ref, ga_ref, gpre_ref, gpost_ref, o1_ref, o2_ref, o3_ref, o4_ref,
                  wm_ref, bm_ref, wb_ref, wo_ref, y_ref):
    x = x_ref[...]
    hb = (_rms(x, gpre_ref[...]) * (1.0 + sc_ref[...]) + sh_ref[...]).astype(BF16)
    merged = None
    for k, o_ref in enumerate((o1_ref, o2_ref, o3_ref, o4_ref)):
        cols = slice(k * D_MODEL, (k + 1) * D_MODEL)
        gate = jax.nn.sigmoid(jnp.dot(hb, wm_ref[:, cols], preferred_element_type=F32) + bm_ref[:, cols])
        term = gate * _bdot(o_ref[...], wb_ref[k])
        merged = term if merged is None else merged + term
    m = _bdot(merged, wo_ref[...])
    y_ref[...] = x + ga_ref[...] * _rms(m, gpost_ref[...])


def _merge(x2d, mod5, layer, latent, seq_len, g_pre, g_post, branches, w_merge, b_merge, w_br, w_out):
    n = x2d.shape[0]
    tm = TM_ROWS
    row = lambda i: (i, 0)
    const = lambda i: (0, 0)
    in_specs = ([pl.BlockSpec((tm, D_MODEL), row)]
                + [_mod_spec(layer, latent, seq_len, tm, k) for k in (0, 1, 2)]
                + [pl.BlockSpec((1, D_MODEL), const)] * 2
                + [pl.BlockSpec((tm, 512), row)] * 4
                + [_resident((None, D_MODEL, 4 * D_MODEL), lambda i: (layer, 0, 0)),
                   pl.BlockSpec((1, 4 * D_MODEL), const),
                   _resident((None, 4, 512, D_MODEL), lambda i: (layer, 0, 0, 0)),
                   _resident((None, D_MODEL, D_MODEL), lambda i: (layer, 0, 0))])
    return pl.pallas_call(
        _merge_kernel,
        grid=(n // tm,),
        in_specs=in_specs,
        out_specs=pl.BlockSpec((tm, D_MODEL), row),
        out_shape=jax.ShapeDtypeStruct((n, D_MODEL), F32),
        compiler_params=_cparams(1),
        name="merge_lat" if latent else "merge_ctx",
    )(x2d, mod5, mod5, mod5, g_pre, g_post, *[b.reshape(n, 512) for b in branches],
      w_merge, b_merge, w_br, w_out)


def _ffn_kernel(tm, seq_len, x_ref, xp_ref, xn_ref, sh_ref, sc_ref, gf_ref, gpre_ref, gpost_ref,
                wu_ref, cw_ref, cb_ref, wd_ref, y_ref):
    tiles_per_seq = seq_len // tm
    i = pl.program_id(0)
    first = (i % tiles_per_seq) == 0
    last = (i % tiles_per_seq) == tiles_per_seq - 1

    def nm(x):
        return _rms(x, gpre_ref[...]) * (1.0 + sc_ref[...]) + sh_ref[...]

    x = x_ref[...]
    h_prev = jnp.where(first, 0.0, nm(xp_ref[...]))
    h_next = jnp.where(last, 0.0, nm(xn_ref[...]))
    hb = jnp.concatenate([h_prev, nm(x), h_next], axis=0).astype(BF16)
    rows = tm + 2 * SUBLANES

    def conv_cols(c0):
        cols = slice(c0, c0 + FF_HALF)
        p = jnp.dot(hb, wu_ref[:, cols], preferred_element_type=F32)
        w = cw_ref[:, cols]
        u = (pltpu.roll(p, 1, 0) * w[0:1] + p * w[1:2] + pltpu.roll(p, rows - 1, 0) * w[2:3])
        return u[SUBLANES:SUBLANES + tm] + cb_ref[:, cols]

    acc = None
    for j in range(D_FF // FF_HALF):
        up = conv_cols(j * FF_HALF)
        gate = conv_cols(D_FF + j * FF_HALF)
        part = _bdot(_silu(gate) * up, wd_ref[j * FF_HALF:(j + 1) * FF_HALF, :])
        acc = part if acc is None else acc + part
    y_ref[...] = x + gf_ref[...] * _rms(acc, gpost_ref[...])


def _ffn(x2d, mod5, layer, latent, seq_len, g_pre, g_post, w_up, conv_w, conv_b, w_down):
    n = x2d.shape[0]
    tm = min(TM_ROWS, seq_len)
    assert seq_len % tm == 0
    hb = tm // SUBLANES
    n_hb = n // SUBLANES
    row = lambda i: (i, 0)
    const = lambda i: (0, 0)
    lyr3 = lambda i: (layer, 0, 0)
    in_specs = ([pl.BlockSpec((tm, D_MODEL), row),
                 pl.BlockSpec((SUBLANES, D_MODEL), lambda i: (jnp.maximum(i * hb - 1, 0), 0)),
                 pl.BlockSpec((SUBLANES, D_MODEL), lambda i: (jnp.minimum((i + 1) * hb, n_hb - 1), 0))]
                + [_mod_spec(layer, latent, seq_len, tm, k) for k in (3, 4, 5)]
                + [pl.BlockSpec((1, D_MODEL), const)] * 2
                + [_resident((None, D_MODEL, 2 * D_FF), lyr3),
                   _resident((None, 3, 2 * D_FF), lyr3),
                   _resident((None, 1, 2 * D_FF), lyr3),
                   _resident((None, D_FF, D_MODEL), lyr3)])
    return pl.pallas_call(
        functools.partial(_ffn_kernel, tm, seq_len),
        grid=(n // tm,),
        in_specs=in_specs,
        out_specs=pl.BlockSpec((tm, D_MODEL), row),
        out_shape=jax.ShapeDtypeStruct((n, D_MODEL), F32),
        compiler_params=_cparams(1),
        name="ffn_lat" if latent else "ffn_ctx",
    )(x2d, x2d, x2d, mod5, mod5, mod5, g_pre, g_post, w_up, conv_w, conv_b, w_down)


def _rope_tables(seq_len, head_dim, lane_of_dim):
    d_axis = head_dim // 2
    t = np.arange(seq_len)
    pos = np.stack([(t // GRID_W).astype(np.float32), (t % GRID_W).astype(np.float32)], axis=0)
    inv_freq = (np.float32(ROPE_BASE) ** (-np.arange(0, d_axis, 2, dtype=np.float32) / np.float32(d_axis))).astype(np.float32)
    cos = np.ones((seq_len, LANES), np.float32)
    s_up = np.zeros((seq_len, LANES), np.float32)
    s_dn = np.zeros((seq_len, LANES), np.float32)
    for lane, d in enumerate(lane_of_dim):
        if d < 0:
            continue
        axis, j = divmod(d, d_axis)
        second = j >= d_axis // 2
        ang = (pos[axis] * inv_freq[j % (d_axis // 2)]).astype(np.float32)
        cos[:, lane] = np.cos(ang)
        if second:
            s_up[:, lane] = np.sin(ang)
        else:
            s_dn[:, lane] = -np.sin(ang)
    return jnp.asarray(cos), jnp.asarray(s_up), jnp.asarray(s_dn)


def _ssd_expand_matrix():
    e = np.zeros((LANES, 1024), np.float32)
    for d in range(2):
        for h in range(SSD_HEADS):
            e[d * SSD_HEADS + h, d * 512 + h * 64:d * 512 + (h + 1) * 64] = 1.0
    return jnp.asarray(e, dtype=BF16)


def _matmul_weights(w_in, mla_w_uq, mla_w_ukv, w_br_ssd, w_br_ret, w_br_att, w_br_mla, w_merge, w_out,
                    w_ffn_up, ffn_conv_w, ffn_conv_b, w_ffn_down):
    off = np.cumsum([0, 512, 768, 16, 256, 256, 512, 512, 512, 128, 128, 384, 288])
    col = lambda k: w_in[:, :, off[k]:off[k + 1]]
    zeros = lambda n: jnp.zeros((DEPTH, D_MODEL, n), F32)
    aq = col(7)
    aq_pairs = [jnp.concatenate([aq[:, :, j * 64:(j + 1) * 64], aq[:, :, (j + 4) * 64:(j + 5) * 64]], axis=2)
                for j in range(4)]
    mckv = col(11)
    w_in_p = jnp.concatenate(
        [col(0), col(1), col(2), zeros(112),
         col(3), col(4), col(5), col(6),
         *aq_pairs, col(8), col(9),
         col(10), mckv[:, :, :MLA_KV_RANK], zeros(64), mckv[:, :, MLA_KV_RANK:], zeros(32)],
        axis=2).astype(BF16)

    uq = mla_w_uq.reshape(DEPTH, MLA_Q_RANK, MLA_HEADS, MLA_NOPE + MLA_ROPE)
    wq = jnp.pad(uq, ((0, 0), (0, 0), (0, 0), (0, 32))).reshape(DEPTH, MLA_Q_RANK, 1024).astype(BF16)
    ukv = mla_w_ukv.reshape(DEPTH, MLA_KV_RANK, MLA_HEADS, MLA_NOPE + MLA_V)
    wk = jnp.pad(ukv[..., :MLA_NOPE], ((0, 0), (0, 0), (0, 0), (0, 64))).reshape(DEPTH, MLA_KV_RANK, 1024).astype(BF16)
    wv = ukv[..., MLA_NOPE:].reshape(DEPTH, MLA_KV_RANK, 512).astype(BF16)

    w_att = (w_br_att.reshape(DEPTH, ATT_KV_HEADS, 4, ATT_HEAD_DIM, D_MODEL).swapaxes(1, 2)
             .reshape(DEPTH, 512, D_MODEL))
    w_br = jnp.stack([w_br_ssd, w_br_ret, w_att, w_br_mla], axis=1).astype(BF16)

    return dict(
        w_in=w_in_p, wq=wq, wk=wk, wv=wv, w_br=w_br,
        w_merge=w_merge.astype(BF16), w_out=w_out.astype(BF16),
        w_up=w_ffn_up.astype(BF16), ffn_cw=ffn_conv_w,
        ffn_cb=ffn_conv_b.reshape(DEPTH, 1, 2 * D_FF),
        w_down=w_ffn_down.astype(BF16),
    )


def _layer_params(i, ssd_conv_w, ssd_conv_b, ssd_dt_bias, ssd_a_log, ssd_d, ssd_norm_w,
                  ret_decay_logit, ret_gn_w, att_q_norm, att_k_norm, mla_q_norm, mla_kv_norm,
                  b_merge, g_pre_mix, g_post_mix, g_pre_ffn, g_post_ffn):
    pad_row = lambda v: jnp.pad(v.reshape(1, -1), ((0, 0), (0, LANES - v.size)))
    return dict(
        qn=jnp.tile(att_q_norm[i], 2).reshape(1, LANES),
        kn=jnp.tile(att_k_norm[i], 2).reshape(1, LANES),
        cn=mla_kv_norm[i].reshape(1, MLA_KV_RANK),
        ssd_cw=ssd_conv_w[i], ssd_cb=ssd_conv_b[i].reshape(1, SSD_CONV_CH),
        ssd_dtb=pad_row(ssd_dt_bias[i]), ssd_alog=pad_row(ssd_a_log[i]),
        ssd_dsk=jnp.repeat(ssd_d[i], SSD_HEAD_DIM).reshape(1, SSD_INNER),
        ssd_nw=ssd_norm_w[i].reshape(1, SSD_INNER),
        ret_logit=jnp.broadcast_to(ret_decay_logit[i].reshape(SUBLANES, 1), (SUBLANES, LANES)),
        ret_gn=ret_gn_w[i].reshape(1, 512),
        mla_qn=mla_q_norm[i].reshape(1, MLA_Q_RANK),
        b_merge=b_merge[i].reshape(1, 4 * D_MODEL),
        g_pre_mix=g_pre_mix[i].reshape(1, D_MODEL), g_post_mix=g_post_mix[i].reshape(1, D_MODEL),
        g_pre_ffn=g_pre_ffn[i].reshape(1, D_MODEL), g_post_ffn=g_post_ffn[i].reshape(1, D_MODEL),
    )


def _ssd_state_to_pairs(s):
    b = s.shape[0]
    s = s.reshape(b, 2, 4, 2, 64, 64).transpose(0, 2, 4, 1, 3, 5)
    return s.reshape(b, 4, 64, 256)


def _ret_state_to_lanes(s):
    b = s.shape[0]
    return s.transpose(0, 2, 3, 1, 4).reshape(b, RET_HEADS, 64, 256)


def _trunk_pass(x2d, mod5, layer, latent, bsz, seq_len, lp, mw, tabs64, tabs_mla, expand, ctx):
    (ssd_in, ret_in, aq, ak, av, mq, ckv, kr) = _inproj(
        x2d, mod5, layer, latent, seq_len, lp['g_pre_mix'], mw['w_in'], lp['qn'], lp['kn'], lp['cn'],
        (tabs64 + tabs_mla) if latent else None)
    ssd_res = _ssd_mixer(ssd_in, bsz, seq_len, latent, lp['ssd_cw'], lp['ssd_cb'], lp['ssd_dtb'],
                         lp['ssd_alog'], lp['ssd_dsk'], lp['ssd_nw'], expand,
                         ctx['ssd'] if latent else None)
    ret_res = _ret_mixer(ret_in, bsz, seq_len, latent, lp['ret_logit'], lp['ret_gn'],
                         ctx['ret'] if latent else None)
    o_att = _att_mixer(aq, ak, av, bsz, seq_len, latent,
                       ctx['att_k'] if latent else None, ctx['att_v'] if latent else None)
    o_mla = _mla_mixer(mq, ckv, kr, bsz, seq_len, latent,
                       ctx['mla_ckv'] if latent else None, ctx['mla_kr'] if latent else None,
                       lp['mla_qn'], mw['wq'], mw['wk'], mw['wv'], layer, tabs_mla)
    x2d = _merge(x2d, mod5, layer, latent, seq_len, lp['g_pre_mix'], lp['g_post_mix'],
                 (ssd_res[0], ret_res[0], o_att, o_mla), mw['w_merge'], lp['b_merge'], mw['w_br'], mw['w_out'])
    x2d = _ffn(x2d, mod5, layer, latent, seq_len, lp['g_pre_ffn'], lp['g_post_ffn'],
               mw['w_up'], mw['ffn_cw'], mw['ffn_cb'], mw['w_down'])
    cache = None
    if not latent:
        cache = dict(ssd=ssd_res[1], ret=ret_res[1],
                     att_k=ak.reshape(bsz, seq_len, ATT_KV_HEADS, ATT_HEAD_DIM),
                     att_v=av.reshape(bsz, seq_len, ATT_KV_HEADS, ATT_HEAD_DIM),
                     mla_ckv=ckv.reshape(bsz, seq_len, MLA_KV_RANK),
                     mla_krope=kr.reshape(bsz, seq_len, LANES)[:, :, 64:64 + MLA_ROPE])
    return x2d, cache


def kernel(x_prompt, x_sample, state_ssd, state_ret, cache_att_k, cache_att_v, cache_mla_ckv, cache_mla_krope, c, c_ctx, w_mod, b_mod, g_pre_mix, g_post_mix, g_pre_ffn, g_post_ffn, w_in, ssd_conv_w, ssd_conv_b, ssd_dt_bias, ssd_a_log, ssd_d, ssd_norm_w, ret_decay_logit, ret_gn_w, att_q_norm, att_k_norm, mla_q_norm, mla_w_uq, mla_kv_norm, mla_w_ukv, w_br_ssd, w_br_ret, w_br_att, w_br_mla, w_merge, b_merge, w_out, w_ffn_up, ffn_conv_w, ffn_conv_b, w_ffn_down):
    cvec = jnp.concatenate([c_ctx[None, :], c, jnp.zeros((SUBLANES - 1 - DEC_BATCH, D_MODEL), F32)], axis=0)
    mod5 = _modulation(cvec, w_mod, b_mod).reshape(DEPTH, SUBLANES, 6, 1, D_MODEL)

    lane_dims_64 = [d % 64 for d in range(LANES)]
    lane_dims_mla = [d - 64 if 64 <= d < 96 else -1 for d in range(LANES)]
    tabs64 = _rope_tables(DEC_SEQ, ATT_HEAD_DIM, lane_dims_64)
    tabs_mla = _rope_tables(DEC_SEQ, MLA_ROPE, lane_dims_mla)
    expand = _ssd_expand_matrix()

    xp = x_prompt.reshape(BATCH * SEQ, D_MODEL)
    xs = x_sample.reshape(DEC_BATCH * DEC_SEQ, D_MODEL)
    caches = []
    mw = _matmul_weights(w_in, mla_w_uq, mla_w_ukv, w_br_ssd, w_br_ret, w_br_att, w_br_mla, w_merge, w_out,
                         w_ffn_up, ffn_conv_w, ffn_conv_b, w_ffn_down)
    for i in range(DEPTH):
        lp = _layer_params(i, ssd_conv_w, ssd_conv_b, ssd_dt_bias, ssd_a_log, ssd_d, ssd_norm_w,
                           ret_decay_logit, ret_gn_w, att_q_norm, att_k_norm, mla_q_norm, mla_kv_norm,
                           b_merge, g_pre_mix, g_post_mix, g_pre_ffn, g_post_ffn)
        xp, cache = _trunk_pass(xp, mod5, i, False, BATCH, SEQ, lp, mw, tabs64, tabs_mla, expand, None)
        caches.append(cache)
        ctx = dict(ssd=_ssd_state_to_pairs(state_ssd[:, i]), ret=_ret_state_to_lanes(state_ret[:, i]),
                   att_k=cache_att_k[:, i].reshape(DEC_BATCH, PAST_LEN, LANES),
                   att_v=cache_att_v[:, i].reshape(DEC_BATCH, PAST_LEN, LANES),
                   mla_ckv=cache_mla_ckv[:, i],
                   mla_kr=jnp.pad(cache_mla_krope[:, i], ((0, 0), (0, 0), (64, 32))))
        xs, _ = _trunk_pass(xs, mod5, i, True, DEC_BATCH, DEC_SEQ, lp, mw, tabs64, tabs_mla, expand, ctx)

    stack = lambda k: jnp.stack([cc[k] for cc in caches], axis=1)
    return (xp.reshape(BATCH, SEQ, D_MODEL), xs.reshape(DEC_BATCH, DEC_SEQ, D_MODEL),
            stack('ssd'), stack('ret'), stack('att_k'), stack('att_v'), stack('mla_ckv'), stack('mla_krope'))
```

```python
import functools

import numpy as np
import jax
import jax.numpy as jnp
from jax import lax
from jax.experimental import pallas as pl
from jax.experimental.pallas import tpu as pltpu

D_MODEL = 1024
BATCH = 32
SEQ = 256
DEPTH = 2
DEC_BATCH = 4
DEC_SEQ = 2048
PAST_LEN = 256
GRID_W = 64
ROPE_BASE = 10000.0
NORM_EPS = 1e-6
SCAN_CHUNK = 128
SSD_HEADS = 8
SSD_HEAD_DIM = 64
SSD_INNER = 512
SSD_GROUPS = 2
SSD_STATE = 64
SSD_CONV_CH = 768
RET_HEADS = 4
RET_QK = 64
RET_V = 128
ATT_HEADS = 8
ATT_KV_HEADS = 2
ATT_HEAD_DIM = 64
MLA_HEADS = 8
MLA_Q_RANK = 384
MLA_KV_RANK = 256
MLA_NOPE = 64
MLA_ROPE = 32
MLA_V = 64
D_FF = 2816

F32 = jnp.float32
BF16 = jnp.bfloat16
LANES = 128
SUBLANES = 8
VMEM_LIMIT = 56 * 1024 * 1024
NEG_BIG = -1e30

SSD_W = 1408
RET_W = 1536
ATT_MLA_W = 1536
TM_ROWS = 512
RET_UNROLL = 4
SSD_UNROLL = 4
ATT_SUB_ROWS = 128
FF_HALF = D_FF


def _cparams(n_axes):
    return pltpu.CompilerParams(dimension_semantics=("arbitrary",) * n_axes,
                                vmem_limit_bytes=VMEM_LIMIT)


def _silu(x):
    return x * jax.nn.sigmoid(x)


def _softplus(x):
    return jnp.maximum(x, 0.0) + jnp.log1p(jnp.exp(-jnp.abs(x)))


def _rms(x, w):
    ms = jnp.mean(x * x, axis=-1, keepdims=True)
    return x * lax.rsqrt(ms + NORM_EPS) * w


def _bdot(a, b):
    return jnp.dot(a.astype(BF16), b.astype(BF16), preferred_element_type=F32)


def _bdot_nt(a, b):
    return lax.dot_general(a.astype(BF16), b.astype(BF16), (((1,), (1,)), ((), ())),
                           preferred_element_type=F32)


def _bdot_tn(a, b):
    return lax.dot_general(a.astype(BF16), b.astype(BF16), (((0,), (0,)), ((), ())),
                           preferred_element_type=F32)


def _split_dot(a, b_bf16):
    hi = a.astype(BF16)
    lo = (a - hi.astype(F32)).astype(BF16)
    return (jnp.dot(hi, b_bf16, preferred_element_type=F32)
            + jnp.dot(lo, b_bf16, preferred_element_type=F32))


def _split_dot_left(a_bf16, b):
    hi = b.astype(BF16)
    lo = (b - hi.astype(F32)).astype(BF16)
    return (jnp.dot(a_bf16, hi, preferred_element_type=F32)
            + jnp.dot(a_bf16, lo, preferred_element_type=F32))


def _lane_iota(shape):
    return lax.broadcasted_iota(jnp.int32, shape, len(shape) - 1)


def _row_iota(shape):
    return lax.broadcasted_iota(jnp.int32, shape, 0)


def _rope(xb, cos, sin_up, sin_dn, half):
    return (xb * cos + pltpu.roll(xb, half, 1) * sin_up
            + pltpu.roll(xb, LANES - half, 1) * sin_dn)


def _half_rms(xb, w_row):
    lo = _lane_iota(xb.shape) < 64
    sq = xb * xb
    s_lo = jnp.sum(jnp.where(lo, sq, 0.0), axis=-1, keepdims=True)
    s_hi = jnp.sum(jnp.where(lo, 0.0, sq), axis=-1, keepdims=True)
    ms = jnp.where(lo, s_lo, s_hi) * (1.0 / 64.0)
    return xb * lax.rsqrt(ms + NORM_EPS) * w_row


def _mod_kernel(c_ref, w_ref, b_ref, o_ref):
    c = c_ref[...]
    o_ref[...] = _bdot(_silu(c), w_ref[...]) + b_ref[...]


def _modulation(cvec, w_mod, b_mod):
    tn = 1536
    n_out = 6 * D_MODEL
    return pl.pallas_call(
        _mod_kernel,
        grid=(DEPTH, n_out // tn),
        in_specs=[pl.BlockSpec((SUBLANES, D_MODEL), lambda l, j: (0, 0)),
                  pl.BlockSpec((None, D_MODEL, tn), lambda l, j: (l, 0, j)),
                  pl.BlockSpec((None, 1, tn), lambda l, j: (l, 0, j))],
        out_specs=pl.BlockSpec((None, SUBLANES, tn), lambda l, j: (l, 0, j)),
        out_shape=jax.ShapeDtypeStruct((DEPTH, SUBLANES, n_out), F32),
        compiler_params=_cparams(2),
        name="modulation",
    )(cvec, w_mod, b_mod.reshape(DEPTH, 1, n_out))


def _mod_spec(layer, latent, seq_len, tm, k):
    if latent:
        assert seq_len % tm == 0
    tiles_per_seq = max(seq_len // tm, 1)

    def index_map(i):
        row = 1 + i // tiles_per_seq if latent else 0
        return (layer, row, k, 0, 0)

    return pl.BlockSpec((None, None, None, 1, D_MODEL), index_map)


def _layer_spec(shape2d, layer):
    return pl.BlockSpec((None,) + tuple(shape2d), lambda *_: (layer, 0, 0))


def _resident(block_shape, index_map):
    return pl.BlockSpec(block_shape, index_map, pipeline_mode=pl.Buffered(1))


def _inproj_kernel(latent, *refs):
    if latent:
        (x_ref, sh_ref, sc_ref, g_ref, ws_ref, wr_ref, wa_ref, qn_ref, kn_ref, cn_ref,
         c64_ref, su64_ref, sd64_ref, cm_ref, sum_ref, sdm_ref,
         ssd_ref, ret_ref, aq_ref, ak_ref, av_ref, mq_ref, ckv_ref, kr_ref) = refs
    else:
        (x_ref, sh_ref, sc_ref, g_ref, ws_ref, wr_ref, wa_ref, qn_ref, kn_ref, cn_ref,
         ssd_ref, ret_ref, aq_ref, ak_ref, av_ref, mq_ref, ckv_ref, kr_ref) = refs

    half = x_ref.shape[0] // 2
    for r in range(2):
        rows = slice(r * half, (r + 1) * half)
        h = _rms(x_ref[rows, :], g_ref[...]) * (1.0 + sc_ref[...]) + sh_ref[...]
        hb = h.astype(BF16)

        def proj(w_ref):
            return jnp.dot(hb, w_ref[...], preferred_element_type=F32)

        def rope64(xb):
            if not latent:
                return xb
            return _rope(xb, c64_ref[rows, :], su64_ref[rows, :], sd64_ref[rows, :], 16)

        pa = proj(wa_ref)
        for j in range(4):
            q = _half_rms(pa[:, j * LANES:(j + 1) * LANES], qn_ref[...])
            aq_ref[rows, j * LANES:(j + 1) * LANES] = rope64(q)
        ak_ref[rows, :] = rope64(_half_rms(pa[:, 512:640], kn_ref[...]))
        av_ref[rows, :] = pa[:, 640:768]
        mq_ref[rows, :] = pa[:, 768:768 + MLA_Q_RANK]
        ckv_ref[rows, :] = _rms(pa[:, 768 + MLA_Q_RANK:768 + MLA_Q_RANK + MLA_KV_RANK], cn_ref[...])
        kr = pa[:, 1408:1536]
        if latent:
            kr = _rope(kr, cm_ref[rows, :], sum_ref[rows, :], sdm_ref[rows, :], 8)
        kr_ref[rows, :] = kr

        pr = proj(wr_ref)
        for j in range(2):
            ret_ref[rows, j * LANES:(j + 1) * LANES] = rope64(pr[:, j * LANES:(j + 1) * LANES])
        for j in range(2, 4):
            ret_ref[rows, j * LANES:(j + 1) * LANES] = rope64(pr[:, j * LANES:(j + 1) * LANES] * 0.125)
        ret_ref[rows, 512:RET_W] = pr[:, 512:RET_W]

        ssd_ref[rows, :] = proj(ws_ref)


def _inproj(x2d, mod5, layer, latent, seq_len, g_pre, w_ssd, w_ret, w_am, qn, kn, cn, tabs):
    n = x2d.shape[0]
    tm = TM_ROWS
    tiles_per_seq = max(seq_len // tm, 1)
    row = lambda i: (i, 0)
    lyr3 = lambda i: (layer, 0, 0)
    in_specs = [pl.BlockSpec((tm, D_MODEL), row),
                _mod_spec(layer, latent, seq_len, tm, 0),
                _mod_spec(layer, latent, seq_len, tm, 1),
                _layer_spec((1, D_MODEL), layer),
                _resident((None, D_MODEL, SSD_W), lyr3),
                _resident((None, D_MODEL, RET_W), lyr3),
                _resident((None, D_MODEL, ATT_MLA_W), lyr3),
                _layer_spec((1, LANES), layer),
                _layer_spec((1, LANES), layer),
                _layer_spec((1, MLA_KV_RANK), layer)]
    args = [x2d, mod5, mod5, g_pre, w_ssd, w_ret, w_am, qn, kn, cn]
    if latent:
        tab = pl.BlockSpec((tm, LANES), lambda i: (i % tiles_per_seq, 0))
        in_specs += [tab] * 6
        args += list(tabs)
    widths = (SSD_W, RET_W, 512, LANES, LANES, MLA_Q_RANK, MLA_KV_RANK, LANES)
    return pl.pallas_call(
        functools.partial(_inproj_kernel, latent),
        grid=(n // tm,),
        in_specs=in_specs,
        out_specs=[pl.BlockSpec((tm, w), row) for w in widths],
        out_shape=[jax.ShapeDtypeStruct((n, w), F32) for w in widths],
        compiler_params=_cparams(1),
        name="inproj_lat" if latent else "inproj_ctx",
    )(*args)


def _ssd_kernel(latent, seq_len, *refs):
    if latent:
        (in_ref, cw_ref, cb_ref, dtb_ref, alog_ref, dsk_ref, nw_ref, ex_ref, s0_ref,
         o_ref, xc_ref, dt_ref, cum_ref, sf_ref, sb_ref, cdb_ref, run_ref) = refs
    else:
        (in_ref, cw_ref, cb_ref, dtb_ref, alog_ref, dsk_ref, nw_ref, ex_ref,
         o_ref, st_ref, xc_ref, dt_ref, cum_ref, sf_ref, sb_ref, cdb_ref, run_ref) = refs
    T = SCAN_CHUNK
    nc = seq_len // T
    ii = _row_iota((T, T))
    jj = _lane_iota((T, T))
    tri_lo = (jj <= ii).astype(BF16)
    tri_up = (jj >= ii).astype(BF16)
    lane_c = _lane_iota((T, LANES))
    row_c = _row_iota((T, SSD_CONV_CH))
    fwd_lane = lane_c < SSD_HEADS
    a_row = -jnp.exp(alog_ref[...])
    expand = ex_ref[...]

    if latent:
        run_ref[...] = s0_ref[...]
    else:
        run_ref[...] = jnp.zeros(run_ref.shape, F32)

    def phase0(c, carry):
        r0 = pl.multiple_of(c * T, T)
        x_cur = in_ref[pl.ds(r0, T), 512:1280]
        prev_blk = in_ref[pl.ds(pl.multiple_of(jnp.maximum(r0 - SUBLANES, 0), SUBLANES), SUBLANES), 512:1280]
        next_blk = in_ref[pl.ds(pl.multiple_of(jnp.minimum(r0 + T, seq_len - SUBLANES), SUBLANES), SUBLANES), 512:1280]
        prev_row = jnp.where(c > 0, prev_blk[SUBLANES - 1:SUBLANES, :], 0.0)
        next_row = jnp.where(c < nc - 1, next_blk[0:1, :], 0.0)
        x_prev = jnp.where(row_c == 0, prev_row, pltpu.roll(x_cur, 1, 0))
        x_next = jnp.where(row_c == T - 1, next_row, pltpu.roll(x_cur, T - 1, 0))
        xc = _silu(cb_ref[...] + x_prev * cw_ref[0:1, :] + x_cur * cw_ref[1:2, :] + x_next * cw_ref[2:3, :])
        xc_ref[pl.ds(r0, T), :] = xc

        dt = _softplus(in_ref[pl.ds(r0, T), 1280:SSD_W] + dtb_ref[...])
        la = dt * a_row
        cum = jnp.where(fwd_lane, _split_dot_left(tri_lo, la), _split_dot_left(tri_up, la))
        tot = jnp.where(fwd_lane[0:1], cum[T - 1:T, :], cum[0:1, :])
        dt_ref[pl.ds(r0, T), :] = dt
        cum_ref[pl.ds(r0, T), :] = cum
        w = dt * jnp.exp(tot - cum)
        wcd = jnp.concatenate([w, jnp.broadcast_to(jnp.exp(tot), (SUBLANES, LANES))], axis=0)
        wcd_x = _split_dot(wcd, expand)
        w_x = wcd_x[0:T]
        cd_x = wcd_x[T:T + 1]
        cdb_ref[c] = jnp.broadcast_to(cd_x[:, 512:1024], (SUBLANES, 512))
        xs = xc[:, 0:512]
        bm = xc[:, 512:640].astype(BF16)
        for p in range(4):
            g = p // 2
            sl = slice(p * LANES, (p + 1) * LANES)
            vw = jnp.concatenate([xs[:, sl] * w_x[:, sl], xs[:, sl] * w_x[:, 512 + p * LANES:512 + (p + 1) * LANES]],
                                 axis=1)
            cs = _bdot_tn(bm, vw)[g * 64:(g + 1) * 64]
            s_run = run_ref[p]
            sf_ref[c, p] = s_run[:, 0:LANES]
            sb_ref[c, p] = cs[:, LANES:2 * LANES]
            new_f = s_run[:, 0:LANES] * cd_x[:, sl] + cs[:, 0:LANES]
            run_ref[p] = jnp.concatenate([new_f, s_run[:, LANES:2 * LANES]], axis=1)
        return carry

    lax.fori_loop(0, nc, phase0, 0, unroll=min(nc, SSD_UNROLL))

    def bwd_states(k, carry):
        c = nc - 1 - k
        cd = cdb_ref[c]
        for p in range(4):
            s_run = run_ref[p]
            cs_b = sb_ref[c, p]
            sb_ref[c, p] = s_run[:, LANES:2 * LANES]
            new_b = s_run[:, LANES:2 * LANES] * cd[0:1, p * LANES:(p + 1) * LANES] + cs_b
            run_ref[p] = jnp.concatenate([s_run[:, 0:LANES], new_b], axis=1)
        return carry

    lax.fori_loop(0, nc, bwd_states, 0)
    if not latent:
        for p in range(4):
            s_fin = run_ref[p]
            for d in range(2):
                for hh in range(2):
                    c0 = d * LANES + hh * 64
                    st_ref[d, 2 * p + hh] = s_fin[:, c0:c0 + 64]

    lower = jj <= ii
    upper = jj >= ii
    zeros64 = jnp.zeros((64, 2 * LANES), F32)

    def phase1(c, carry):
        r0 = pl.multiple_of(c * T, T)
        xc = xc_ref[pl.ds(r0, T), :]
        xs = xc[:, 0:512]
        bm = xc[:, 512:640].astype(BF16)
        cm = xc[:, 640:768]
        dt = dt_ref[pl.ds(r0, T), :]
        cum = cum_ref[pl.ds(r0, T), :]
        cum_t = cum.T
        dt_t = dt.T
        e_x = _split_dot(jnp.exp(cum), expand)
        gmat = [_bdot_nt(jnp.where((lane_c < 64) == (g == 0), cm, 0.0), bm) for g in range(2)]
        cmb = cm.astype(BF16)
        outs = []
        for p in range(4):
            g = p // 2
            sl = slice(p * LANES, (p + 1) * LANES)
            xs_p = xs[:, sl].astype(BF16)
            halves = []
            for h in (2 * p, 2 * p + 1):
                df = jnp.exp(jnp.where(lower, cum[:, h:h + 1] - cum_t[h:h + 1, :], NEG_BIG)) * dt_t[h:h + 1, :]
                hb = SSD_HEADS + h
                db = jnp.exp(jnp.where(upper, cum[:, hb:hb + 1] - cum_t[hb:hb + 1, :], NEG_BIG)) * dt_t[hb:hb + 1, :]
                m_h = (gmat[g] * (df + db)).astype(BF16)
                halves.append(jnp.dot(m_h, xs_p, preferred_element_type=F32))
            o_p = jnp.where(lane_c < 64, halves[0], halves[1])
            s_in = jnp.concatenate([sf_ref[c, p], sb_ref[c, p]], axis=1)
            s_pad = jnp.concatenate([s_in, zeros64] if g == 0 else [zeros64, s_in], axis=0)
            oi = jnp.dot(cmb, s_pad.astype(BF16), preferred_element_type=F32)
            o_p = o_p + oi[:, 0:LANES] * e_x[:, sl] + oi[:, LANES:2 * LANES] * e_x[:, 512 + p * LANES:512 + (p + 1) * LANES]
            outs.append(o_p)
        y = jnp.concatenate(outs, axis=1) + dsk_ref[...] * xs
        z = in_ref[pl.ds(r0, T), 0:512]
        o_ref[pl.ds(r0, T), :] = _rms(y * _silu(z), nw_ref[...])
        return carry

    lax.fori_loop(0, nc, phase1, 0)


def _ssd_mixer(ssd_in, bsz, seq_len, latent, layer, cw, cb, dtb, alog, dsk, nw, expand, s0):
    nc = seq_len // SCAN_CHUNK
    const2 = lambda b: (0, 0)
    per_b3 = lambda b: (b, 0, 0)
    per_b4 = lambda b: (b, 0, 0, 0)
    in_specs = [pl.BlockSpec((None, seq_len, SSD_W), per_b3),
                _layer_spec((3, SSD_CONV_CH), layer),
                _layer_spec((1, SSD_CONV_CH), layer),
                _layer_spec((1, LANES), layer),
                _layer_spec((1, LANES), layer),
                _layer_spec((1, SSD_INNER), layer),
                _layer_spec((1, SSD_INNER), layer),
                pl.BlockSpec((LANES, 1024), const2)]
    args = [ssd_in.reshape(bsz, seq_len, SSD_W), cw, cb, dtb, alog, dsk, nw, expand]
    out_specs = [pl.BlockSpec((None, seq_len, SSD_INNER), per_b3)]
    out_shape = [jax.ShapeDtypeStruct((bsz, seq_len, SSD_INNER), F32)]
    if latent:
        in_specs.append(pl.BlockSpec((None, 4, 64, 2 * LANES), per_b4))
        args.append(s0)
    else:
        out_specs.append(pl.BlockSpec((None, 2, SSD_HEADS, SSD_STATE, SSD_HEAD_DIM), lambda b: (b, 0, 0, 0, 0)))
        out_shape.append(jax.ShapeDtypeStruct((bsz, 2, SSD_HEADS, SSD_STATE, SSD_HEAD_DIM), F32))
    scratch = [pltpu.VMEM((seq_len, SSD_CONV_CH), F32),
               pltpu.VMEM((seq_len, LANES), F32),
               pltpu.VMEM((seq_len, LANES), F32),
               pltpu.VMEM((nc, 4, 64, LANES), F32),
               pltpu.VMEM((nc, 4, 64, LANES), F32),
               pltpu.VMEM((nc, SUBLANES, 512), F32),
               pltpu.VMEM((4, 64, 2 * LANES), F32)]
    res = pl.pallas_call(
        functools.partial(_ssd_kernel, latent, seq_len),
        grid=(bsz,),
        in_specs=in_specs, out_specs=out_specs, out_shape=out_shape,
        scratch_shapes=scratch,
        compiler_params=_cparams(1),
        name="ssd_lat" if latent else "ssd_ctx",
    )(*args)
    return res


def _ret_kernel(latent, seq_len, *refs):
    if latent:
        (in_ref, lg_ref, gn_ref, s0_ref,
         o_ref, dc_ref, we_ref, cd_ref, sf_ref, sb_ref, run_ref) = refs
    else:
        (in_ref, lg_ref, gn_ref,
         o_ref, st_ref, dc_ref, we_ref, cd_ref, sf_ref, sb_ref, run_ref) = refs
    T = SCAN_CHUNK
    nc = seq_len // T

    @pl.when(pl.program_id(0) == 0)
    def _tables():
        la = -_softplus(-lg_ref[...])
        ii = _row_iota((T, T)).astype(F32)
        jj = _lane_iota((T, T)).astype(F32)
        for h in range(RET_HEADS):
            la_f = la[h:h + 1, :]
            la_b = la[RET_HEADS + h:RET_HEADS + h + 1, :]
            dc_ref[h] = (jnp.exp(jnp.where(jj <= ii, (ii - jj) * la_f, NEG_BIG))
                         + jnp.exp(jnp.where(jj >= ii, (jj - ii) * la_b, NEG_BIG)))
            we_ref[h, 0] = jnp.exp((T - 1.0 - ii) * la_f)
            we_ref[h, 1] = jnp.exp(ii * la_b)
            we_ref[h, 2] = jnp.exp((ii + 1.0) * la_f)
            we_ref[h, 3] = jnp.exp((T - ii) * la_b)
            cd_ref[h] = jnp.concatenate([jnp.broadcast_to(jnp.exp(T * la_f), (SUBLANES, LANES)),
                                         jnp.broadcast_to(jnp.exp(T * la_b), (SUBLANES, LANES))], axis=1)

    if latent:
        run_ref[...] = s0_ref[...]
    else:
        run_ref[...] = jnp.zeros(run_ref.shape, F32)

    def phase0(c, carry):
        r0 = pl.multiple_of(c * T, T)
        for h in range(RET_HEADS):
            blk = h // 2
            k_blk = in_ref[pl.ds(r0, T), 256 + blk * LANES:256 + (blk + 1) * LANES].astype(BF16)
            v_h = in_ref[pl.ds(r0, T), 512 + h * LANES:512 + (h + 1) * LANES]
            vw = jnp.concatenate([v_h * we_ref[h, 0], v_h * we_ref[h, 1]], axis=1)
            cs = _bdot_tn(k_blk, vw)[(h % 2) * 64:(h % 2 + 1) * 64]
            s_run = run_ref[h]
            sf_ref[c, h] = s_run[:, 0:LANES]
            sb_ref[c, h] = cs[:, LANES:2 * LANES]
            new_f = s_run[:, 0:LANES] * cd_ref[h, 0:1, 0:LANES] + cs[:, 0:LANES]
            run_ref[h] = jnp.concatenate([new_f, s_run[:, LANES:2 * LANES]], axis=1)
        return carry

    unroll = min(nc, RET_UNROLL)
    lax.fori_loop(0, nc, phase0, 0, unroll=unroll)

    def bwd_states(k, carry):
        c = nc - 1 - k
        for h in range(RET_HEADS):
            s_run = run_ref[h]
            cs_b = sb_ref[c, h]
            sb_ref[c, h] = s_run[:, LANES:2 * LANES]
            new_b = s_run[:, LANES:2 * LANES] * cd_ref[h, 0:1, LANES:2 * LANES] + cs_b
            run_ref[h] = jnp.concatenate([s_run[:, 0:LANES], new_b], axis=1)
        return carry

    lax.fori_loop(0, nc, bwd_states, 0)
    if not latent:
        for h in range(RET_HEADS):
            for d in range(2):
                st_ref[d, h] = run_ref[h, :, d * LANES:(d + 1) * LANES]

    lane_c = _lane_iota((T, LANES))
    zeros64 = jnp.zeros((64, 2 * LANES), F32)

    def phase1(c, carry):
        r0 = pl.multiple_of(c * T, T)
        for h in range(RET_HEADS):
            blk = h // 2
            q_blk = in_ref[pl.ds(r0, T), blk * LANES:(blk + 1) * LANES]
            k_blk = in_ref[pl.ds(r0, T), 256 + blk * LANES:256 + (blk + 1) * LANES].astype(BF16)
            v_h = in_ref[pl.ds(r0, T), 512 + h * LANES:512 + (h + 1) * LANES].astype(BF16)
            g_h = in_ref[pl.ds(r0, T), 1024 + h * LANES:1024 + (h + 1) * LANES]
            q_m = jnp.where((lane_c < 64) == (h % 2 == 0), q_blk, 0.0)
            m_h = (_bdot_nt(q_m, k_blk) * dc_ref[h]).astype(BF16)
            o = jnp.dot(m_h, v_h, preferred_element_type=F32)
            s_in = jnp.concatenate([sf_ref[c, h], sb_ref[c, h]], axis=1)
            s_pad = jnp.concatenate([s_in, zeros64] if h % 2 == 0 else [zeros64, s_in], axis=0)
            oi = _bdot(q_blk, s_pad)
            o = o + oi[:, 0:LANES] * we_ref[h, 2] + oi[:, LANES:2 * LANES] * we_ref[h, 3]
            oc = o - jnp.mean(o, axis=-1, keepdims=True)
            on = oc * lax.rsqrt(jnp.mean(oc * oc, axis=-1, keepdims=True) + NORM_EPS)
            o_ref[pl.ds(r0, T), h * LANES:(h + 1) * LANES] = (
                on * gn_ref[:, h * LANES:(h + 1) * LANES] * _silu(g_h))
        return carry

    lax.fori_loop(0, nc, phase1, 0, unroll=unroll)


def _ret_mixer(ret_in, bsz, seq_len, latent, layer, logit_rows, gn_w, s0):
    nc = seq_len // SCAN_CHUNK
    T = SCAN_CHUNK
    per_b3 = lambda b: (b, 0, 0)
    per_b4 = lambda b: (b, 0, 0, 0)
    in_specs = [pl.BlockSpec((None, seq_len, RET_W), per_b3),
                _layer_spec((SUBLANES, LANES), layer),
                _layer_spec((1, 512), layer)]
    args = [ret_in.reshape(bsz, seq_len, RET_W), logit_rows, gn_w]
    out_specs = [pl.BlockSpec((None, seq_len, 512), per_b3)]
    out_shape = [jax.ShapeDtypeStruct((bsz, seq_len, 512), F32)]
    if latent:
        in_specs.append(pl.BlockSpec((None, RET_HEADS, 64, 2 * LANES), per_b4))
        args.append(s0)
    else:
        out_specs.append(pl.BlockSpec((None, 2, RET_HEADS, RET_QK, RET_V), lambda b: (b, 0, 0, 0, 0)))
        out_shape.append(jax.ShapeDtypeStruct((bsz, 2, RET_HEADS, RET_QK, RET_V), F32))
    scratch = [pltpu.VMEM((RET_HEADS, T, T), F32),
               pltpu.VMEM((RET_HEADS, 4, T, LANES), F32),
               pltpu.VMEM((RET_HEADS, SUBLANES, 2 * LANES), F32),
               pltpu.VMEM((nc, RET_HEADS, 64, LANES), F32),
               pltpu.VMEM((nc, RET_HEADS, 64, LANES), F32),
               pltpu.VMEM((RET_HEADS, 64, 2 * LANES), F32)]
    return pl.pallas_call(
        functools.partial(_ret_kernel, latent, seq_len),
        grid=(bsz,),
        in_specs=in_specs, out_specs=out_specs, out_shape=out_shape,
        scratch_shapes=scratch,
        compiler_params=_cparams(1),
        name="ret_lat" if latent else "ret_ctx",
    )(*args)


LOG2_E = 1.4426950408889634


def _softmax_pv(s, v, scale=1.0):
    m = jnp.max(s, axis=-1, keepdims=True)
    p = jnp.exp2((s - m) * (scale * LOG2_E))
    l = jnp.sum(p, axis=-1, keepdims=True)
    return jnp.dot(p.astype(BF16), v, preferred_element_type=F32) / l


def _att_kernel(latent, tq, *refs):
    if latent:
        q_ref, kn_ref, vn_ref, kc_ref, vc_ref, o_ref, kb_ref, vb_ref = refs
    else:
        q_ref, kn_ref, vn_ref, o_ref, kb_ref, vb_ref = refs

    @pl.when(pl.program_id(1) == 0)
    def _fill():
        if latent:
            kb_ref[0:PAST_LEN, :] = kc_ref[...].astype(BF16)
            vb_ref[0:PAST_LEN, :] = vc_ref[...].astype(BF16)
            kb_ref[PAST_LEN:, :] = kn_ref[...].astype(BF16)
            vb_ref[PAST_LEN:, :] = vn_ref[...].astype(BF16)
        else:
            kb_ref[...] = kn_ref[...].astype(BF16)
            vb_ref[...] = vn_ref[...].astype(BF16)

    sub = min(tq, ATT_SUB_ROWS)
    lane_q = _lane_iota((sub, LANES))
    for i in range(tq // sub):
        rows = slice(i * sub, (i + 1) * sub)
        acc = []
        for g in range(ATT_KV_HEADS):
            qs = jnp.concatenate(
                [jnp.where((lane_q < 64) == (g == 0), q_ref[rows, j * LANES:(j + 1) * LANES] * 0.125, 0.0).astype(BF16)
                 for j in range(4)], axis=0)
            s = lax.dot_general(qs, kb_ref[...], (((1,), (1,)), ((), ())), preferred_element_type=F32)
            acc.append(_softmax_pv(s, vb_ref[...]))
        for j in range(4):
            o_ref[rows, j * LANES:(j + 1) * LANES] = jnp.where(lane_q < 64, acc[0][j * sub:(j + 1) * sub],
                                                               acc[1][j * sub:(j + 1) * sub])


def _att_mixer(aq, ak, av, bsz, seq_len, latent, k_ctx, v_ctx):
    tq = 512 if latent else 256
    lk = seq_len + (PAST_LEN if latent else 0)
    qmap = lambda b, i: (b, i, 0)
    bmap = lambda b, i: (b, 0, 0)
    in_specs = [pl.BlockSpec((None, tq, 512), qmap),
                pl.BlockSpec((None, seq_len, LANES), bmap),
                pl.BlockSpec((None, seq_len, LANES), bmap)]
    args = [aq.reshape(bsz, seq_len, 512), ak.reshape(bsz, seq_len, LANES), av.reshape(bsz, seq_len, LANES)]
    if latent:
        in_specs += [pl.BlockSpec((None, PAST_LEN, LANES), bmap)] * 2
        args += [k_ctx, v_ctx]
    return pl.pallas_call(
        functools.partial(_att_kernel, latent, tq),
        grid=(bsz, seq_len // tq),
        in_specs=in_specs,
        out_specs=pl.BlockSpec((None, tq, 512), qmap),
        out_shape=jax.ShapeDtypeStruct((bsz, seq_len, 512), F32),
        scratch_shapes=[pltpu.VMEM((lk, LANES), BF16), pltpu.VMEM((lk, LANES), BF16)],
        compiler_params=_cparams(2),
        name="att_lat" if latent else "att_ctx",
    )(*args)


def _mla_kernel(latent, tq, *refs):
    if latent:
        (mq_ref, cn_ref, rn_ref, cc_ref, rc_ref, qn_ref, wq_ref, wk_ref, wv_ref,
         cos_ref, su_ref, sd_ref, o_ref, kb_ref, vb_ref) = refs
    else:
        (mq_ref, cn_ref, rn_ref, qn_ref, wq_ref, wk_ref, wv_ref, o_ref, kb_ref, vb_ref) = refs

    def fill(r0, ckv, kr):
        cb = ckv.astype(BF16)
        n = ckv.shape[0]
        kn = jnp.dot(cb, wk_ref[...], preferred_element_type=F32)
        for h in range(MLA_HEADS):
            kb_ref[r0:r0 + n, h * LANES:(h + 1) * LANES] = (kn[:, h * LANES:(h + 1) * LANES] + kr).astype(BF16)
        vb_ref[r0:r0 + n, :] = jnp.dot(cb, wv_ref[...], preferred_element_type=F32).astype(BF16)

    @pl.when(pl.program_id(1) == 0)
    def _fill():
        if latent:
            fill(0, cc_ref[...], rc_ref[...])
            fill(PAST_LEN, cn_ref[...], rn_ref[...])
        else:
            fill(0, cn_ref[...], rn_ref[...])

    q = _bdot(_rms(mq_ref[...], qn_ref[...]), wq_ref[...])
    lane_q = _lane_iota((tq, LANES))
    scale = (MLA_NOPE + MLA_ROPE) ** -0.5
    for p in range(4):
        halves = []
        for h in (2 * p, 2 * p + 1):
            qh = q[:, h * LANES:(h + 1) * LANES]
            if latent:
                qh = _rope(qh, cos_ref[...], su_ref[...], sd_ref[...], 8)
            s = _bdot_nt(qh, kb_ref[:, h * LANES:(h + 1) * LANES])
            halves.append(_softmax_pv(s, vb_ref[:, p * LANES:(p + 1) * LANES], scale))
        o_ref[:, p * LANES:(p + 1) * LANES] = jnp.where(lane_q < 64, halves[0], halves[1])


def _mla_mixer(mq, ckv, kr, bsz, seq_len, latent, ckv_ctx, kr_ctx, qn, wq, wk, wv, layer, tabs):
    tq = 512 if latent else 256
    lk = seq_len + (PAST_LEN if latent else 0)
    qmap = lambda b, i: (b, i, 0)
    bmap = lambda b, i: (b, 0, 0)
    const = lambda b, i: (0, 0)
    lyr3 = lambda b, i: (layer, 0, 0)
    in_specs = [pl.BlockSpec((None, tq, MLA_Q_RANK), qmap),
                pl.BlockSpec((None, seq_len, MLA_KV_RANK), bmap),
                pl.BlockSpec((None, seq_len, LANES), bmap)]
    args = [mq.reshape(bsz, seq_len, MLA_Q_RANK), ckv.reshape(bsz, seq_len, MLA_KV_RANK),
            kr.reshape(bsz, seq_len, LANES)]
    if latent:
        in_specs += [pl.BlockSpec((None, PAST_LEN, MLA_KV_RANK), bmap),
                     pl.BlockSpec((None, PAST_LEN, LANES), bmap)]
        args += [ckv_ctx, kr_ctx]
    in_specs += [_layer_spec((1, MLA_Q_RANK), layer),
                 _resident((None, MLA_Q_RANK, 1024), lyr3),
                 _resident((None, MLA_KV_RANK, 1024), lyr3),
                 _resident((None, MLA_KV_RANK, 512), lyr3)]
    args += [qn, wq, wk, wv]
    if latent:
        in_specs += [pl.BlockSpec((tq, LANES), lambda b, i: (i, 0))] * 3
        args += list(tabs)
    return pl.pallas_call(
        functools.partial(_mla_kernel, latent, tq),
        grid=(bsz, seq_len // tq),
        in_specs=in_specs,
        out_specs=pl.BlockSpec((None, tq, 512), qmap),
        out_shape=jax.ShapeDtypeStruct((bsz, seq_len, 512), F32),
        scratch_shapes=[pltpu.VMEM((lk, 1024), BF16), pltpu.VMEM((lk, 512), BF16)],
        compiler_params=_cparams(2),
        name="mla_lat" if latent else "mla_ctx",
    )(*args)


def _merge_kernel(x_ref, sh_ref, sc_ref, ga_ref, gpre_ref, gpost_ref, o1_ref, o2_ref, o3_ref, o4_ref,
                  wm_ref, bm_ref, wb_ref, wo_ref, y_ref):
    x = x_ref[...]
    hb = (_rms(x, gpre_ref[...]) * (1.0 + sc_ref[...]) + sh_ref[...]).astype(BF16)
    merged = None
    for k, o_ref in enumerate((o1_ref, o2_ref, o3_ref, o4_ref)):
        cols = slice(k * D_MODEL, (k + 1) * D_MODEL)
        gate = jax.nn.sigmoid(jnp.dot(hb, wm_ref[:, cols], preferred_element_type=F32) + bm_ref[:, cols])
        term = gate * _bdot(o_ref[...], wb_ref[k])
        merged = term if merged is None else merged + term
    m = _bdot(merged, wo_ref[...])
    y_ref[...] = x + ga_ref[...] * _rms(m, gpost_ref[...])


def _merge(x2d, mod5, layer, latent, seq_len, g_pre, g_post, branches, w_merge, b_merge, w_br, w_out):
    n = x2d.shape[0]
    tm = TM_ROWS
    row = lambda i: (i, 0)
    in_specs = ([pl.BlockSpec((tm, D_MODEL), row)]
                + [_mod_spec(layer, latent, seq_len, tm, k) for k in (0, 1, 2)]
                + [_layer_spec((1, D_MODEL), layer)] * 2
                + [pl.BlockSpec((tm, 512), row)] * 4
                + [_resident((None, D_MODEL, 4 * D_MODEL), lambda i: (layer, 0, 0)),
                   _layer_spec((1, 4 * D_MODEL), layer),
                   _resident((None, 4, 512, D_MODEL), lambda i: (layer, 0, 0, 0)),
                   _resident((None, D_MODEL, D_MODEL), lambda i: (layer, 0, 0))])
    return pl.pallas_call(
        _merge_kernel,
        grid=(n // tm,),
        in_specs=in_specs,
        out_specs=pl.BlockSpec((tm, D_MODEL), row),
        out_shape=jax.ShapeDtypeStruct((n, D_MODEL), F32),
        compiler_params=_cparams(1),
        name="merge_lat" if latent else "merge_ctx",
    )(x2d, mod5, mod5, mod5, g_pre, g_post, *[b.reshape(n, 512) for b in branches],
      w_merge, b_merge, w_br, w_out)


def _ffn_kernel(tm, seq_len, x_ref, xp_ref, xn_ref, sh_ref, sc_ref, gf_ref, gpre_ref, gpost_ref,
                wu_ref, cw_ref, cb_ref, wd_ref, y_ref):
    n_seq = max(tm // seq_len, 1)
    seg = tm // n_seq
    tiles_per_seq = seq_len // tm
    if tiles_per_seq > 1:
        i = pl.program_id(0)
        first = (i % tiles_per_seq) == 0
        last = (i % tiles_per_seq) == tiles_per_seq - 1
    else:
        first = last = True

    def nm(x):
        return _rms(x, gpre_ref[...]) * (1.0 + sc_ref[...]) + sh_ref[...]

    x = x_ref[...]
    h = nm(x)
    zero8 = jnp.zeros((SUBLANES, D_MODEL), F32)
    pieces = [jnp.where(first, 0.0, nm(xp_ref[...]))]
    for s in range(n_seq):
        pieces.append(h[s * seg:(s + 1) * seg])
        pieces.append(zero8 if s < n_seq - 1 else jnp.where(last, 0.0, nm(xn_ref[...])))
    hb = jnp.concatenate(pieces, axis=0).astype(BF16)
    rows = tm + SUBLANES * (n_seq + 1)

    def conv_cols(c0):
        cols = slice(c0, c0 + FF_HALF)
        p = jnp.dot(hb, wu_ref[:, cols], preferred_element_type=F32)
        w = cw_ref[:, cols]
        u = (pltpu.roll(p, 1, 0) * w[0:1] + p * w[1:2] + pltpu.roll(p, rows - 1, 0) * w[2:3])
        segs = [u[SUBLANES + s * (seg + SUBLANES):SUBLANES + s * (seg + SUBLANES) + seg] for s in range(n_seq)]
        return (segs[0] if n_seq == 1 else jnp.concatenate(segs, axis=0)) + cb_ref[:, cols]

    acc = None
    for j in range(D_FF // FF_HALF):
        up = conv_cols(j * FF_HALF)
        gate = conv_cols(D_FF + j * FF_HALF)
        part = _bdot(_silu(gate) * up, wd_ref[j * FF_HALF:(j + 1) * FF_HALF, :])
        acc = part if acc is None else acc + part
    y_ref[...] = x + gf_ref[...] * _rms(acc, gpost_ref[...])


def _ffn(x2d, mod5, layer, latent, seq_len, g_pre, g_post, w_up, conv_w, conv_b, w_down):
    n = x2d.shape[0]
    tm = TM_ROWS
    assert seq_len % tm == 0 or tm % seq_len == 0
    hb = tm // SUBLANES
    n_hb = n // SUBLANES
    row = lambda i: (i, 0)
    lyr3 = lambda i: (layer, 0, 0)
    in_specs = ([pl.BlockSpec((tm, D_MODEL), row),
                 pl.BlockSpec((SUBLANES, D_MODEL), lambda i: (jnp.maximum(i * hb - 1, 0), 0)),
                 pl.BlockSpec((SUBLANES, D_MODEL), lambda i: (jnp.minimum((i + 1) * hb, n_hb - 1), 0))]
                + [_mod_spec(layer, latent, seq_len, tm, k) for k in (3, 4, 5)]
                + [_layer_spec((1, D_MODEL), layer)] * 2
                + [_resident((None, D_MODEL, 2 * D_FF), lyr3),
                   _resident((None, 3, 2 * D_FF), lyr3),
                   _resident((None, 1, 2 * D_FF), lyr3),
                   _resident((None, D_FF, D_MODEL), lyr3)])
    return pl.pallas_call(
        functools.partial(_ffn_kernel, tm, seq_len),
        grid=(n // tm,),
        in_specs=in_specs,
        out_specs=pl.BlockSpec((tm, D_MODEL), row),
        out_shape=jax.ShapeDtypeStruct((n, D_MODEL), F32),
        compiler_params=_cparams(1),
        name="ffn_lat" if latent else "ffn_ctx",
    )(x2d, x2d, x2d, mod5, mod5, mod5, g_pre, g_post, w_up, conv_w, conv_b, w_down)


def _rope_tables(seq_len, head_dim, lane_of_dim):
    d_axis = head_dim // 2
    t = np.arange(seq_len)
    pos = np.stack([(t // GRID_W).astype(np.float32), (t % GRID_W).astype(np.float32)], axis=0)
    inv_freq = (np.float32(ROPE_BASE) ** (-np.arange(0, d_axis, 2, dtype=np.float32) / np.float32(d_axis))).astype(np.float32)
    cos = np.ones((seq_len, LANES), np.float32)
    s_up = np.zeros((seq_len, LANES), np.float32)
    s_dn = np.zeros((seq_len, LANES), np.float32)
    for lane, d in enumerate(lane_of_dim):
        if d < 0:
            continue
        axis, j = divmod(d, d_axis)
        second = j >= d_axis // 2
        ang = (pos[axis] * inv_freq[j % (d_axis // 2)]).astype(np.float32)
        cos[:, lane] = np.cos(ang)
        if second:
            s_up[:, lane] = np.sin(ang)
        else:
            s_dn[:, lane] = -np.sin(ang)
    return jnp.asarray(cos), jnp.asarray(s_up), jnp.asarray(s_dn)


def _ssd_expand_matrix():
    e = np.zeros((LANES, 1024), np.float32)
    for d in range(2):
        for h in range(SSD_HEADS):
            e[d * SSD_HEADS + h, d * 512 + h * 64:d * 512 + (h + 1) * 64] = 1.0
    return jnp.asarray(e, dtype=BF16)


def _matmul_weights(w_in, mla_w_uq, mla_w_ukv, w_br_ssd, w_br_ret, w_br_att, w_br_mla, w_merge, w_out,
                    w_ffn_up, ffn_conv_w, ffn_conv_b, w_ffn_down):
    off = np.cumsum([0, 512, 768, 16, 256, 256, 512, 512, 512, 128, 128, 384, 288])
    col = lambda k: w_in[:, :, off[k]:off[k + 1]]
    zeros = lambda n: jnp.zeros((DEPTH, D_MODEL, n), F32)
    w_ssd = jnp.pad(w_in[:, :, :off[3]], ((0, 0), (0, 0), (0, SSD_W - off[3]))).astype(BF16)
    w_ret = w_in[:, :, off[3]:off[7]].astype(BF16)
    aq = col(7).reshape(DEPTH, D_MODEL, ATT_KV_HEADS, 4, ATT_HEAD_DIM).swapaxes(2, 3).reshape(DEPTH, D_MODEL, 512)
    mckv = col(11)
    w_am = jnp.concatenate(
        [aq, col(8), col(9), col(10), mckv[:, :, :MLA_KV_RANK], zeros(64), mckv[:, :, MLA_KV_RANK:], zeros(32)],
        axis=2).astype(BF16)

    uq = mla_w_uq.reshape(DEPTH, MLA_Q_RANK, MLA_HEADS, MLA_NOPE + MLA_ROPE)
    wq = jnp.pad(uq, ((0, 0), (0, 0), (0, 0), (0, 32))).reshape(DEPTH, MLA_Q_RANK, 1024).astype(BF16)
    ukv = mla_w_ukv.reshape(DEPTH, MLA_KV_RANK, MLA_HEADS, MLA_NOPE + MLA_V)
    wk = jnp.pad(ukv[..., :MLA_NOPE], ((0, 0), (0, 0), (0, 0), (0, 64))).reshape(DEPTH, MLA_KV_RANK, 1024).astype(BF16)
    wv = ukv[..., MLA_NOPE:].reshape(DEPTH, MLA_KV_RANK, 512).astype(BF16)

    w_att = (w_br_att.reshape(DEPTH, ATT_KV_HEADS, 4, ATT_HEAD_DIM, D_MODEL).swapaxes(1, 2)
             .reshape(DEPTH, 512, D_MODEL))
    w_br = jnp.stack([w_br_ssd, w_br_ret, w_att, w_br_mla], axis=1).astype(BF16)

    return dict(
        w_ssd=w_ssd, w_ret=w_ret, w_am=w_am, wq=wq, wk=wk, wv=wv, w_br=w_br,
        w_merge=w_merge.astype(BF16), w_out=w_out.astype(BF16),
        w_up=w_ffn_up.astype(BF16), ffn_cw=ffn_conv_w,
        ffn_cb=ffn_conv_b.reshape(DEPTH, 1, 2 * D_FF),
        w_down=w_ffn_down.astype(BF16),
    )


def _small_params(ssd_conv_w, ssd_conv_b, ssd_dt_bias, ssd_a_log, ssd_d, ssd_norm_w,
                  ret_decay_logit, ret_gn_w, att_q_norm, att_k_norm, mla_q_norm, mla_kv_norm,
                  b_merge, g_pre_mix, g_post_mix, g_pre_ffn, g_post_ffn):
    row = lambda v: v.reshape(DEPTH, 1, -1)
    pad_row = lambda v: jnp.pad(row(v), ((0, 0), (0, 0), (0, LANES - v[0].size)))
    return dict(
        qn=row(jnp.tile(att_q_norm, (1, 2))), kn=row(jnp.tile(att_k_norm, (1, 2))),
        cn=row(mla_kv_norm),
        ssd_cw=ssd_conv_w, ssd_cb=row(ssd_conv_b),
        ssd_dtb=pad_row(ssd_dt_bias), ssd_alog=pad_row(ssd_a_log),
        ssd_dsk=row(jnp.repeat(ssd_d, SSD_HEAD_DIM, axis=1)),
        ssd_nw=row(ssd_norm_w),
        ret_logit=jnp.broadcast_to(ret_decay_logit.reshape(DEPTH, SUBLANES, 1), (DEPTH, SUBLANES, LANES)),
        ret_gn=row(ret_gn_w),
        mla_qn=row(mla_q_norm),
        b_merge=row(b_merge),
        g_pre_mix=row(g_pre_mix), g_post_mix=row(g_post_mix),
        g_pre_ffn=row(g_pre_ffn), g_post_ffn=row(g_post_ffn),
    )


def _ssd_state_to_pairs(s):
    b = s.shape[0]
    s = s.reshape(b, 2, 4, 2, 64, 64).transpose(0, 2, 4, 1, 3, 5)
    return s.reshape(b, 4, 64, 256)


def _ret_state_to_lanes(s):
    b = s.shape[0]
    return s.transpose(0, 2, 3, 1, 4).reshape(b, RET_HEADS, 64, 256)


def _trunk_pass(x2d, mod5, layer, latent, bsz, seq_len, lp, mw, tabs64, tabs_mla, expand, ctx):
    (ssd_in, ret_in, aq, ak, av, mq, ckv, kr) = _inproj(
        x2d, mod5, layer, latent, seq_len, lp['g_pre_mix'], mw['w_ssd'], mw['w_ret'], mw['w_am'],
        lp['qn'], lp['kn'], lp['cn'], (tabs64 + tabs_mla) if latent else None)
    ssd_res = _ssd_mixer(ssd_in, bsz, seq_len, latent, layer, lp['ssd_cw'], lp['ssd_cb'], lp['ssd_dtb'],
                         lp['ssd_alog'], lp['ssd_dsk'], lp['ssd_nw'], expand,
                         ctx['ssd'] if latent else None)
    ret_res = _ret_mixer(ret_in, bsz, seq_len, latent, layer, lp['ret_logit'], lp['ret_gn'],
                         ctx['ret'] if latent else None)
    o_att = _att_mixer(aq, ak, av, bsz, seq_len, latent,
                       ctx['att_k'] if latent else None, ctx['att_v'] if latent else None)
    o_mla = _mla_mixer(mq, ckv, kr, bsz, seq_len, latent,
                       ctx['mla_ckv'] if latent else None, ctx['mla_kr'] if latent else None,
                       lp['mla_qn'], mw['wq'], mw['wk'], mw['wv'], layer, tabs_mla)
    x2d = _merge(x2d, mod5, layer, latent, seq_len, lp['g_pre_mix'], lp['g_post_mix'],
                 (ssd_res[0], ret_res[0], o_att, o_mla), mw['w_merge'], lp['b_merge'], mw['w_br'], mw['w_out'])
    x2d = _ffn(x2d, mod5, layer, latent, seq_len, lp['g_pre_ffn'], lp['g_post_ffn'],
               mw['w_up'], mw['ffn_cw'], mw['ffn_cb'], mw['w_down'])
    cache = None
    if not latent:
        cache = dict(ssd=ssd_res[1], ret=ret_res[1],
                     att_k=ak.reshape(bsz, seq_len, ATT_KV_HEADS, ATT_HEAD_DIM),
                     att_v=av.reshape(bsz, seq_len, ATT_KV_HEADS, ATT_HEAD_DIM),
                     mla_ckv=ckv.reshape(bsz, seq_len, MLA_KV_RANK),
                     mla_krope=kr.reshape(bsz, seq_len, LANES)[:, :, 64:64 + MLA_ROPE])
    return x2d, cache


def kernel(x_prompt, x_sample, state_ssd, state_ret, cache_att_k, cache_att_v, cache_mla_ckv, cache_mla_krope, c, c_ctx, w_mod, b_mod, g_pre_mix, g_post_mix, g_pre_ffn, g_post_ffn, w_in, ssd_conv_w, ssd_conv_b, ssd_dt_bias, ssd_a_log, ssd_d, ssd_norm_w, ret_decay_logit, ret_gn_w, att_q_norm, att_k_norm, mla_q_norm, mla_w_uq, mla_kv_norm, mla_w_ukv, w_br_ssd, w_br_ret, w_br_att, w_br_mla, w_merge, b_merge, w_out, w_ffn_up, ffn_conv_w, ffn_conv_b, w_ffn_down):
    cvec = jnp.concatenate([c_ctx[None, :], c, jnp.zeros((SUBLANES - 1 - DEC_BATCH, D_MODEL), F32)], axis=0)
    mod5 = _modulation(cvec, w_mod, b_mod).reshape(DEPTH, SUBLANES, 6, 1, D_MODEL)

    lane_dims_64 = [d % 64 for d in range(LANES)]
    lane_dims_mla = [d - 64 if 64 <= d < 96 else -1 for d in range(LANES)]
    tabs64 = _rope_tables(DEC_SEQ, ATT_HEAD_DIM, lane_dims_64)
    tabs_mla = _rope_tables(DEC_SEQ, MLA_ROPE, lane_dims_mla)
    expand = _ssd_expand_matrix()

    xp = x_prompt.reshape(BATCH * SEQ, D_MODEL)
    xs = x_sample.reshape(DEC_BATCH * DEC_SEQ, D_MODEL)
    caches = []
    mw = _matmul_weights(w_in, mla_w_uq, mla_w_ukv, w_br_ssd, w_br_ret, w_br_att, w_br_mla, w_merge, w_out,
                         w_ffn_up, ffn_conv_w, ffn_conv_b, w_ffn_down)
    lp = _small_params(ssd_conv_w, ssd_conv_b, ssd_dt_bias, ssd_a_log, ssd_d, ssd_norm_w,
                       ret_decay_logit, ret_gn_w, att_q_norm, att_k_norm, mla_q_norm, mla_kv_norm,
                       b_merge, g_pre_mix, g_post_mix, g_pre_ffn, g_post_ffn)
    for i in range(DEPTH):
        xp, cache = _trunk_pass(xp, mod5, i, False, BATCH, SEQ, lp, mw, tabs64, tabs_mla, expand, None)
        caches.append(cache)
        ctx = dict(ssd=_ssd_state_to_pairs(state_ssd[:, i]), ret=_ret_state_to_lanes(state_ret[:, i]),
                   att_k=cache_att_k[:, i].reshape(DEC_BATCH, PAST_LEN, LANES),
                   att_v=cache_att_v[:, i].reshape(DEC_BATCH, PAST_LEN, LANES),
                   mla_ckv=cache_mla_ckv[:, i],
                   mla_kr=jnp.pad(cache_mla_krope[:, i], ((0, 0), (0, 0), (64, 32))))
        xs, _ = _trunk_pass(xs, mod5, i, True, DEC_BATCH, DEC_SEQ, lp, mw, tabs64, tabs_mla, expand, ctx)

    stack = lambda k: jnp.stack([cc[k] for cc in caches], axis=1)
    return (xp.reshape(BATCH, SEQ, D_MODEL), xs.reshape(DEC_BATCH, DEC_SEQ, D_MODEL),
            stack('ssd'), stack('ret'), stack('att_k'), stack('att_v'), stack('mla_ckv'), stack('mla_krope'))
```

```python
import functools

import numpy as np
import jax
import jax.numpy as jnp
from jax import lax
from jax.experimental import pallas as pl
from jax.experimental.pallas import tpu as pltpu

D_MODEL = 1024
BATCH = 32
SEQ = 256
DEPTH = 2
DEC_BATCH = 4
DEC_SEQ = 2048
PAST_LEN = 256
GRID_W = 64
ROPE_BASE = 10000.0
NORM_EPS = 1e-6
SCAN_CHUNK = 128
SSD_HEADS = 8
SSD_HEAD_DIM = 64
SSD_INNER = 512
SSD_GROUPS = 2
SSD_STATE = 64
SSD_CONV_CH = 768
RET_HEADS = 4
RET_QK = 64
RET_V = 128
ATT_HEADS = 8
ATT_KV_HEADS = 2
ATT_HEAD_DIM = 64
MLA_HEADS = 8
MLA_Q_RANK = 384
MLA_KV_RANK = 256
MLA_NOPE = 64
MLA_ROPE = 32
MLA_V = 64
D_FF = 2816

F32 = jnp.float32
BF16 = jnp.bfloat16
LANES = 128
SUBLANES = 8
VMEM_LIMIT = 56 * 1024 * 1024
NEG_BIG = -1e30

SSD_W = 1408
RET_W = 1536
ATT_MLA_W = 1536
TM_ROWS = 512
SSD_CHUNKS_PER_STEP = 4
SCAN_CHUNKS_PER_STEP = 8
RET_UNROLL = 4
SSD_UNROLL = 4
ATT_SUB_ROWS = 128
FF_HALF = D_FF


def _cparams(n_axes):
    return pltpu.CompilerParams(dimension_semantics=("arbitrary",) * n_axes,
                                vmem_limit_bytes=VMEM_LIMIT)


def _silu(x):
    return x * jax.nn.sigmoid(x)


def _softplus(x):
    return jnp.maximum(x, 0.0) + jnp.log1p(jnp.exp(-jnp.abs(x)))


def _rms(x, w):
    ms = jnp.mean(x * x, axis=-1, keepdims=True)
    return x * lax.rsqrt(ms + NORM_EPS) * w


def _bdot(a, b):
    return jnp.dot(a.astype(BF16), b.astype(BF16), preferred_element_type=F32)


def _bdot_nt(a, b):
    return lax.dot_general(a.astype(BF16), b.astype(BF16), (((1,), (1,)), ((), ())),
                           preferred_element_type=F32)


def _bdot_tn(a, b):
    return lax.dot_general(a.astype(BF16), b.astype(BF16), (((0,), (0,)), ((), ())),
                           preferred_element_type=F32)


def _split_dot(a, b_bf16):
    hi = a.astype(BF16)
    lo = (a - hi.astype(F32)).astype(BF16)
    return (jnp.dot(hi, b_bf16, preferred_element_type=F32)
            + jnp.dot(lo, b_bf16, preferred_element_type=F32))


def _split_dot_left(a_bf16, b):
    hi = b.astype(BF16)
    lo = (b - hi.astype(F32)).astype(BF16)
    return (jnp.dot(a_bf16, hi, preferred_element_type=F32)
            + jnp.dot(a_bf16, lo, preferred_element_type=F32))


def _lane_iota(shape):
    return lax.broadcasted_iota(jnp.int32, shape, len(shape) - 1)


def _row_iota(shape):
    return lax.broadcasted_iota(jnp.int32, shape, 0)


def _rope(xb, cos, sin_up, sin_dn, half):
    return (xb * cos + pltpu.roll(xb, half, 1) * sin_up
            + pltpu.roll(xb, LANES - half, 1) * sin_dn)


def _half_rms(xb, w_row):
    lo = _lane_iota(xb.shape) < 64
    sq = xb * xb
    s_lo = jnp.sum(jnp.where(lo, sq, 0.0), axis=-1, keepdims=True)
    s_hi = jnp.sum(jnp.where(lo, 0.0, sq), axis=-1, keepdims=True)
    ms = jnp.where(lo, s_lo, s_hi) * (1.0 / 64.0)
    return xb * lax.rsqrt(ms + NORM_EPS) * w_row


def _mod_kernel(c_ref, w_ref, b_ref, o_ref):
    c = c_ref[...]
    o_ref[...] = _bdot(_silu(c), w_ref[...]) + b_ref[...]


def _modulation(cvec, w_mod, b_mod):
    tn = 1536
    n_out = 6 * D_MODEL
    return pl.pallas_call(
        _mod_kernel,
        grid=(DEPTH, n_out // tn),
        in_specs=[pl.BlockSpec((SUBLANES, D_MODEL), lambda l, j: (0, 0)),
                  pl.BlockSpec((None, D_MODEL, tn), lambda l, j: (l, 0, j)),
                  pl.BlockSpec((None, 1, tn), lambda l, j: (l, 0, j))],
        out_specs=pl.BlockSpec((None, SUBLANES, tn), lambda l, j: (l, 0, j)),
        out_shape=jax.ShapeDtypeStruct((DEPTH, SUBLANES, n_out), F32),
        compiler_params=_cparams(2),
        name="modulation",
    )(cvec, w_mod, b_mod.reshape(DEPTH, 1, n_out))


def _mod_spec(layer, latent, seq_len, tm, k):
    if latent:
        assert seq_len % tm == 0
    tiles_per_seq = max(seq_len // tm, 1)

    def index_map(i):
        row = 1 + i // tiles_per_seq if latent else 0
        return (layer, row, k, 0, 0)

    return pl.BlockSpec((None, None, None, 1, D_MODEL), index_map)


def _layer_spec(shape2d, layer):
    return pl.BlockSpec((None,) + tuple(shape2d), lambda *_: (layer, 0, 0))


def _resident(block_shape, index_map):
    return pl.BlockSpec(block_shape, index_map, pipeline_mode=pl.Buffered(1))


def _inproj_kernel(latent, *refs):
    if latent:
        (x_ref, sh_ref, sc_ref, g_ref, ws_ref, wr_ref, wa_ref, qn_ref, kn_ref, cn_ref,
         c64_ref, su64_ref, sd64_ref, cm_ref, sum_ref, sdm_ref,
         ssd_ref, ret_ref, aq_ref, ak_ref, av_ref, mq_ref, ckv_ref, kr_ref) = refs
    else:
        (x_ref, sh_ref, sc_ref, g_ref, ws_ref, wr_ref, wa_ref, qn_ref, kn_ref, cn_ref,
         ssd_ref, ret_ref, aq_ref, ak_ref, av_ref, mq_ref, ckv_ref, kr_ref) = refs

    half = x_ref.shape[0] // 2
    for r in range(2):
        rows = slice(r * half, (r + 1) * half)
        h = _rms(x_ref[rows, :], g_ref[...]) * (1.0 + sc_ref[...]) + sh_ref[...]
        hb = h.astype(BF16)

        def proj(w_ref):
            return jnp.dot(hb, w_ref[...], preferred_element_type=F32)

        def rope64(xb):
            if not latent:
                return xb
            return _rope(xb, c64_ref[rows, :], su64_ref[rows, :], sd64_ref[rows, :], 16)

        pa = proj(wa_ref)
        for j in range(4):
            q = _half_rms(pa[:, j * LANES:(j + 1) * LANES], qn_ref[...])
            aq_ref[rows, j * LANES:(j + 1) * LANES] = rope64(q)
        ak_ref[rows, :] = rope64(_half_rms(pa[:, 512:640], kn_ref[...]))
        av_ref[rows, :] = pa[:, 640:768]
        mq_ref[rows, :] = pa[:, 768:768 + MLA_Q_RANK]
        ckv_ref[rows, :] = _rms(pa[:, 768 + MLA_Q_RANK:768 + MLA_Q_RANK + MLA_KV_RANK], cn_ref[...])
        kr = pa[:, 1408:1536]
        if latent:
            kr = _rope(kr, cm_ref[rows, :], sum_ref[rows, :], sdm_ref[rows, :], 8)
        kr_ref[rows, :] = kr

        pr = proj(wr_ref)
        for j in range(2):
            ret_ref[rows, j * LANES:(j + 1) * LANES] = rope64(pr[:, j * LANES:(j + 1) * LANES])
        for j in range(2, 4):
            ret_ref[rows, j * LANES:(j + 1) * LANES] = rope64(pr[:, j * LANES:(j + 1) * LANES] * 0.125)
        ret_ref[rows, 512:RET_W] = pr[:, 512:RET_W]

        ssd_ref[rows, :] = proj(ws_ref)


def _inproj(x2d, mod5, layer, latent, seq_len, g_pre, w_ssd, w_ret, w_am, qn, kn, cn, tabs):
    n = x2d.shape[0]
    tm = TM_ROWS
    tiles_per_seq = max(seq_len // tm, 1)
    row = lambda i: (i, 0)
    lyr3 = lambda i: (layer, 0, 0)
    in_specs = [pl.BlockSpec((tm, D_MODEL), row),
                _mod_spec(layer, latent, seq_len, tm, 0),
                _mod_spec(layer, latent, seq_len, tm, 1),
                _layer_spec((1, D_MODEL), layer),
                _resident((None, D_MODEL, SSD_W), lyr3),
                _resident((None, D_MODEL, RET_W), lyr3),
                _resident((None, D_MODEL, ATT_MLA_W), lyr3),
                _layer_spec((1, LANES), layer),
                _layer_spec((1, LANES), layer),
                _layer_spec((1, MLA_KV_RANK), layer)]
    args = [x2d, mod5, mod5, g_pre, w_ssd, w_ret, w_am, qn, kn, cn]
    if latent:
        tab = pl.BlockSpec((tm, LANES), lambda i: (i % tiles_per_seq, 0))
        in_specs += [tab] * 6
        args += list(tabs)
    widths = (SSD_W, RET_W, 512, LANES, LANES, MLA_Q_RANK, MLA_KV_RANK, LANES)
    return pl.pallas_call(
        functools.partial(_inproj_kernel, latent),
        grid=(n // tm,),
        in_specs=in_specs,
        out_specs=[pl.BlockSpec((tm, w), row) for w in widths],
        out_shape=[jax.ShapeDtypeStruct((n, w), F32) for w in widths],
        compiler_params=_cparams(1),
        name="inproj_lat" if latent else "inproj_ctx",
    )(*args)


def _ssd_kernel(latent, seq_len, *refs):
    if latent:
        (in_ref, cw_ref, cb_ref, dtb_ref, alog_ref, dsk_ref, nw_ref, ex_ref, s0_ref,
         o_ref, xc_ref, dt_ref, cum_ref, sf_ref, sb_ref, cdb_ref, run_ref) = refs
        st_ref = None
    else:
        (in_ref, cw_ref, cb_ref, dtb_ref, alog_ref, dsk_ref, nw_ref, ex_ref,
         o_ref, st_ref, xc_ref, dt_ref, cum_ref, sf_ref, sb_ref, cdb_ref, run_ref) = refs
        s0_ref = None
    nseq = in_ref.shape[0]
    for s in range(nseq):
        _ssd_sequence(latent, seq_len, nseq > 1, in_ref.at[s], cw_ref, cb_ref, dtb_ref, alog_ref, dsk_ref, nw_ref,
                      ex_ref, None if s0_ref is None else s0_ref.at[s], o_ref.at[s],
                      None if st_ref is None else st_ref.at[s], xc_ref.at[s], dt_ref.at[s], cum_ref.at[s],
                      sf_ref.at[s], sb_ref.at[s], cdb_ref.at[s], run_ref.at[s])


def _ssd_sequence(latent, seq_len, unroll_all, in_ref, cw_ref, cb_ref, dtb_ref, alog_ref, dsk_ref, nw_ref, ex_ref,
                  s0_ref, o_ref, st_ref, xc_ref, dt_ref, cum_ref, sf_ref, sb_ref, cdb_ref, run_ref):
    T = SCAN_CHUNK
    nc = seq_len // T
    ii = _row_iota((T, T))
    jj = _lane_iota((T, T))
    tri_lo = (jj <= ii).astype(BF16)
    tri_up = (jj >= ii).astype(BF16)
    lane_c = _lane_iota((T, LANES))
    row_c = _row_iota((T, SSD_CONV_CH))
    fwd_lane = lane_c < SSD_HEADS
    a_row = -jnp.exp(alog_ref[...])
    expand = ex_ref[...]

    if latent:
        run_ref[...] = s0_ref[...]
    else:
        run_ref[...] = jnp.zeros(run_ref.shape, F32)

    def phase0(c, carry):
        r0 = pl.multiple_of(c * T, T)
        x_cur = in_ref[pl.ds(r0, T), 512:1280]
        prev_blk = in_ref[pl.ds(pl.multiple_of(jnp.maximum(r0 - SUBLANES, 0), SUBLANES), SUBLANES), 512:1280]
        next_blk = in_ref[pl.ds(pl.multiple_of(jnp.minimum(r0 + T, seq_len - SUBLANES), SUBLANES), SUBLANES), 512:1280]
        prev_row = jnp.where(c > 0, prev_blk[SUBLANES - 1:SUBLANES, :], 0.0)
        next_row = jnp.where(c < nc - 1, next_blk[0:1, :], 0.0)
        x_prev = jnp.where(row_c == 0, prev_row, pltpu.roll(x_cur, 1, 0))
        x_next = jnp.where(row_c == T - 1, next_row, pltpu.roll(x_cur, T - 1, 0))
        xc = _silu(cb_ref[...] + x_prev * cw_ref[0:1, :] + x_cur * cw_ref[1:2, :] + x_next * cw_ref[2:3, :])
        xc_ref[pl.ds(r0, T), :] = xc

        dt = _softplus(in_ref[pl.ds(r0, T), 1280:SSD_W] + dtb_ref[...])
        la = dt * a_row
        cum = jnp.where(fwd_lane, _split_dot_left(tri_lo, la), _split_dot_left(tri_up, la))
        tot = jnp.where(fwd_lane[0:1], cum[T - 1:T, :], cum[0:1, :])
        dt_ref[pl.ds(r0, T), :] = dt
        cum_ref[pl.ds(r0, T), :] = cum
        w = dt * jnp.exp(tot - cum)
        wcd = jnp.concatenate([w, jnp.broadcast_to(jnp.exp(tot), (SUBLANES, LANES))], axis=0)
        wcd_x = _split_dot(wcd, expand)
        w_x = wcd_x[0:T]
        cd_x = wcd_x[T:T + 1]
        cdb_ref[c] = jnp.broadcast_to(cd_x[:, 512:1024], (SUBLANES, 512))
        xs = xc[:, 0:512]
        bm = xc[:, 512:640].astype(BF16)
        for p in range(4):
            g = p // 2
            sl = slice(p * LANES, (p + 1) * LANES)
            vw = jnp.concatenate([xs[:, sl] * w_x[:, sl], xs[:, sl] * w_x[:, 512 + p * LANES:512 + (p + 1) * LANES]],
                                 axis=1)
            cs = _bdot_tn(bm, vw)[g * 64:(g + 1) * 64]
            s_run = run_ref[p]
            sf_ref[c, p] = s_run[:, 0:LANES]
            sb_ref[c, p] = cs[:, LANES:2 * LANES]
            new_f = s_run[:, 0:LANES] * cd_x[:, sl] + cs[:, 0:LANES]
            run_ref[p] = jnp.concatenate([new_f, s_run[:, LANES:2 * LANES]], axis=1)
        return carry

    lax.fori_loop(0, nc, phase0, 0, unroll=min(nc, SSD_UNROLL))

    def bwd_states(k, carry):
        c = nc - 1 - k
        cd = cdb_ref[c]
        for p in range(4):
            s_run = run_ref[p]
            cs_b = sb_ref[c, p]
            sb_ref[c, p] = s_run[:, LANES:2 * LANES]
            new_b = s_run[:, LANES:2 * LANES] * cd[0:1, p * LANES:(p + 1) * LANES] + cs_b
            run_ref[p] = jnp.concatenate([s_run[:, 0:LANES], new_b], axis=1)
        return carry

    lax.fori_loop(0, nc, bwd_states, 0, unroll=unroll_all)
    if not latent:
        for p in range(4):
            s_fin = run_ref[p]
            for d in range(2):
                for hh in range(2):
                    c0 = d * LANES + hh * 64
                    st_ref[d, 2 * p + hh] = s_fin[:, c0:c0 + 64]

    lower = jj <= ii
    upper = jj >= ii
    zeros64 = jnp.zeros((64, 2 * LANES), F32)

    def phase1(c, carry):
        r0 = pl.multiple_of(c * T, T)
        xc = xc_ref[pl.ds(r0, T), :]
        xs = xc[:, 0:512]
        bm = xc[:, 512:640].astype(BF16)
        cm = xc[:, 640:768]
        dt = dt_ref[pl.ds(r0, T), :]
        cum = cum_ref[pl.ds(r0, T), :]
        cum_t = cum.T
        dt_t = dt.T
        e_x = _split_dot(jnp.exp(cum), expand)
        gmat = [_bdot_nt(jnp.where((lane_c < 64) == (g == 0), cm, 0.0), bm) for g in range(2)]
        cmb = cm.astype(BF16)
        outs = []
        for p in range(4):
            g = p // 2
            sl = slice(p * LANES, (p + 1) * LANES)
            xs_p = xs[:, sl].astype(BF16)
            halves = []
            for h in (2 * p, 2 * p + 1):
                df = jnp.exp(jnp.where(lower, cum[:, h:h + 1] - cum_t[h:h + 1, :], NEG_BIG)) * dt_t[h:h + 1, :]
                hb = SSD_HEADS + h
                db = jnp.exp(jnp.where(upper, cum[:, hb:hb + 1] - cum_t[hb:hb + 1, :], NEG_BIG)) * dt_t[hb:hb + 1, :]
                m_h = (gmat[g] * (df + db)).astype(BF16)
                halves.append(jnp.dot(m_h, xs_p, preferred_element_type=F32))
            o_p = jnp.where(lane_c < 64, halves[0], halves[1])
            s_in = jnp.concatenate([sf_ref[c, p], sb_ref[c, p]], axis=1)
            s_pad = jnp.concatenate([s_in, zeros64] if g == 0 else [zeros64, s_in], axis=0)
            oi = jnp.dot(cmb, s_pad.astype(BF16), preferred_element_type=F32)
            o_p = o_p + oi[:, 0:LANES] * e_x[:, sl] + oi[:, LANES:2 * LANES] * e_x[:, 512 + p * LANES:512 + (p + 1) * LANES]
            outs.append(o_p)
        y = jnp.concatenate(outs, axis=1) + dsk_ref[...] * xs
        z = in_ref[pl.ds(r0, T), 0:512]
        o_ref[pl.ds(r0, T), :] = _rms(y * _silu(z), nw_ref[...])
        return carry

    lax.fori_loop(0, nc, phase1, 0, unroll=unroll_all)


def _ssd_mixer(ssd_in, bsz, seq_len, latent, layer, cw, cb, dtb, alog, dsk, nw, expand, s0):
    nc = seq_len // SCAN_CHUNK
    nseq = max(1, SSD_CHUNKS_PER_STEP // nc)
    assert bsz % nseq == 0
    const2 = lambda b: (0, 0)
    per_b3 = lambda b: (b, 0, 0)
    per_b4 = lambda b: (b, 0, 0, 0)
    in_specs = [pl.BlockSpec((nseq, seq_len, SSD_W), per_b3),
                _layer_spec((3, SSD_CONV_CH), layer),
                _layer_spec((1, SSD_CONV_CH), layer),
                _layer_spec((1, LANES), layer),
                _layer_spec((1, LANES), layer),
                _layer_spec((1, SSD_INNER), layer),
                _layer_spec((1, SSD_INNER), layer),
                pl.BlockSpec((LANES, 1024), const2)]
    args = [ssd_in.reshape(bsz, seq_len, SSD_W), cw, cb, dtb, alog, dsk, nw, expand]
    out_specs = [pl.BlockSpec((nseq, seq_len, SSD_INNER), per_b3)]
    out_shape = [jax.ShapeDtypeStruct((bsz, seq_len, SSD_INNER), F32)]
    if latent:
        in_specs.append(pl.BlockSpec((nseq, 4, 64, 2 * LANES), per_b4))
        args.append(s0)
    else:
        out_specs.append(pl.BlockSpec((nseq, 2, SSD_HEADS, SSD_STATE, SSD_HEAD_DIM), lambda b: (b, 0, 0, 0, 0)))
        out_shape.append(jax.ShapeDtypeStruct((bsz, 2, SSD_HEADS, SSD_STATE, SSD_HEAD_DIM), F32))
    scratch = [pltpu.VMEM((nseq, seq_len, SSD_CONV_CH), F32),
               pltpu.VMEM((nseq, seq_len, LANES), F32),
               pltpu.VMEM((nseq, seq_len, LANES), F32),
               pltpu.VMEM((nseq, nc, 4, 64, LANES), F32),
               pltpu.VMEM((nseq, nc, 4, 64, LANES), F32),
               pltpu.VMEM((nseq, nc, SUBLANES, 512), F32),
               pltpu.VMEM((nseq, 4, 64, 2 * LANES), F32)]
    res = pl.pallas_call(
        functools.partial(_ssd_kernel, latent, seq_len),
        grid=(bsz // nseq,),
        in_specs=in_specs, out_specs=out_specs, out_shape=out_shape,
        scratch_shapes=scratch,
        compiler_params=_cparams(1),
        name="ssd_lat" if latent else "ssd_ctx",
    )(*args)
    return res


def _ret_kernel(latent, seq_len, *refs):
    if latent:
        (in_ref, lg_ref, gn_ref, s0_ref,
         o_ref, dc_ref, we_ref, cd_ref, sf_ref, sb_ref, run_ref) = refs
        st_ref = None
    else:
        (in_ref, lg_ref, gn_ref,
         o_ref, st_ref, dc_ref, we_ref, cd_ref, sf_ref, sb_ref, run_ref) = refs
        s0_ref = None
    T = SCAN_CHUNK

    @pl.when(pl.program_id(0) == 0)
    def _tables():
        la = -_softplus(-lg_ref[...])
        ii = _row_iota((T, T)).astype(F32)
        jj = _lane_iota((T, T)).astype(F32)
        for h in range(RET_HEADS):
            la_f = la[h:h + 1, :]
            la_b = la[RET_HEADS + h:RET_HEADS + h + 1, :]
            dc_ref[h] = (jnp.exp(jnp.where(jj <= ii, (ii - jj) * la_f, NEG_BIG))
                         + jnp.exp(jnp.where(jj >= ii, (jj - ii) * la_b, NEG_BIG)))
            we_ref[h, 0] = jnp.exp((T - 1.0 - ii) * la_f)
            we_ref[h, 1] = jnp.exp(ii * la_b)
            we_ref[h, 2] = jnp.exp((ii + 1.0) * la_f)
            we_ref[h, 3] = jnp.exp((T - ii) * la_b)
            cd_ref[h] = jnp.concatenate([jnp.broadcast_to(jnp.exp(T * la_f), (SUBLANES, LANES)),
                                         jnp.broadcast_to(jnp.exp(T * la_b), (SUBLANES, LANES))], axis=1)

    for s in range(in_ref.shape[0]):
        _ret_sequence(latent, seq_len, in_ref.at[s], gn_ref, None if s0_ref is None else s0_ref.at[s],
                      o_ref.at[s], None if st_ref is None else st_ref.at[s], dc_ref, we_ref, cd_ref,
                      sf_ref.at[s], sb_ref.at[s], run_ref.at[s])


def _ret_sequence(latent, seq_len, in_ref, gn_ref, s0_ref, o_ref, st_ref, dc_ref, we_ref, cd_ref,
                  sf_ref, sb_ref, run_ref):
    T = SCAN_CHUNK
    nc = seq_len // T

    if latent:
        run_ref[...] = s0_ref[...]
    else:
        run_ref[...] = jnp.zeros(run_ref.shape, F32)

    def phase0(c, carry):
        r0 = pl.multiple_of(c * T, T)
        for h in range(RET_HEADS):
            blk = h // 2
            k_blk = in_ref[pl.ds(r0, T), 256 + blk * LANES:256 + (blk + 1) * LANES].astype(BF16)
            v_h = in_ref[pl.ds(r0, T), 512 + h * LANES:512 + (h + 1) * LANES]
            vw = jnp.concatenate([v_h * we_ref[h, 0], v_h * we_ref[h, 1]], axis=1)
            cs = _bdot_tn(k_blk, vw)[(h % 2) * 64:(h % 2 + 1) * 64]
            s_run = run_ref[h]
            sf_ref[c, h] = s_run[:, 0:LANES]
            sb_ref[c, h] = cs[:, LANES:2 * LANES]
            new_f = s_run[:, 0:LANES] * cd_ref[h, 0:1, 0:LANES] + cs[:, 0:LANES]
            run_ref[h] = jnp.concatenate([new_f, s_run[:, LANES:2 * LANES]], axis=1)
        return carry

    unroll = min(nc, RET_UNROLL)
    lax.fori_loop(0, nc, phase0, 0, unroll=unroll)

    def bwd_states(k, carry):
        c = nc - 1 - k
        for h in range(RET_HEADS):
            s_run = run_ref[h]
            cs_b = sb_ref[c, h]
            sb_ref[c, h] = s_run[:, LANES:2 * LANES]
            new_b = s_run[:, LANES:2 * LANES] * cd_ref[h, 0:1, LANES:2 * LANES] + cs_b
            run_ref[h] = jnp.concatenate([s_run[:, 0:LANES], new_b], axis=1)
        return carry

    lax.fori_loop(0, nc, bwd_states, 0, unroll=unroll)
    if not latent:
        for h in range(RET_HEADS):
            for d in range(2):
                st_ref[d, h] = run_ref[h, :, d * LANES:(d + 1) * LANES]

    lane_c = _lane_iota((T, LANES))
    zeros64 = jnp.zeros((64, 2 * LANES), F32)

    def phase1(c, carry):
        r0 = pl.multiple_of(c * T, T)
        for h in range(RET_HEADS):
            blk = h // 2
            q_blk = in_ref[pl.ds(r0, T), blk * LANES:(blk + 1) * LANES]
            k_blk = in_ref[pl.ds(r0, T), 256 + blk * LANES:256 + (blk + 1) * LANES].astype(BF16)
            v_h = in_ref[pl.ds(r0, T), 512 + h * LANES:512 + (h + 1) * LANES].astype(BF16)
            g_h = in_ref[pl.ds(r0, T), 1024 + h * LANES:1024 + (h + 1) * LANES]
            q_m = jnp.where((lane_c < 64) == (h % 2 == 0), q_blk, 0.0)
            m_h = (_bdot_nt(q_m, k_blk) * dc_ref[h]).astype(BF16)
            o = jnp.dot(m_h, v_h, preferred_element_type=F32)
            s_in = jnp.concatenate([sf_ref[c, h], sb_ref[c, h]], axis=1)
            s_pad = jnp.concatenate([s_in, zeros64] if h % 2 == 0 else [zeros64, s_in], axis=0)
            oi = _bdot(q_blk, s_pad)
            o = o + oi[:, 0:LANES] * we_ref[h, 2] + oi[:, LANES:2 * LANES] * we_ref[h, 3]
            oc = o - jnp.mean(o, axis=-1, keepdims=True)
            on = oc * lax.rsqrt(jnp.mean(oc * oc, axis=-1, keepdims=True) + NORM_EPS)
            o_ref[pl.ds(r0, T), h * LANES:(h + 1) * LANES] = (
                on * gn_ref[:, h * LANES:(h + 1) * LANES] * _silu(g_h))
        return carry

    lax.fori_loop(0, nc, phase1, 0, unroll=unroll)


def _ret_mixer(ret_in, bsz, seq_len, latent, layer, logit_rows, gn_w, s0):
    nc = seq_len // SCAN_CHUNK
    T = SCAN_CHUNK
    nseq = max(1, SCAN_CHUNKS_PER_STEP // nc)
    assert bsz % nseq == 0
    per_b3 = lambda b: (b, 0, 0)
    per_b4 = lambda b: (b, 0, 0, 0)
    in_specs = [pl.BlockSpec((nseq, seq_len, RET_W), per_b3),
                _layer_spec((SUBLANES, LANES), layer),
                _layer_spec((1, 512), layer)]
    args = [ret_in.reshape(bsz, seq_len, RET_W), logit_rows, gn_w]
    out_specs = [pl.BlockSpec((nseq, seq_len, 512), per_b3)]
    out_shape = [jax.ShapeDtypeStruct((bsz, seq_len, 512), F32)]
    if latent:
        in_specs.append(pl.BlockSpec((nseq, RET_HEADS, 64, 2 * LANES), per_b4))
        args.append(s0)
    else:
        out_specs.append(pl.BlockSpec((nseq, 2, RET_HEADS, RET_QK, RET_V), lambda b: (b, 0, 0, 0, 0)))
        out_shape.append(jax.ShapeDtypeStruct((bsz, 2, RET_HEADS, RET_QK, RET_V), F32))
    scratch = [pltpu.VMEM((RET_HEADS, T, T), F32),
               pltpu.VMEM((RET_HEADS, 4, T, LANES), F32),
               pltpu.VMEM((RET_HEADS, SUBLANES, 2 * LANES), F32),
               pltpu.VMEM((nseq, nc, RET_HEADS, 64, LANES), F32),
               pltpu.VMEM((nseq, nc, RET_HEADS, 64, LANES), F32),
               pltpu.VMEM((nseq, RET_HEADS, 64, 2 * LANES), F32)]
    return pl.pallas_call(
        functools.partial(_ret_kernel, latent, seq_len),
        grid=(bsz // nseq,),
        in_specs=in_specs, out_specs=out_specs, out_shape=out_shape,
        scratch_shapes=scratch,
        compiler_params=_cparams(1),
        name="ret_lat" if latent else "ret_ctx",
    )(*args)


LOG2_E = 1.4426950408889634


def _softmax_pv(s, v, scale=1.0):
    m = jnp.max(s, axis=-1, keepdims=True)
    p = jnp.exp2((s - m) * (scale * LOG2_E))
    l = jnp.sum(p, axis=-1, keepdims=True)
    return jnp.dot(p.astype(BF16), v, preferred_element_type=F32) / l


def _att_kernel(latent, tq, *refs):
    if latent:
        q_ref, kn_ref, vn_ref, kc_ref, vc_ref, o_ref, kb_ref, vb_ref = refs
    else:
        q_ref, kn_ref, vn_ref, o_ref, kb_ref, vb_ref = refs

    @pl.when(pl.program_id(1) == 0)
    def _fill():
        if latent:
            kb_ref[0:PAST_LEN, :] = kc_ref[...].astype(BF16)
            vb_ref[0:PAST_LEN, :] = vc_ref[...].astype(BF16)
            kb_ref[PAST_LEN:, :] = kn_ref[...].astype(BF16)
            vb_ref[PAST_LEN:, :] = vn_ref[...].astype(BF16)
        else:
            kb_ref[...] = kn_ref[...].astype(BF16)
            vb_ref[...] = vn_ref[...].astype(BF16)

    sub = min(tq, ATT_SUB_ROWS)
    lane_q = _lane_iota((sub, LANES))
    for i in range(tq // sub):
        rows = slice(i * sub, (i + 1) * sub)
        acc = []
        for g in range(ATT_KV_HEADS):
            qs = jnp.concatenate(
                [jnp.where((lane_q < 64) == (g == 0), q_ref[rows, j * LANES:(j + 1) * LANES] * 0.125, 0.0).astype(BF16)
                 for j in range(4)], axis=0)
            s = lax.dot_general(qs, kb_ref[...], (((1,), (1,)), ((), ())), preferred_element_type=F32)
            acc.append(_softmax_pv(s, vb_ref[...]))
        for j in range(4):
            o_ref[rows, j * LANES:(j + 1) * LANES] = jnp.where(lane_q < 64, acc[0][j * sub:(j + 1) * sub],
                                                               acc[1][j * sub:(j + 1) * sub])


def _att_mixer(aq, ak, av, bsz, seq_len, latent, k_ctx, v_ctx):
    tq = 512 if latent else 256
    lk = seq_len + (PAST_LEN if latent else 0)
    qmap = lambda b, i: (b, i, 0)
    bmap = lambda b, i: (b, 0, 0)
    in_specs = [pl.BlockSpec((None, tq, 512), qmap),
                pl.BlockSpec((None, seq_len, LANES), bmap),
                pl.BlockSpec((None, seq_len, LANES), bmap)]
    args = [aq.reshape(bsz, seq_len, 512), ak.reshape(bsz, seq_len, LANES), av.reshape(bsz, seq_len, LANES)]
    if latent:
        in_specs += [pl.BlockSpec((None, PAST_LEN, LANES), bmap)] * 2
        args += [k_ctx, v_ctx]
    return pl.pallas_call(
        functools.partial(_att_kernel, latent, tq),
        grid=(bsz, seq_len // tq),
        in_specs=in_specs,
        out_specs=pl.BlockSpec((None, tq, 512), qmap),
        out_shape=jax.ShapeDtypeStruct((bsz, seq_len, 512), F32),
        scratch_shapes=[pltpu.VMEM((lk, LANES), BF16), pltpu.VMEM((lk, LANES), BF16)],
        compiler_params=_cparams(2),
        name="att_lat" if latent else "att_ctx",
    )(*args)


def _mla_kernel(latent, tq, *refs):
    if latent:
        (mq_ref, cn_ref, rn_ref, cc_ref, rc_ref, qn_ref, wq_ref, wk_ref, wv_ref,
         cos_ref, su_ref, sd_ref, o_ref, kb_ref, vb_ref) = refs
    else:
        (mq_ref, cn_ref, rn_ref, qn_ref, wq_ref, wk_ref, wv_ref, o_ref, kb_ref, vb_ref) = refs

    def fill(r0, ckv, kr):
        cb = ckv.astype(BF16)
        n = ckv.shape[0]
        kn = jnp.dot(cb, wk_ref[...], preferred_element_type=F32)
        for h in range(MLA_HEADS):
            kb_ref[r0:r0 + n, h * LANES:(h + 1) * LANES] = (kn[:, h * LANES:(h + 1) * LANES] + kr).astype(BF16)
        vb_ref[r0:r0 + n, :] = jnp.dot(cb, wv_ref[...], preferred_element_type=F32).astype(BF16)

    @pl.when(pl.program_id(1) == 0)
    def _fill():
        if latent:
            fill(0, cc_ref[...], rc_ref[...])
            fill(PAST_LEN, cn_ref[...], rn_ref[...])
        else:
            fill(0, cn_ref[...], rn_ref[...])

    q = _bdot(_rms(mq_ref[...], qn_ref[...]), wq_ref[...])
    lane_q = _lane_iota((tq, LANES))
    scale = (MLA_NOPE + MLA_ROPE) ** -0.5
    for p in range(4):
        halves = []
        for h in (2 * p, 2 * p + 1):
            qh = q[:, h * LANES:(h + 1) * LANES]
            if latent:
                qh = _rope(qh, cos_ref[...], su_ref[...], sd_ref[...], 8)
            s = _bdot_nt(qh, kb_ref[:, h * LANES:(h + 1) * LANES])
            halves.append(_softmax_pv(s, vb_ref[:, p * LANES:(p + 1) * LANES], scale))
        o_ref[:, p * LANES:(p + 1) * LANES] = jnp.where(lane_q < 64, halves[0], halves[1])


def _mla_mixer(mq, ckv, kr, bsz, seq_len, latent, ckv_ctx, kr_ctx, qn, wq, wk, wv, layer, tabs):
    tq = 512 if latent else 256
    lk = seq_len + (PAST_LEN if latent else 0)
    qmap = lambda b, i: (b, i, 0)
    bmap = lambda b, i: (b, 0, 0)
    const = lambda b, i: (0, 0)
    lyr3 = lambda b, i: (layer, 0, 0)
    in_specs = [pl.BlockSpec((None, tq, MLA_Q_RANK), qmap),
                pl.BlockSpec((None, seq_len, MLA_KV_RANK), bmap),
                pl.BlockSpec((None, seq_len, LANES), bmap)]
    args = [mq.reshape(bsz, seq_len, MLA_Q_RANK), ckv.reshape(bsz, seq_len, MLA_KV_RANK),
            kr.reshape(bsz, seq_len, LANES)]
    if latent:
        in_specs += [pl.BlockSpec((None, PAST_LEN, MLA_KV_RANK), bmap),
                     pl.BlockSpec((None, PAST_LEN, LANES), bmap)]
        args += [ckv_ctx, kr_ctx]
    in_specs += [_layer_spec((1, MLA_Q_RANK), layer),
                 _resident((None, MLA_Q_RANK, 1024), lyr3),
                 _resident((None, MLA_KV_RANK, 1024), lyr3),
                 _resident((None, MLA_KV_RANK, 512), lyr3)]
    args += [qn, wq, wk, wv]
    if latent:
        in_specs += [pl.BlockSpec((tq, LANES), lambda b, i: (i, 0))] * 3
        args += list(tabs)
    return pl.pallas_call(
        functools.partial(_mla_kernel, latent, tq),
        grid=(bsz, seq_len // tq),
        in_specs=in_specs,
        out_specs=pl.BlockSpec((None, tq, 512), qmap),
        out_shape=jax.ShapeDtypeStruct((bsz, seq_len, 512), F32),
        scratch_shapes=[pltpu.VMEM((lk, 1024), BF16), pltpu.VMEM((lk, 512), BF16)],
        compiler_params=_cparams(2),
        name="mla_lat" if latent else "mla_ctx",
    )(*args)


def _merge_kernel(x_ref, sh_ref, sc_ref, ga_ref, gpre_ref, gpost_ref, o1_ref, o2_ref, o3_ref, o4_ref,
                  wm_ref, bm_ref, wb_ref, wo_ref, y_ref):
    x = x_ref[...]
    hb = (_rms(x, gpre_ref[...]) * (1.0 + sc_ref[...]) + sh_ref[...]).astype(BF16)
    merged = None
    for k, o_ref in enumerate((o1_ref, o2_ref, o3_ref, o4_ref)):
        cols = slice(k * D_MODEL, (k + 1) * D_MODEL)
        gate = jax.nn.sigmoid(jnp.dot(hb, wm_ref[:, cols], preferred_element_type=F32) + bm_ref[:, cols])
        term = gate * _bdot(o_ref[...], wb_ref[k])
        merged = term if merged is None else merged + term
    m = _bdot(merged, wo_ref[...])
    y_ref[...] = x + ga_ref[...] * _rms(m, gpost_ref[...])


def _merge(x2d, mod5, layer, latent, seq_len, g_pre, g_post, branches, w_merge, b_merge, w_br, w_out):
    n = x2d.shape[0]
    tm = TM_ROWS
    row = lambda i: (i, 0)
    in_specs = ([pl.BlockSpec((tm, D_MODEL), row)]
                + [_mod_spec(layer, latent, seq_len, tm, k) for k in (0, 1, 2)]
                + [_layer_spec((1, D_MODEL), layer)] * 2
                + [pl.BlockSpec((tm, 512), row)] * 4
                + [_resident((None, D_MODEL, 4 * D_MODEL), lambda i: (layer, 0, 0)),
                   _layer_spec((1, 4 * D_MODEL), layer),
                   _resident((None, 4, 512, D_MODEL), lambda i: (layer, 0, 0, 0)),
                   _resident((None, D_MODEL, D_MODEL), lambda i: (layer, 0, 0))])
    return pl.pallas_call(
        _merge_kernel,
        grid=(n // tm,),
        in_specs=in_specs,
        out_specs=pl.BlockSpec((tm, D_MODEL), row),
        out_shape=jax.ShapeDtypeStruct((n, D_MODEL), F32),
        compiler_params=_cparams(1),
        name="merge_lat" if latent else "merge_ctx",
    )(x2d, mod5, mod5, mod5, g_pre, g_post, *[b.reshape(n, 512) for b in branches],
      w_merge, b_merge, w_br, w_out)


def _ffn_kernel(tm, seq_len, x_ref, xp_ref, xn_ref, sh_ref, sc_ref, gf_ref, gpre_ref, gpost_ref,
                wu_ref, cw_ref, cb_ref, wd_ref, y_ref):
    n_seq = max(tm // seq_len, 1)
    seg = tm // n_seq
    tiles_per_seq = seq_len // tm
    if tiles_per_seq > 1:
        i = pl.program_id(0)
        first = (i % tiles_per_seq) == 0
        last = (i % tiles_per_seq) == tiles_per_seq - 1
    else:
        first = last = True

    def nm(x):
        return _rms(x, gpre_ref[...]) * (1.0 + sc_ref[...]) + sh_ref[...]

    x = x_ref[...]
    h = nm(x)
    zero8 = jnp.zeros((SUBLANES, D_MODEL), F32)
    pieces = [jnp.where(first, 0.0, nm(xp_ref[...]))]
    for s in range(n_seq):
        pieces.append(h[s * seg:(s + 1) * seg])
        pieces.append(zero8 if s < n_seq - 1 else jnp.where(last, 0.0, nm(xn_ref[...])))
    hb = jnp.concatenate(pieces, axis=0).astype(BF16)
    rows = tm + SUBLANES * (n_seq + 1)

    def conv_cols(c0):
        cols = slice(c0, c0 + FF_HALF)
        p = jnp.dot(hb, wu_ref[:, cols], preferred_element_type=F32)
        w = cw_ref[:, cols]
        u = (pltpu.roll(p, 1, 0) * w[0:1] + p * w[1:2] + pltpu.roll(p, rows - 1, 0) * w[2:3])
        segs = [u[SUBLANES + s * (seg + SUBLANES):SUBLANES + s * (seg + SUBLANES) + seg] for s in range(n_seq)]
        return (segs[0] if n_seq == 1 else jnp.concatenate(segs, axis=0)) + cb_ref[:, cols]

    acc = None
    for j in range(D_FF // FF_HALF):
        up = conv_cols(j * FF_HALF)
        gate = conv_cols(D_FF + j * FF_HALF)
        part = _bdot(_silu(gate) * up, wd_ref[j * FF_HALF:(j + 1) * FF_HALF, :])
        acc = part if acc is None else acc + part
    y_ref[...] = x + gf_ref[...] * _rms(acc, gpost_ref[...])


def _ffn(x2d, mod5, layer, latent, seq_len, g_pre, g_post, w_up, conv_w, conv_b, w_down):
    n = x2d.shape[0]
    tm = TM_ROWS
    assert seq_len % tm == 0 or tm % seq_len == 0
    hb = tm // SUBLANES
    n_hb = n // SUBLANES
    row = lambda i: (i, 0)
    lyr3 = lambda i: (layer, 0, 0)
    in_specs = ([pl.BlockSpec((tm, D_MODEL), row),
                 pl.BlockSpec((SUBLANES, D_MODEL), lambda i: (jnp.maximum(i * hb - 1, 0), 0)),
                 pl.BlockSpec((SUBLANES, D_MODEL), lambda i: (jnp.minimum((i + 1) * hb, n_hb - 1), 0))]
                + [_mod_spec(layer, latent, seq_len, tm, k) for k in (3, 4, 5)]
                + [_layer_spec((1, D_MODEL), layer)] * 2
                + [_resident((None, D_MODEL, 2 * D_FF), lyr3),
                   _resident((None, 3, 2 * D_FF), lyr3),
                   _resident((None, 1, 2 * D_FF), lyr3),
                   _resident((None, D_FF, D_MODEL), lyr3)])
    return pl.pallas_call(
        functools.partial(_ffn_kernel, tm, seq_len),
        grid=(n // tm,),
        in_specs=in_specs,
        out_specs=pl.BlockSpec((tm, D_MODEL), row),
        out_shape=jax.ShapeDtypeStruct((n, D_MODEL), F32),
        compiler_params=_cparams(1),
        name="ffn_lat" if latent else "ffn_ctx",
    )(x2d, x2d, x2d, mod5, mod5, mod5, g_pre, g_post, w_up, conv_w, conv_b, w_down)


def _rope_tables(seq_len, head_dim, lane_of_dim):
    d_axis = head_dim // 2
    t = np.arange(seq_len)
    pos = np.stack([(t // GRID_W).astype(np.float32), (t % GRID_W).astype(np.float32)], axis=0)
    inv_freq = (np.float32(ROPE_BASE) ** (-np.arange(0, d_axis, 2, dtype=np.float32) / np.float32(d_axis))).astype(np.float32)
    cos = np.ones((seq_len, LANES), np.float32)
    s_up = np.zeros((seq_len, LANES), np.float32)
    s_dn = np.zeros((seq_len, LANES), np.float32)
    for lane, d in enumerate(lane_of_dim):
        if d < 0:
            continue
        axis, j = divmod(d, d_axis)
        second = j >= d_axis // 2
        ang = (pos[axis] * inv_freq[j % (d_axis // 2)]).astype(np.float32)
        cos[:, lane] = np.cos(ang)
        if second:
            s_up[:, lane] = np.sin(ang)
        else:
            s_dn[:, lane] = -np.sin(ang)
    return jnp.asarray(cos), jnp.asarray(s_up), jnp.asarray(s_dn)


def _ssd_expand_matrix():
    e = np.zeros((LANES, 1024), np.float32)
    for d in range(2):
        for h in range(SSD_HEADS):
            e[d * SSD_HEADS + h, d * 512 + h * 64:d * 512 + (h + 1) * 64] = 1.0
    return jnp.asarray(e, dtype=BF16)


def _matmul_weights(w_in, mla_w_uq, mla_w_ukv, w_br_ssd, w_br_ret, w_br_att, w_br_mla, w_merge, w_out,
                    w_ffn_up, ffn_conv_w, ffn_conv_b, w_ffn_down):
    off = np.cumsum([0, 512, 768, 16, 256, 256, 512, 512, 512, 128, 128, 384, 288])
    col = lambda k: w_in[:, :, off[k]:off[k + 1]]
    zeros = lambda n: jnp.zeros((DEPTH, D_MODEL, n), F32)
    w_ssd = jnp.pad(w_in[:, :, :off[3]], ((0, 0), (0, 0), (0, SSD_W - off[3]))).astype(BF16)
    w_ret = w_in[:, :, off[3]:off[7]].astype(BF16)
    aq = col(7).reshape(DEPTH, D_MODEL, ATT_KV_HEADS, 4, ATT_HEAD_DIM).swapaxes(2, 3).reshape(DEPTH, D_MODEL, 512)
    mckv = col(11)
    w_am = jnp.concatenate(
        [aq, col(8), col(9), col(10), mckv[:, :, :MLA_KV_RANK], zeros(64), mckv[:, :, MLA_KV_RANK:], zeros(32)],
        axis=2).astype(BF16)

    uq = mla_w_uq.reshape(DEPTH, MLA_Q_RANK, MLA_HEADS, MLA_NOPE + MLA_ROPE)
    wq = jnp.pad(uq, ((0, 0), (0, 0), (0, 0), (0, 32))).reshape(DEPTH, MLA_Q_RANK, 1024).astype(BF16)
    ukv = mla_w_ukv.reshape(DEPTH, MLA_KV_RANK, MLA_HEADS, MLA_NOPE + MLA_V)
    wk = jnp.pad(ukv[..., :MLA_NOPE], ((0, 0), (0, 0), (0, 0), (0, 64))).reshape(DEPTH, MLA_KV_RANK, 1024).astype(BF16)
    wv = ukv[..., MLA_NOPE:].reshape(DEPTH, MLA_KV_RANK, 512).astype(BF16)

    w_att = (w_br_att.reshape(DEPTH, ATT_KV_HEADS, 4, ATT_HEAD_DIM, D_MODEL).swapaxes(1, 2)
             .reshape(DEPTH, 512, D_MODEL))
    w_br = jnp.stack([w_br_ssd, w_br_ret, w_att, w_br_mla], axis=1).astype(BF16)

    return dict(
        w_ssd=w_ssd, w_ret=w_ret, w_am=w_am, wq=wq, wk=wk, wv=wv, w_br=w_br,
        w_merge=w_merge.astype(BF16), w_out=w_out.astype(BF16),
        w_up=w_ffn_up.astype(BF16), ffn_cw=ffn_conv_w,
        ffn_cb=ffn_conv_b.reshape(DEPTH, 1, 2 * D_FF),
        w_down=w_ffn_down.astype(BF16),
    )


def _small_params(ssd_conv_w, ssd_conv_b, ssd_dt_bias, ssd_a_log, ssd_d, ssd_norm_w,
                  ret_decay_logit, ret_gn_w, att_q_norm, att_k_norm, mla_q_norm, mla_kv_norm,
                  b_merge, g_pre_mix, g_post_mix, g_pre_ffn, g_post_ffn):
    row = lambda v: v.reshape(DEPTH, 1, -1)
    pad_row = lambda v: jnp.pad(row(v), ((0, 0), (0, 0), (0, LANES - v[0].size)))
    return dict(
        qn=row(jnp.tile(att_q_norm, (1, 2))), kn=row(jnp.tile(att_k_norm, (1, 2))),
        cn=row(mla_kv_norm),
        ssd_cw=ssd_conv_w, ssd_cb=row(ssd_conv_b),
        ssd_dtb=pad_row(ssd_dt_bias), ssd_alog=pad_row(ssd_a_log),
        ssd_dsk=row(jnp.repeat(ssd_d, SSD_HEAD_DIM, axis=1)),
        ssd_nw=row(ssd_norm_w),
        ret_logit=jnp.broadcast_to(ret_decay_logit.reshape(DEPTH, SUBLANES, 1), (DEPTH, SUBLANES, LANES)),
        ret_gn=row(ret_gn_w),
        mla_qn=row(mla_q_norm),
        b_merge=row(b_merge),
        g_pre_mix=row(g_pre_mix), g_post_mix=row(g_post_mix),
        g_pre_ffn=row(g_pre_ffn), g_post_ffn=row(g_post_ffn),
    )


def _ssd_state_to_pairs(s):
    b = s.shape[0]
    s = s.reshape(b, 2, 4, 2, 64, 64).transpose(0, 2, 4, 1, 3, 5)
    return s.reshape(b, 4, 64, 256)


def _ret_state_to_lanes(s):
    b = s.shape[0]
    return s.transpose(0, 2, 3, 1, 4).reshape(b, RET_HEADS, 64, 256)


def _trunk_pass(x2d, mod5, layer, latent, bsz, seq_len, lp, mw, tabs64, tabs_mla, expand, ctx):
    (ssd_in, ret_in, aq, ak, av, mq, ckv, kr) = _inproj(
        x2d, mod5, layer, latent, seq_len, lp['g_pre_mix'], mw['w_ssd'], mw['w_ret'], mw['w_am'],
        lp['qn'], lp['kn'], lp['cn'], (tabs64 + tabs_mla) if latent else None)
    ssd_res = _ssd_mixer(ssd_in, bsz, seq_len, latent, layer, lp['ssd_cw'], lp['ssd_cb'], lp['ssd_dtb'],
                         lp['ssd_alog'], lp['ssd_dsk'], lp['ssd_nw'], expand,
                         ctx['ssd'] if latent else None)
    ret_res = _ret_mixer(ret_in, bsz, seq_len, latent, layer, lp['ret_logit'], lp['ret_gn'],
                         ctx['ret'] if latent else None)
    o_att = _att_mixer(aq, ak, av, bsz, seq_len, latent,
                       ctx['att_k'] if latent else None, ctx['att_v'] if latent else None)
    o_mla = _mla_mixer(mq, ckv, kr, bsz, seq_len, latent,
                       ctx['mla_ckv'] if latent else None, ctx['mla_kr'] if latent else None,
                       lp['mla_qn'], mw['wq'], mw['wk'], mw['wv'], layer, tabs_mla)
    x2d = _merge(x2d, mod5, layer, latent, seq_len, lp['g_pre_mix'], lp['g_post_mix'],
                 (ssd_res[0], ret_res[0], o_att, o_mla), mw['w_merge'], lp['b_merge'], mw['w_br'], mw['w_out'])
    x2d = _ffn(x2d, mod5, layer, latent, seq_len, lp['g_pre_ffn'], lp['g_post_ffn'],
               mw['w_up'], mw['ffn_cw'], mw['ffn_cb'], mw['w_down'])
    cache = None
    if not latent:
        cache = dict(ssd=ssd_res[1], ret=ret_res[1],
                     att_k=ak.reshape(bsz, seq_len, ATT_KV_HEADS, ATT_HEAD_DIM),
                     att_v=av.reshape(bsz, seq_len, ATT_KV_HEADS, ATT_HEAD_DIM),
                     mla_ckv=ckv.reshape(bsz, seq_len, MLA_KV_RANK),
                     mla_krope=kr.reshape(bsz, seq_len, LANES)[:, :, 64:64 + MLA_ROPE])
    return x2d, cache


def kernel(x_prompt, x_sample, state_ssd, state_ret, cache_att_k, cache_att_v, cache_mla_ckv, cache_mla_krope, c, c_ctx, w_mod, b_mod, g_pre_mix, g_post_mix, g_pre_ffn, g_post_ffn, w_in, ssd_conv_w, ssd_conv_b, ssd_dt_bias, ssd_a_log, ssd_d, ssd_norm_w, ret_decay_logit, ret_gn_w, att_q_norm, att_k_norm, mla_q_norm, mla_w_uq, mla_kv_norm, mla_w_ukv, w_br_ssd, w_br_ret, w_br_att, w_br_mla, w_merge, b_merge, w_out, w_ffn_up, ffn_conv_w, ffn_conv_b, w_ffn_down):
    cvec = jnp.concatenate([c_ctx[None, :], c, jnp.zeros((SUBLANES - 1 - DEC_BATCH, D_MODEL), F32)], axis=0)
    mod5 = _modulation(cvec, w_mod, b_mod).reshape(DEPTH, SUBLANES, 6, 1, D_MODEL)

    lane_dims_64 = [d % 64 for d in range(LANES)]
    lane_dims_mla = [d - 64 if 64 <= d < 96 else -1 for d in range(LANES)]
    tabs64 = _rope_tables(DEC_SEQ, ATT_HEAD_DIM, lane_dims_64)
    tabs_mla = _rope_tables(DEC_SEQ, MLA_ROPE, lane_dims_mla)
    expand = _ssd_expand_matrix()

    xp = x_prompt.reshape(BATCH * SEQ, D_MODEL)
    xs = x_sample.reshape(DEC_BATCH * DEC_SEQ, D_MODEL)
    caches = []
    mw = _matmul_weights(w_in, mla_w_uq, mla_w_ukv, w_br_ssd, w_br_ret, w_br_att, w_br_mla, w_merge, w_out,
                         w_ffn_up, ffn_conv_w, ffn_conv_b, w_ffn_down)
    lp = _small_params(ssd_conv_w, ssd_conv_b, ssd_dt_bias, ssd_a_log, ssd_d, ssd_norm_w,
                       ret_decay_logit, ret_gn_w, att_q_norm, att_k_norm, mla_q_norm, mla_kv_norm,
                       b_merge, g_pre_mix, g_post_mix, g_pre_ffn, g_post_ffn)
    for i in range(DEPTH):
        xp, cache = _trunk_pass(xp, mod5, i, False, BATCH, SEQ, lp, mw, tabs64, tabs_mla, expand, None)
        caches.append(cache)
        ctx = dict(ssd=_ssd_state_to_pairs(state_ssd[:, i]), ret=_ret_state_to_lanes(state_ret[:, i]),
                   att_k=cache_att_k[:, i].reshape(DEC_BATCH, PAST_LEN, LANES),
                   att_v=cache_att_v[:, i].reshape(DEC_BATCH, PAST_LEN, LANES),
                   mla_ckv=cache_mla_ckv[:, i],
                   mla_kr=jnp.pad(cache_mla_krope[:, i], ((0, 0), (0, 0), (64, 32))))
        xs, _ = _trunk_pass(xs, mod5, i, True, DEC_BATCH, DEC_SEQ, lp, mw, tabs64, tabs_mla, expand, ctx)

    stack = lambda k: jnp.stack([cc[k] for cc in caches], axis=1)
    return (xp.reshape(BATCH, SEQ, D_MODEL), xs.reshape(DEC_BATCH, DEC_SEQ, D_MODEL),
            stack('ssd'), stack('ret'), stack('att_k'), stack('att_v'), stack('mla_ckv'), stack('mla_krope'))
```

```python
import functools

import numpy as np
import jax
import jax.numpy as jnp
from jax import lax
from jax.experimental import pallas as pl
from jax.experimental.pallas import tpu as pltpu

D_MODEL = 1024
BATCH = 32
SEQ = 256
DEPTH = 2
DEC_BATCH = 4
DEC_SEQ = 2048
PAST_LEN = 256
GRID_W = 64
ROPE_BASE = 10000.0
NORM_EPS = 1e-6
SCAN_CHUNK = 128
SSD_HEADS = 8
SSD_HEAD_DIM = 64
SSD_INNER = 512
SSD_GROUPS = 2
SSD_STATE = 64
SSD_CONV_CH = 768
RET_HEADS = 4
RET_QK = 64
RET_V = 128
ATT_HEADS = 8
ATT_KV_HEADS = 2
ATT_HEAD_DIM = 64
MLA_HEADS = 8
MLA_Q_RANK = 384
MLA_KV_RANK = 256
MLA_NOPE = 64
MLA_ROPE = 32
MLA_V = 64
D_FF = 2816

F32 = jnp.float32
BF16 = jnp.bfloat16
LANES = 128
SUBLANES = 8
VMEM_LIMIT = 56 * 1024 * 1024
NEG_BIG = -1e30

SSD_W = 1408
RET_W = 1536
ATT_MLA_W = 1536
TM_ROWS = 512
SSD_CHUNKS_PER_STEP = 4
SCAN_CHUNKS_PER_STEP = 8
RET_UNROLL = 4
SSD_UNROLL = 4
ATT_CTX_SEQS = 8
MLA_CTX_SEQS = 4
ATT_SUB_ROWS = 128
FF_HALF = D_FF


def _cparams(n_axes):
    return pltpu.CompilerParams(dimension_semantics=("arbitrary",) * n_axes,
                                vmem_limit_bytes=VMEM_LIMIT)


def _silu(x):
    return x * jax.nn.sigmoid(x)


def _softplus(x):
    return jnp.maximum(x, 0.0) + jnp.log1p(jnp.exp(-jnp.abs(x)))


def _rms(x, w):
    ms = jnp.mean(x * x, axis=-1, keepdims=True)
    return x * lax.rsqrt(ms + NORM_EPS) * w


def _bdot(a, b):
    return jnp.dot(a.astype(BF16), b.astype(BF16), preferred_element_type=F32)


def _bdot_nt(a, b):
    return lax.dot_general(a.astype(BF16), b.astype(BF16), (((1,), (1,)), ((), ())),
                           preferred_element_type=F32)


def _bdot_tn(a, b):
    return lax.dot_general(a.astype(BF16), b.astype(BF16), (((0,), (0,)), ((), ())),
                           preferred_element_type=F32)


def _split_dot(a, b_bf16):
    hi = a.astype(BF16)
    lo = (a - hi.astype(F32)).astype(BF16)
    return (jnp.dot(hi, b_bf16, preferred_element_type=F32)
            + jnp.dot(lo, b_bf16, preferred_element_type=F32))


def _split_dot_left(a_bf16, b):
    hi = b.astype(BF16)
    lo = (b - hi.astype(F32)).astype(BF16)
    return (jnp.dot(a_bf16, hi, preferred_element_type=F32)
            + jnp.dot(a_bf16, lo, preferred_element_type=F32))


def _lane_iota(shape):
    return lax.broadcasted_iota(jnp.int32, shape, len(shape) - 1)


def _row_iota(shape):
    return lax.broadcasted_iota(jnp.int32, shape, 0)


def _rope(xb, cos, sin_up, sin_dn, half):
    return (xb * cos + pltpu.roll(xb, half, 1) * sin_up
            + pltpu.roll(xb, LANES - half, 1) * sin_dn)


def _half_rms(xb, w_row):
    lo = _lane_iota(xb.shape) < 64
    sq = xb * xb
    s_lo = jnp.sum(jnp.where(lo, sq, 0.0), axis=-1, keepdims=True)
    s_hi = jnp.sum(jnp.where(lo, 0.0, sq), axis=-1, keepdims=True)
    ms = jnp.where(lo, s_lo, s_hi) * (1.0 / 64.0)
    return xb * lax.rsqrt(ms + NORM_EPS) * w_row


def _mod_kernel(c_ref, w_ref, b_ref, o_ref):
    c = c_ref[...]
    o_ref[...] = _bdot(_silu(c), w_ref[...]) + b_ref[...]


def _modulation(cvec, w_mod, b_mod):
    tn = 1536
    n_out = 6 * D_MODEL
    return pl.pallas_call(
        _mod_kernel,
        grid=(DEPTH, n_out // tn),
        in_specs=[pl.BlockSpec((SUBLANES, D_MODEL), lambda l, j: (0, 0)),
                  pl.BlockSpec((None, D_MODEL, tn), lambda l, j: (l, 0, j)),
                  pl.BlockSpec((None, 1, tn), lambda l, j: (l, 0, j))],
        out_specs=pl.BlockSpec((None, SUBLANES, tn), lambda l, j: (l, 0, j)),
        out_shape=jax.ShapeDtypeStruct((DEPTH, SUBLANES, n_out), F32),
        compiler_params=_cparams(2),
        name="modulation",
    )(cvec, w_mod, b_mod.reshape(DEPTH, 1, n_out))


def _mod_spec(layer, latent, seq_len, tm, k):
    if latent:
        assert seq_len % tm == 0
    tiles_per_seq = max(seq_len // tm, 1)

    def index_map(i):
        row = 1 + i // tiles_per_seq if latent else 0
        return (layer, row, k, 0, 0)

    return pl.BlockSpec((None, None, None, 1, D_MODEL), index_map)


def _layer_spec(shape2d, layer):
    return pl.BlockSpec((None,) + tuple(shape2d), lambda *_: (layer, 0, 0))


def _resident(block_shape, index_map):
    return pl.BlockSpec(block_shape, index_map, pipeline_mode=pl.Buffered(1))


def _inproj_kernel(latent, *refs):
    if latent:
        (x_ref, sh_ref, sc_ref, g_ref, ws_ref, wr_ref, wa_ref, qn_ref, kn_ref, cn_ref,
         c64_ref, su64_ref, sd64_ref, cm_ref, sum_ref, sdm_ref,
         ssd_ref, ret_ref, aq_ref, ak_ref, av_ref, mq_ref, ckv_ref, kr_ref) = refs
    else:
        (x_ref, sh_ref, sc_ref, g_ref, ws_ref, wr_ref, wa_ref, qn_ref, kn_ref, cn_ref,
         ssd_ref, ret_ref, aq_ref, ak_ref, av_ref, mq_ref, ckv_ref, kr_ref) = refs

    half = x_ref.shape[0] // 2
    for r in range(2):
        rows = slice(r * half, (r + 1) * half)
        h = _rms(x_ref[rows, :], g_ref[...]) * (1.0 + sc_ref[...]) + sh_ref[...]
        hb = h.astype(BF16)

        def proj(w_ref):
            return jnp.dot(hb, w_ref[...], preferred_element_type=F32)

        def rope64(xb):
            if not latent:
                return xb
            return _rope(xb, c64_ref[rows, :], su64_ref[rows, :], sd64_ref[rows, :], 16)

        pa = proj(wa_ref)
        for j in range(4):
            q = _half_rms(pa[:, j * LANES:(j + 1) * LANES], qn_ref[...])
            aq_ref[rows, j * LANES:(j + 1) * LANES] = rope64(q)
        ak_ref[rows, :] = rope64(_half_rms(pa[:, 512:640], kn_ref[...]))
        av_ref[rows, :] = pa[:, 640:768]
        mq_ref[rows, :] = pa[:, 768:768 + MLA_Q_RANK]
        ckv_ref[rows, :] = _rms(pa[:, 768 + MLA_Q_RANK:768 + MLA_Q_RANK + MLA_KV_RANK], cn_ref[...])
        kr = pa[:, 1408:1536]
        if latent:
            kr = _rope(kr, cm_ref[rows, :], sum_ref[rows, :], sdm_ref[rows, :], 8)
        kr_ref[rows, :] = kr

        pr = proj(wr_ref)
        for j in range(2):
            ret_ref[rows, j * LANES:(j + 1) * LANES] = rope64(pr[:, j * LANES:(j + 1) * LANES])
        for j in range(2, 4):
            ret_ref[rows, j * LANES:(j + 1) * LANES] = rope64(pr[:, j * LANES:(j + 1) * LANES] * 0.125)
        ret_ref[rows, 512:RET_W] = pr[:, 512:RET_W]

        ssd_ref[rows, :] = proj(ws_ref)


def _inproj(x2d, mod5, layer, latent, seq_len, g_pre, w_ssd, w_ret, w_am, qn, kn, cn, tabs):
    n = x2d.shape[0]
    tm = TM_ROWS
    tiles_per_seq = max(seq_len // tm, 1)
    row = lambda i: (i, 0)
    lyr3 = lambda i: (layer, 0, 0)
    in_specs = [pl.BlockSpec((tm, D_MODEL), row),
                _mod_spec(layer, latent, seq_len, tm, 0),
                _mod_spec(layer, latent, seq_len, tm, 1),
                _layer_spec((1, D_MODEL), layer),
                _resident((None, D_MODEL, SSD_W), lyr3),
                _resident((None, D_MODEL, RET_W), lyr3),
                _resident((None, D_MODEL, ATT_MLA_W), lyr3),
                _layer_spec((1, LANES), layer),
                _layer_spec((1, LANES), layer),
                _layer_spec((1, MLA_KV_RANK), layer)]
    args = [x2d, mod5, mod5, g_pre, w_ssd, w_ret, w_am, qn, kn, cn]
    if latent:
        tab = pl.BlockSpec((tm, LANES), lambda i: (i % tiles_per_seq, 0))
        in_specs += [tab] * 6
        args += list(tabs)
    widths = (SSD_W, RET_W, 512, LANES, LANES, MLA_Q_RANK, MLA_KV_RANK, LANES)
    return pl.pallas_call(
        functools.partial(_inproj_kernel, latent),
        grid=(n // tm,),
        in_specs=in_specs,
        out_specs=[pl.BlockSpec((tm, w), row) for w in widths],
        out_shape=[jax.ShapeDtypeStruct((n, w), F32) for w in widths],
        compiler_params=_cparams(1),
        name="inproj_lat" if latent else "inproj_ctx",
    )(*args)


def _ssd_kernel(latent, seq_len, *refs):
    if latent:
        (in_ref, cw_ref, cb_ref, dtb_ref, alog_ref, dsk_ref, nw_ref, ex_ref, s0_ref,
         o_ref, xc_ref, dt_ref, cum_ref, sf_ref, sb_ref, cdb_ref, run_ref) = refs
        st_ref = None
    else:
        (in_ref, cw_ref, cb_ref, dtb_ref, alog_ref, dsk_ref, nw_ref, ex_ref,
         o_ref, st_ref, xc_ref, dt_ref, cum_ref, sf_ref, sb_ref, cdb_ref, run_ref) = refs
        s0_ref = None
    nseq = in_ref.shape[0]
    for s in range(nseq):
        _ssd_sequence(latent, seq_len, nseq > 1, in_ref.at[s], cw_ref, cb_ref, dtb_ref, alog_ref, dsk_ref, nw_ref,
                      ex_ref, None if s0_ref is None else s0_ref.at[s], o_ref.at[s],
                      None if st_ref is None else st_ref.at[s], xc_ref.at[s], dt_ref.at[s], cum_ref.at[s],
                      sf_ref.at[s], sb_ref.at[s], cdb_ref.at[s], run_ref.at[s])


def _ssd_sequence(latent, seq_len, unroll_all, in_ref, cw_ref, cb_ref, dtb_ref, alog_ref, dsk_ref, nw_ref, ex_ref,
                  s0_ref, o_ref, st_ref, xc_ref, dt_ref, cum_ref, sf_ref, sb_ref, cdb_ref, run_ref):
    T = SCAN_CHUNK
    nc = seq_len // T
    ii = _row_iota((T, T))
    jj = _lane_iota((T, T))
    tri_lo = (jj <= ii).astype(BF16)
    tri_up = (jj >= ii).astype(BF16)
    lane_c = _lane_iota((T, LANES))
    row_c = _row_iota((T, SSD_CONV_CH))
    fwd_lane = lane_c < SSD_HEADS
    a_row = -jnp.exp(alog_ref[...])
    expand = ex_ref[...]

    if latent:
        run_ref[...] = s0_ref[...]
    else:
        run_ref[...] = jnp.zeros(run_ref.shape, F32)

    def phase0(c, carry):
        r0 = pl.multiple_of(c * T, T)
        x_cur = in_ref[pl.ds(r0, T), 512:1280]
        prev_blk = in_ref[pl.ds(pl.multiple_of(jnp.maximum(r0 - SUBLANES, 0), SUBLANES), SUBLANES), 512:1280]
        next_blk = in_ref[pl.ds(pl.multiple_of(jnp.minimum(r0 + T, seq_len - SUBLANES), SUBLANES), SUBLANES), 512:1280]
        prev_row = jnp.where(c > 0, prev_blk[SUBLANES - 1:SUBLANES, :], 0.0)
        next_row = jnp.where(c < nc - 1, next_blk[0:1, :], 0.0)
        x_prev = jnp.where(row_c == 0, prev_row, pltpu.roll(x_cur, 1, 0))
        x_next = jnp.where(row_c == T - 1, next_row, pltpu.roll(x_cur, T - 1, 0))
        xc = _silu(cb_ref[...] + x_prev * cw_ref[0:1, :] + x_cur * cw_ref[1:2, :] + x_next * cw_ref[2:3, :])
        xc_ref[pl.ds(r0, T), :] = xc

        dt = _softplus(in_ref[pl.ds(r0, T), 1280:SSD_W] + dtb_ref[...])
        la = dt * a_row
        cum = jnp.where(fwd_lane, _split_dot_left(tri_lo, la), _split_dot_left(tri_up, la))
        tot = jnp.where(fwd_lane[0:1], cum[T - 1:T, :], cum[0:1, :])
        dt_ref[pl.ds(r0, T), :] = dt
        cum_ref[pl.ds(r0, T), :] = cum
        w = dt * jnp.exp(tot - cum)
        wcd = jnp.concatenate([w, jnp.broadcast_to(jnp.exp(tot), (SUBLANES, LANES))], axis=0)
        wcd_x = _split_dot(wcd, expand)
        w_x = wcd_x[0:T]
        cd_x = wcd_x[T:T + 1]
        cdb_ref[c] = jnp.broadcast_to(cd_x[:, 512:1024], (SUBLANES, 512))
        xs = xc[:, 0:512]
        bm = xc[:, 512:640].astype(BF16)
        for p in range(4):
            g = p // 2
            sl = slice(p * LANES, (p + 1) * LANES)
            vw = jnp.concatenate([xs[:, sl] * w_x[:, sl], xs[:, sl] * w_x[:, 512 + p * LANES:512 + (p + 1) * LANES]],
                                 axis=1)
            cs = _bdot_tn(bm, vw)[g * 64:(g + 1) * 64]
            s_run = run_ref[p]
            sf_ref[c, p] = s_run[:, 0:LANES]
            sb_ref[c, p] = cs[:, LANES:2 * LANES]
            new_f = s_run[:, 0:LANES] * cd_x[:, sl] + cs[:, 0:LANES]
            run_ref[p] = jnp.concatenate([new_f, s_run[:, LANES:2 * LANES]], axis=1)
        return carry

    lax.fori_loop(0, nc, phase0, 0, unroll=min(nc, SSD_UNROLL))

    def bwd_states(k, carry):
        c = nc - 1 - k
        cd = cdb_ref[c]
        for p in range(4):
            s_run = run_ref[p]
            cs_b = sb_ref[c, p]
            sb_ref[c, p] = s_run[:, LANES:2 * LANES]
            new_b = s_run[:, LANES:2 * LANES] * cd[0:1, p * LANES:(p + 1) * LANES] + cs_b
            run_ref[p] = jnp.concatenate([s_run[:, 0:LANES], new_b], axis=1)
        return carry

    lax.fori_loop(0, nc, bwd_states, 0, unroll=unroll_all)
    if not latent:
        for p in range(4):
            s_fin = run_ref[p]
            for d in range(2):
                for hh in range(2):
                    c0 = d * LANES + hh * 64
                    st_ref[d, 2 * p + hh] = s_fin[:, c0:c0 + 64]

    lower = jj <= ii
    upper = jj >= ii
    zeros64 = jnp.zeros((64, 2 * LANES), F32)

    def phase1(c, carry):
        r0 = pl.multiple_of(c * T, T)
        xc = xc_ref[pl.ds(r0, T), :]
        xs = xc[:, 0:512]
        bm = xc[:, 512:640].astype(BF16)
        cm = xc[:, 640:768]
        dt = dt_ref[pl.ds(r0, T), :]
        cum = cum_ref[pl.ds(r0, T), :]
        cum_t = cum.T
        dt_t = dt.T
        e_x = _split_dot(jnp.exp(cum), expand)
        gmat = [_bdot_nt(jnp.where((lane_c < 64) == (g == 0), cm, 0.0), bm) for g in range(2)]
        cmb = cm.astype(BF16)
        outs = []
        for p in range(4):
            g = p // 2
            sl = slice(p * LANES, (p + 1) * LANES)
            xs_p = xs[:, sl].astype(BF16)
            halves = []
            for h in (2 * p, 2 * p + 1):
                df = jnp.exp(jnp.where(lower, cum[:, h:h + 1] - cum_t[h:h + 1, :], NEG_BIG)) * dt_t[h:h + 1, :]
                hb = SSD_HEADS + h
                db = jnp.exp(jnp.where(upper, cum[:, hb:hb + 1] - cum_t[hb:hb + 1, :], NEG_BIG)) * dt_t[hb:hb + 1, :]
                m_h = (gmat[g] * (df + db)).astype(BF16)
                halves.append(jnp.dot(m_h, xs_p, preferred_element_type=F32))
            o_p = jnp.where(lane_c < 64, halves[0], halves[1])
            s_in = jnp.concatenate([sf_ref[c, p], sb_ref[c, p]], axis=1)
            s_pad = jnp.concatenate([s_in, zeros64] if g == 0 else [zeros64, s_in], axis=0)
            oi = jnp.dot(cmb, s_pad.astype(BF16), preferred_element_type=F32)
            o_p = o_p + oi[:, 0:LANES] * e_x[:, sl] + oi[:, LANES:2 * LANES] * e_x[:, 512 + p * LANES:512 + (p + 1) * LANES]
            outs.append(o_p)
        y = jnp.concatenate(outs, axis=1) + dsk_ref[...] * xs
        z = in_ref[pl.ds(r0, T), 0:512]
        o_ref[pl.ds(r0, T), :] = _rms(y * _silu(z), nw_ref[...])
        return carry

    lax.fori_loop(0, nc, phase1, 0, unroll=unroll_all)


def _ssd_mixer(ssd_in, bsz, seq_len, latent, layer, cw, cb, dtb, alog, dsk, nw, expand, s0):
    nc = seq_len // SCAN_CHUNK
    nseq = max(1, SSD_CHUNKS_PER_STEP // nc)
    assert bsz % nseq == 0
    const2 = lambda b: (0, 0)
    per_b3 = lambda b: (b, 0, 0)
    per_b4 = lambda b: (b, 0, 0, 0)
    in_specs = [pl.BlockSpec((nseq, seq_len, SSD_W), per_b3),
                _layer_spec((3, SSD_CONV_CH), layer),
                _layer_spec((1, SSD_CONV_CH), layer),
                _layer_spec((1, LANES), layer),
                _layer_spec((1, LANES), layer),
                _layer_spec((1, SSD_INNER), layer),
                _layer_spec((1, SSD_INNER), layer),
                pl.BlockSpec((LANES, 1024), const2)]
    args = [ssd_in.reshape(bsz, seq_len, SSD_W), cw, cb, dtb, alog, dsk, nw, expand]
    out_specs = [pl.BlockSpec((nseq, seq_len, SSD_INNER), per_b3)]
    out_shape = [jax.ShapeDtypeStruct((bsz, seq_len, SSD_INNER), F32)]
    if latent:
        in_specs.append(pl.BlockSpec((nseq, 4, 64, 2 * LANES), per_b4))
        args.append(s0)
    else:
        out_specs.append(pl.BlockSpec((nseq, 2, SSD_HEADS, SSD_STATE, SSD_HEAD_DIM), lambda b: (b, 0, 0, 0, 0)))
        out_shape.append(jax.ShapeDtypeStruct((bsz, 2, SSD_HEADS, SSD_STATE, SSD_HEAD_DIM), F32))
    scratch = [pltpu.VMEM((nseq, seq_len, SSD_CONV_CH), F32),
               pltpu.VMEM((nseq, seq_len, LANES), F32),
               pltpu.VMEM((nseq, seq_len, LANES), F32),
               pltpu.VMEM((nseq, nc, 4, 64, LANES), F32),
               pltpu.VMEM((nseq, nc, 4, 64, LANES), F32),
               pltpu.VMEM((nseq, nc, SUBLANES, 512), F32),
               pltpu.VMEM((nseq, 4, 64, 2 * LANES), F32)]
    res = pl.pallas_call(
        functools.partial(_ssd_kernel, latent, seq_len),
        grid=(bsz // nseq,),
        in_specs=in_specs, out_specs=out_specs, out_shape=out_shape,
        scratch_shapes=scratch,
        compiler_params=_cparams(1),
        name="ssd_lat" if latent else "ssd_ctx",
    )(*args)
    return res


def _ret_kernel(latent, seq_len, *refs):
    if latent:
        (in_ref, lg_ref, gn_ref, s0_ref,
         o_ref, dc_ref, we_ref, cd_ref, sf_ref, sb_ref, run_ref) = refs
        st_ref = None
    else:
        (in_ref, lg_ref, gn_ref,
         o_ref, st_ref, dc_ref, we_ref, cd_ref, sf_ref, sb_ref, run_ref) = refs
        s0_ref = None
    T = SCAN_CHUNK

    @pl.when(pl.program_id(0) == 0)
    def _tables():
        la = -_softplus(-lg_ref[...])
        ii = _row_iota((T, T)).astype(F32)
        jj = _lane_iota((T, T)).astype(F32)
        for h in range(RET_HEADS):
            la_f = la[h:h + 1, :]
            la_b = la[RET_HEADS + h:RET_HEADS + h + 1, :]
            dc_ref[h] = (jnp.exp(jnp.where(jj <= ii, (ii - jj) * la_f, NEG_BIG))
                         + jnp.exp(jnp.where(jj >= ii, (jj - ii) * la_b, NEG_BIG)))
            we_ref[h, 0] = jnp.exp((T - 1.0 - ii) * la_f)
            we_ref[h, 1] = jnp.exp(ii * la_b)
            we_ref[h, 2] = jnp.exp((ii + 1.0) * la_f)
            we_ref[h, 3] = jnp.exp((T - ii) * la_b)
            cd_ref[h] = jnp.concatenate([jnp.broadcast_to(jnp.exp(T * la_f), (SUBLANES, LANES)),
                                         jnp.broadcast_to(jnp.exp(T * la_b), (SUBLANES, LANES))], axis=1)

    for s in range(in_ref.shape[0]):
        _ret_sequence(latent, seq_len, in_ref.at[s], gn_ref, None if s0_ref is None else s0_ref.at[s],
                      o_ref.at[s], None if st_ref is None else st_ref.at[s], dc_ref, we_ref, cd_ref,
                      sf_ref.at[s], sb_ref.at[s], run_ref.at[s])


def _ret_sequence(latent, seq_len, in_ref, gn_ref, s0_ref, o_ref, st_ref, dc_ref, we_ref, cd_ref,
                  sf_ref, sb_ref, run_ref):
    T = SCAN_CHUNK
    nc = seq_len // T

    if latent:
        run_ref[...] = s0_ref[...]
    else:
        run_ref[...] = jnp.zeros(run_ref.shape, F32)

    def phase0(c, carry):
        r0 = pl.multiple_of(c * T, T)
        for h in range(RET_HEADS):
            blk = h // 2
            k_blk = in_ref[pl.ds(r0, T), 256 + blk * LANES:256 + (blk + 1) * LANES].astype(BF16)
            v_h = in_ref[pl.ds(r0, T), 512 + h * LANES:512 + (h + 1) * LANES]
            vw = jnp.concatenate([v_h * we_ref[h, 0], v_h * we_ref[h, 1]], axis=1)
            cs = _bdot_tn(k_blk, vw)[(h % 2) * 64:(h % 2 + 1) * 64]
            s_run = run_ref[h]
            sf_ref[c, h] = s_run[:, 0:LANES]
            sb_ref[c, h] = cs[:, LANES:2 * LANES]
            new_f = s_run[:, 0:LANES] * cd_ref[h, 0:1, 0:LANES] + cs[:, 0:LANES]
            run_ref[h] = jnp.concatenate([new_f, s_run[:, LANES:2 * LANES]], axis=1)
        return carry

    unroll = min(nc, RET_UNROLL)
    lax.fori_loop(0, nc, phase0, 0, unroll=unroll)

    def bwd_states(k, carry):
        c = nc - 1 - k
        for h in range(RET_HEADS):
            s_run = run_ref[h]
            cs_b = sb_ref[c, h]
            sb_ref[c, h] = s_run[:, LANES:2 * LANES]
            new_b = s_run[:, LANES:2 * LANES] * cd_ref[h, 0:1, LANES:2 * LANES] + cs_b
            run_ref[h] = jnp.concatenate([s_run[:, 0:LANES], new_b], axis=1)
        return carry

    lax.fori_loop(0, nc, bwd_states, 0, unroll=unroll)
    if not latent:
        for h in range(RET_HEADS):
            for d in range(2):
                st_ref[d, h] = run_ref[h, :, d * LANES:(d + 1) * LANES]

    lane_c = _lane_iota((T, LANES))
    zeros64 = jnp.zeros((64, 2 * LANES), F32)

    def phase1(c, carry):
        r0 = pl.multiple_of(c * T, T)
        for h in range(RET_HEADS):
            blk = h // 2
            q_blk = in_ref[pl.ds(r0, T), blk * LANES:(blk + 1) * LANES]
            k_blk = in_ref[pl.ds(r0, T), 256 + blk * LANES:256 + (blk + 1) * LANES].astype(BF16)
            v_h = in_ref[pl.ds(r0, T), 512 + h * LANES:512 + (h + 1) * LANES].astype(BF16)
            g_h = in_ref[pl.ds(r0, T), 1024 + h * LANES:1024 + (h + 1) * LANES]
            q_m = jnp.where((lane_c < 64) == (h % 2 == 0), q_blk, 0.0)
            m_h = (_bdot_nt(q_m, k_blk) * dc_ref[h]).astype(BF16)
            o = jnp.dot(m_h, v_h, preferred_element_type=F32)
            s_in = jnp.concatenate([sf_ref[c, h], sb_ref[c, h]], axis=1)
            s_pad = jnp.concatenate([s_in, zeros64] if h % 2 == 0 else [zeros64, s_in], axis=0)
            oi = _bdot(q_blk, s_pad)
            o = o + oi[:, 0:LANES] * we_ref[h, 2] + oi[:, LANES:2 * LANES] * we_ref[h, 3]
            oc = o - jnp.mean(o, axis=-1, keepdims=True)
            on = oc * lax.rsqrt(jnp.mean(oc * oc, axis=-1, keepdims=True) + NORM_EPS)
            o_ref[pl.ds(r0, T), h * LANES:(h + 1) * LANES] = (
                on * gn_ref[:, h * LANES:(h + 1) * LANES] * _silu(g_h))
        return carry

    lax.fori_loop(0, nc, phase1, 0, unroll=unroll)


def _ret_mixer(ret_in, bsz, seq_len, latent, layer, logit_rows, gn_w, s0):
    nc = seq_len // SCAN_CHUNK
    T = SCAN_CHUNK
    nseq = max(1, SCAN_CHUNKS_PER_STEP // nc)
    assert bsz % nseq == 0
    per_b3 = lambda b: (b, 0, 0)
    per_b4 = lambda b: (b, 0, 0, 0)
    in_specs = [pl.BlockSpec((nseq, seq_len, RET_W), per_b3),
                _layer_spec((SUBLANES, LANES), layer),
                _layer_spec((1, 512), layer)]
    args = [ret_in.reshape(bsz, seq_len, RET_W), logit_rows, gn_w]
    out_specs = [pl.BlockSpec((nseq, seq_len, 512), per_b3)]
    out_shape = [jax.ShapeDtypeStruct((bsz, seq_len, 512), F32)]
    if latent:
        in_specs.append(pl.BlockSpec((nseq, RET_HEADS, 64, 2 * LANES), per_b4))
        args.append(s0)
    else:
        out_specs.append(pl.BlockSpec((nseq, 2, RET_HEADS, RET_QK, RET_V), lambda b: (b, 0, 0, 0, 0)))
        out_shape.append(jax.ShapeDtypeStruct((bsz, 2, RET_HEADS, RET_QK, RET_V), F32))
    scratch = [pltpu.VMEM((RET_HEADS, T, T), F32),
               pltpu.VMEM((RET_HEADS, 4, T, LANES), F32),
               pltpu.VMEM((RET_HEADS, SUBLANES, 2 * LANES), F32),
               pltpu.VMEM((nseq, nc, RET_HEADS, 64, LANES), F32),
               pltpu.VMEM((nseq, nc, RET_HEADS, 64, LANES), F32),
               pltpu.VMEM((nseq, RET_HEADS, 64, 2 * LANES), F32)]
    return pl.pallas_call(
        functools.partial(_ret_kernel, latent, seq_len),
        grid=(bsz // nseq,),
        in_specs=in_specs, out_specs=out_specs, out_shape=out_shape,
        scratch_shapes=scratch,
        compiler_params=_cparams(1),
        name="ret_lat" if latent else "ret_ctx",
    )(*args)


LOG2_E = 1.4426950408889634


def _softmax_pv(s, v, scale=1.0):
    m = jnp.max(s, axis=-1, keepdims=True)
    p = jnp.exp2((s - m) * (scale * LOG2_E))
    l = jnp.sum(p, axis=-1, keepdims=True)
    return jnp.dot(p.astype(BF16), v, preferred_element_type=F32) / l


def _once_per_sequence(fill, single_step):
    if single_step:
        fill()
    else:
        pl.when(pl.program_id(1) == 0)(fill)


def _att_kernel(latent, tq, *refs):
    if latent:
        q_ref, kn_ref, vn_ref, kc_ref, vc_ref, o_ref, kb_ref, vb_ref = refs
    else:
        q_ref, kn_ref, vn_ref, o_ref, kb_ref, vb_ref = refs
        kc_ref = vc_ref = None
    for s in range(q_ref.shape[0]):
        _att_sequence(latent, tq, q_ref.at[s], kn_ref.at[s], vn_ref.at[s],
                      None if kc_ref is None else kc_ref.at[s], None if vc_ref is None else vc_ref.at[s],
                      o_ref.at[s], kb_ref.at[s], vb_ref.at[s])


def _att_sequence(latent, tq, q_ref, kn_ref, vn_ref, kc_ref, vc_ref, o_ref, kb_ref, vb_ref):
    def _fill():
        if latent:
            kb_ref[0:PAST_LEN, :] = kc_ref[...].astype(BF16)
            vb_ref[0:PAST_LEN, :] = vc_ref[...].astype(BF16)
            kb_ref[PAST_LEN:, :] = kn_ref[...].astype(BF16)
            vb_ref[PAST_LEN:, :] = vn_ref[...].astype(BF16)
        else:
            kb_ref[...] = kn_ref[...].astype(BF16)
            vb_ref[...] = vn_ref[...].astype(BF16)

    _once_per_sequence(_fill, single_step=(kn_ref.shape[0] == tq))

    sub = min(tq, ATT_SUB_ROWS)
    lane_q = _lane_iota((sub, LANES))
    for i in range(tq // sub):
        rows = slice(i * sub, (i + 1) * sub)
        acc = []
        for g in range(ATT_KV_HEADS):
            qs = jnp.concatenate(
                [jnp.where((lane_q < 64) == (g == 0), q_ref[rows, j * LANES:(j + 1) * LANES] * 0.125, 0.0).astype(BF16)
                 for j in range(4)], axis=0)
            s = lax.dot_general(qs, kb_ref[...], (((1,), (1,)), ((), ())), preferred_element_type=F32)
            acc.append(_softmax_pv(s, vb_ref[...]))
        for j in range(4):
            o_ref[rows, j * LANES:(j + 1) * LANES] = jnp.where(lane_q < 64, acc[0][j * sub:(j + 1) * sub],
                                                               acc[1][j * sub:(j + 1) * sub])


def _att_mixer(aq, ak, av, bsz, seq_len, latent, k_ctx, v_ctx):
    tq = 512 if latent else 256
    lk = seq_len + (PAST_LEN if latent else 0)
    nseq = 1 if latent else ATT_CTX_SEQS
    assert bsz % nseq == 0
    qmap = lambda b, i: (b, i, 0)
    bmap = lambda b, i: (b, 0, 0)
    in_specs = [pl.BlockSpec((nseq, tq, 512), qmap),
                pl.BlockSpec((nseq, seq_len, LANES), bmap),
                pl.BlockSpec((nseq, seq_len, LANES), bmap)]
    args = [aq.reshape(bsz, seq_len, 512), ak.reshape(bsz, seq_len, LANES), av.reshape(bsz, seq_len, LANES)]
    if latent:
        in_specs += [pl.BlockSpec((nseq, PAST_LEN, LANES), bmap)] * 2
        args += [k_ctx, v_ctx]
    return pl.pallas_call(
        functools.partial(_att_kernel, latent, tq),
        grid=(bsz // nseq, seq_len // tq),
        in_specs=in_specs,
        out_specs=pl.BlockSpec((nseq, tq, 512), qmap),
        out_shape=jax.ShapeDtypeStruct((bsz, seq_len, 512), F32),
        scratch_shapes=[pltpu.VMEM((nseq, lk, LANES), BF16), pltpu.VMEM((nseq, lk, LANES), BF16)],
        compiler_params=_cparams(2),
        name="att_lat" if latent else "att_ctx",
    )(*args)


def _mla_kernel(latent, tq, *refs):
    if latent:
        (mq_ref, cn_ref, rn_ref, cc_ref, rc_ref, qn_ref, wq_ref, wk_ref, wv_ref,
         cos_ref, su_ref, sd_ref, o_ref, kb_ref, vb_ref) = refs
    else:
        (mq_ref, cn_ref, rn_ref, qn_ref, wq_ref, wk_ref, wv_ref, o_ref, kb_ref, vb_ref) = refs
        cc_ref = rc_ref = cos_ref = su_ref = sd_ref = None
    for s in range(mq_ref.shape[0]):
        _mla_sequence(latent, tq, mq_ref.at[s], cn_ref.at[s], rn_ref.at[s],
                      None if cc_ref is None else cc_ref.at[s], None if rc_ref is None else rc_ref.at[s],
                      qn_ref, wq_ref, wk_ref, wv_ref, cos_ref, su_ref, sd_ref,
                      o_ref.at[s], kb_ref.at[s], vb_ref.at[s])


def _mla_sequence(latent, tq, mq_ref, cn_ref, rn_ref, cc_ref, rc_ref, qn_ref, wq_ref, wk_ref, wv_ref,
                  cos_ref, su_ref, sd_ref, o_ref, kb_ref, vb_ref):
    def fill(r0, ckv, kr):
        cb = ckv.astype(BF16)
        n = ckv.shape[0]
        kn = jnp.dot(cb, wk_ref[...], preferred_element_type=F32)
        for h in range(MLA_HEADS):
            kb_ref[r0:r0 + n, h * LANES:(h + 1) * LANES] = (kn[:, h * LANES:(h + 1) * LANES] + kr).astype(BF16)
        vb_ref[r0:r0 + n, :] = jnp.dot(cb, wv_ref[...], preferred_element_type=F32).astype(BF16)

    def _fill():
        if latent:
            fill(0, cc_ref[...], rc_ref[...])
            fill(PAST_LEN, cn_ref[...], rn_ref[...])
        else:
            fill(0, cn_ref[...], rn_ref[...])

    _once_per_sequence(_fill, single_step=(cn_ref.shape[0] == tq))

    q = _bdot(_rms(mq_ref[...], qn_ref[...]), wq_ref[...])
    lane_q = _lane_iota((tq, LANES))
    scale = (MLA_NOPE + MLA_ROPE) ** -0.5
    for p in range(4):
        halves = []
        for h in (2 * p, 2 * p + 1):
            qh = q[:, h * LANES:(h + 1) * LANES]
            if latent:
                qh = _rope(qh, cos_ref[...], su_ref[...], sd_ref[...], 8)
            s = _bdot_nt(qh, kb_ref[:, h * LANES:(h + 1) * LANES])
            halves.append(_softmax_pv(s, vb_ref[:, p * LANES:(p + 1) * LANES], scale))
        o_ref[:, p * LANES:(p + 1) * LANES] = jnp.where(lane_q < 64, halves[0], halves[1])


def _mla_mixer(mq, ckv, kr, bsz, seq_len, latent, ckv_ctx, kr_ctx, qn, wq, wk, wv, layer, tabs):
    tq = 512 if latent else 256
    lk = seq_len + (PAST_LEN if latent else 0)
    nseq = 1 if latent else MLA_CTX_SEQS
    assert bsz % nseq == 0
    qmap = lambda b, i: (b, i, 0)
    bmap = lambda b, i: (b, 0, 0)
    lyr3 = lambda b, i: (layer, 0, 0)
    in_specs = [pl.BlockSpec((nseq, tq, MLA_Q_RANK), qmap),
                pl.BlockSpec((nseq, seq_len, MLA_KV_RANK), bmap),
                pl.BlockSpec((nseq, seq_len, LANES), bmap)]
    args = [mq.reshape(bsz, seq_len, MLA_Q_RANK), ckv.reshape(bsz, seq_len, MLA_KV_RANK),
            kr.reshape(bsz, seq_len, LANES)]
    if latent:
        in_specs += [pl.BlockSpec((nseq, PAST_LEN, MLA_KV_RANK), bmap),
                     pl.BlockSpec((nseq, PAST_LEN, LANES), bmap)]
        args += [ckv_ctx, kr_ctx]
    in_specs += [_layer_spec((1, MLA_Q_RANK), layer),
                 _resident((None, MLA_Q_RANK, 1024), lyr3),
                 _resident((None, MLA_KV_RANK, 1024), lyr3),
                 _resident((None, MLA_KV_RANK, 512), lyr3)]
    args += [qn, wq, wk, wv]
    if latent:
        in_specs += [pl.BlockSpec((tq, LANES), lambda b, i: (i, 0))] * 3
        args += list(tabs)
    return pl.pallas_call(
        functools.partial(_mla_kernel, latent, tq),
        grid=(bsz // nseq, seq_len // tq),
        in_specs=in_specs,
        out_specs=pl.BlockSpec((nseq, tq, 512), qmap),
        out_shape=jax.ShapeDtypeStruct((bsz, seq_len, 512), F32),
        scratch_shapes=[pltpu.VMEM((nseq, lk, 1024), BF16), pltpu.VMEM((nseq, lk, 512), BF16)],
        compiler_params=_cparams(2),
        name="mla_lat" if latent else "mla_ctx",
    )(*args)


def _merge_kernel(x_ref, sh_ref, sc_ref, ga_ref, gpre_ref, gpost_ref, o1_ref, o2_ref, o3_ref, o4_ref,
                  wm_ref, bm_ref, wb_ref, wo_ref, y_ref):
    x = x_ref[...]
    hb = (_rms(x, gpre_ref[...]) * (1.0 + sc_ref[...]) + sh_ref[...]).astype(BF16)
    merged = None
    for k, o_ref in enumerate((o1_ref, o2_ref, o3_ref, o4_ref)):
        cols = slice(k * D_MODEL, (k + 1) * D_MODEL)
        gate = jax.nn.sigmoid(jnp.dot(hb, wm_ref[:, cols], preferred_element_type=F32) + bm_ref[:, cols])
        term = gate * _bdot(o_ref[...], wb_ref[k])
        merged = term if merged is None else merged + term
    m = _bdot(merged, wo_ref[...])
    y_ref[...] = x + ga_ref[...] * _rms(m, gpost_ref[...])


def _merge(x2d, mod5, layer, latent, seq_len, g_pre, g_post, branches, w_merge, b_merge, w_br, w_out):
    n = x2d.shape[0]
    tm = TM_ROWS
    row = lambda i: (i, 0)
    in_specs = ([pl.BlockSpec((tm, D_MODEL), row)]
                + [_mod_spec(layer, latent, seq_len, tm, k) for k in (0, 1, 2)]
                + [_layer_spec((1, D_MODEL), layer)] * 2
                + [pl.BlockSpec((tm, 512), row)] * 4
                + [_resident((None, D_MODEL, 4 * D_MODEL), lambda i: (layer, 0, 0)),
                   _layer_spec((1, 4 * D_MODEL), layer),
                   _resident((None, 4, 512, D_MODEL), lambda i: (layer, 0, 0, 0)),
                   _resident((None, D_MODEL, D_MODEL), lambda i: (layer, 0, 0))])
    return pl.pallas_call(
        _merge_kernel,
        grid=(n // tm,),
        in_specs=in_specs,
        out_specs=pl.BlockSpec((tm, D_MODEL), row),
        out_shape=jax.ShapeDtypeStruct((n, D_MODEL), F32),
        compiler_params=_cparams(1),
        name="merge_lat" if latent else "merge_ctx",
    )(x2d, mod5, mod5, mod5, g_pre, g_post, *[b.reshape(n, 512) for b in branches],
      w_merge, b_merge, w_br, w_out)


def _ffn_kernel(tm, seq_len, x_ref, xp_ref, xn_ref, sh_ref, sc_ref, gf_ref, gpre_ref, gpost_ref,
                wu_ref, cw_ref, cb_ref, wd_ref, y_ref):
    n_seq = max(tm // seq_len, 1)
    seg = tm // n_seq
    tiles_per_seq = seq_len // tm
    if tiles_per_seq > 1:
        i = pl.program_id(0)
        first = (i % tiles_per_seq) == 0
        last = (i % tiles_per_seq) == tiles_per_seq - 1
    else:
        first = last = True

    def nm(x):
        return _rms(x, gpre_ref[...]) * (1.0 + sc_ref[...]) + sh_ref[...]

    x = x_ref[...]
    h = nm(x)
    zero8 = jnp.zeros((SUBLANES, D_MODEL), F32)
    pieces = [jnp.where(first, 0.0, nm(xp_ref[...]))]
    for s in range(n_seq):
        pieces.append(h[s * seg:(s + 1) * seg])
        pieces.append(zero8 if s < n_seq - 1 else jnp.where(last, 0.0, nm(xn_ref[...])))
    hb = jnp.concatenate(pieces, axis=0).astype(BF16)
    rows = tm + SUBLANES * (n_seq + 1)

    def conv_cols(c0):
        cols = slice(c0, c0 + FF_HALF)
        p = jnp.dot(hb, wu_ref[:, cols], preferred_element_type=F32)
        w = cw_ref[:, cols]
        u = (pltpu.roll(p, 1, 0) * w[0:1] + p * w[1:2] + pltpu.roll(p, rows - 1, 0) * w[2:3])
        segs = [u[SUBLANES + s * (seg + SUBLANES):SUBLANES + s * (seg + SUBLANES) + seg] for s in range(n_seq)]
        return (segs[0] if n_seq == 1 else jnp.concatenate(segs, axis=0)) + cb_ref[:, cols]

    acc = None
    for j in range(D_FF // FF_HALF):
        up = conv_cols(j * FF_HALF)
        gate = conv_cols(D_FF + j * FF_HALF)
        part = _bdot(_silu(gate) * up, wd_ref[j * FF_HALF:(j + 1) * FF_HALF, :])
        acc = part if acc is None else acc + part
    y_ref[...] = x + gf_ref[...] * _rms(acc, gpost_ref[...])


def _ffn(x2d, mod5, layer, latent, seq_len, g_pre, g_post, w_up, conv_w, conv_b, w_down):
    n = x2d.shape[0]
    tm = TM_ROWS
    assert seq_len % tm == 0 or tm % seq_len == 0
    hb = tm // SUBLANES
    n_hb = n // SUBLANES
    row = lambda i: (i, 0)
    lyr3 = lambda i: (layer, 0, 0)
    in_specs = ([pl.BlockSpec((tm, D_MODEL), row),
                 pl.BlockSpec((SUBLANES, D_MODEL), lambda i: (jnp.maximum(i * hb - 1, 0), 0)),
                 pl.BlockSpec((SUBLANES, D_MODEL), lambda i: (jnp.minimum((i + 1) * hb, n_hb - 1), 0))]
                + [_mod_spec(layer, latent, seq_len, tm, k) for k in (3, 4, 5)]
                + [_layer_spec((1, D_MODEL), layer)] * 2
                + [_resident((None, D_MODEL, 2 * D_FF), lyr3),
                   _resident((None, 3, 2 * D_FF), lyr3),
                   _resident((None, 1, 2 * D_FF), lyr3),
                   _resident((None, D_FF, D_MODEL), lyr3)])
    return pl.pallas_call(
        functools.partial(_ffn_kernel, tm, seq_len),
        grid=(n // tm,),
        in_specs=in_specs,
        out_specs=pl.BlockSpec((tm, D_MODEL), row),
        out_shape=jax.ShapeDtypeStruct((n, D_MODEL), F32),
        compiler_params=_cparams(1),
        name="ffn_lat" if latent else "ffn_ctx",
    )(x2d, x2d, x2d, mod5, mod5, mod5, g_pre, g_post, w_up, conv_w, conv_b, w_down)


def _rope_tables(seq_len, head_dim, lane_of_dim):
    d_axis = head_dim // 2
    t = np.arange(seq_len)
    pos = np.stack([(t // GRID_W).astype(np.float32), (t % GRID_W).astype(np.float32)], axis=0)
    inv_freq = (np.float32(ROPE_BASE) ** (-np.arange(0, d_axis, 2, dtype=np.float32) / np.float32(d_axis))).astype(np.float32)
    cos = np.ones((seq_len, LANES), np.float32)
    s_up = np.zeros((seq_len, LANES), np.float32)
    s_dn = np.zeros((seq_len, LANES), np.float32)
    for lane, d in enumerate(lane_of_dim):
        if d < 0:
            continue
        axis, j = divmod(d, d_axis)
        second = j >= d_axis // 2
        ang = (pos[axis] * inv_freq[j % (d_axis // 2)]).astype(np.float32)
        cos[:, lane] = np.cos(ang)
        if second:
            s_up[:, lane] = np.sin(ang)
        else:
            s_dn[:, lane] = -np.sin(ang)
    return jnp.asarray(cos), jnp.asarray(s_up), jnp.asarray(s_dn)


def _ssd_expand_matrix():
    e = np.zeros((LANES, 1024), np.float32)
    for d in range(2):
        for h in range(SSD_HEADS):
            e[d * SSD_HEADS + h, d * 512 + h * 64:d * 512 + (h + 1) * 64] = 1.0
    return jnp.asarray(e, dtype=BF16)


def _matmul_weights(w_in, mla_w_uq, mla_w_ukv, w_br_ssd, w_br_ret, w_br_att, w_br_mla, w_merge, w_out,
                    w_ffn_up, ffn_conv_w, ffn_conv_b, w_ffn_down):
    off = np.cumsum([0, 512, 768, 16, 256, 256, 512, 512, 512, 128, 128, 384, 288])
    col = lambda k: w_in[:, :, off[k]:off[k + 1]]
    zeros = lambda n: jnp.zeros((DEPTH, D_MODEL, n), F32)
    w_ssd = jnp.pad(w_in[:, :, :off[3]], ((0, 0), (0, 0), (0, SSD_W - off[3]))).astype(BF16)
    w_ret = w_in[:, :, off[3]:off[7]].astype(BF16)
    aq = col(7).reshape(DEPTH, D_MODEL, ATT_KV_HEADS, 4, ATT_HEAD_DIM).swapaxes(2, 3).reshape(DEPTH, D_MODEL, 512)
    mckv = col(11)
    w_am = jnp.concatenate(
        [aq, col(8), col(9), col(10), mckv[:, :, :MLA_KV_RANK], zeros(64), mckv[:, :, MLA_KV_RANK:], zeros(32)],
        axis=2).astype(BF16)

    uq = mla_w_uq.reshape(DEPTH, MLA_Q_RANK, MLA_HEADS, MLA_NOPE + MLA_ROPE)
    wq = jnp.pad(uq, ((0, 0), (0, 0), (0, 0), (0, 32))).reshape(DEPTH, MLA_Q_RANK, 1024).astype(BF16)
    ukv = mla_w_ukv.reshape(DEPTH, MLA_KV_RANK, MLA_HEADS, MLA_NOPE + MLA_V)
    wk = jnp.pad(ukv[..., :MLA_NOPE], ((0, 0), (0, 0), (0, 0), (0, 64))).reshape(DEPTH, MLA_KV_RANK, 1024).astype(BF16)
    wv = ukv[..., MLA_NOPE:].reshape(DEPTH, MLA_KV_RANK, 512).astype(BF16)

    w_att = (w_br_att.reshape(DEPTH, ATT_KV_HEADS, 4, ATT_HEAD_DIM, D_MODEL).swapaxes(1, 2)
             .reshape(DEPTH, 512, D_MODEL))
    w_br = jnp.stack([w_br_ssd, w_br_ret, w_att, w_br_mla], axis=1).astype(BF16)

    return dict(
        w_ssd=w_ssd, w_ret=w_ret, w_am=w_am, wq=wq, wk=wk, wv=wv, w_br=w_br,
        w_merge=w_merge.astype(BF16), w_out=w_out.astype(BF16),
        w_up=w_ffn_up.astype(BF16), ffn_cw=ffn_conv_w,
        ffn_cb=ffn_conv_b.reshape(DEPTH, 1, 2 * D_FF),
        w_down=w_ffn_down.astype(BF16),
    )


def _small_params(ssd_conv_w, ssd_conv_b, ssd_dt_bias, ssd_a_log, ssd_d, ssd_norm_w,
                  ret_decay_logit, ret_gn_w, att_q_norm, att_k_norm, mla_q_norm, mla_kv_norm,
                  b_merge, g_pre_mix, g_post_mix, g_pre_ffn, g_post_ffn):
    row = lambda v: v.reshape(DEPTH, 1, -1)
    pad_row = lambda v: jnp.pad(row(v), ((0, 0), (0, 0), (0, LANES - v[0].size)))
    return dict(
        qn=row(jnp.tile(att_q_norm, (1, 2))), kn=row(jnp.tile(att_k_norm, (1, 2))),
        cn=row(mla_kv_norm),
        ssd_cw=ssd_conv_w, ssd_cb=row(ssd_conv_b),
        ssd_dtb=pad_row(ssd_dt_bias), ssd_alog=pad_row(ssd_a_log),
        ssd_dsk=row(jnp.repeat(ssd_d, SSD_HEAD_DIM, axis=1)),
        ssd_nw=row(ssd_norm_w),
        ret_logit=jnp.broadcast_to(ret_decay_logit.reshape(DEPTH, SUBLANES, 1), (DEPTH, SUBLANES, LANES)),
        ret_gn=row(ret_gn_w),
        mla_qn=row(mla_q_norm),
        b_merge=row(b_merge),
        g_pre_mix=row(g_pre_mix), g_post_mix=row(g_post_mix),
        g_pre_ffn=row(g_pre_ffn), g_post_ffn=row(g_post_ffn),
    )


def _ssd_state_to_pairs(s):
    b = s.shape[0]
    s = s.reshape(b, 2, 4, 2, 64, 64).transpose(0, 2, 4, 1, 3, 5)
    return s.reshape(b, 4, 64, 256)


def _ret_state_to_lanes(s):
    b = s.shape[0]
    return s.transpose(0, 2, 3, 1, 4).reshape(b, RET_HEADS, 64, 256)


def _trunk_pass(x2d, mod5, layer, latent, bsz, seq_len, lp, mw, tabs64, tabs_mla, expand, ctx):
    (ssd_in, ret_in, aq, ak, av, mq, ckv, kr) = _inproj(
        x2d, mod5, layer, latent, seq_len, lp['g_pre_mix'], mw['w_ssd'], mw['w_ret'], mw['w_am'],
        lp['qn'], lp['kn'], lp['cn'], (tabs64 + tabs_mla) if latent else None)
    ssd_res = _ssd_mixer(ssd_in, bsz, seq_len, latent, layer, lp['ssd_cw'], lp['ssd_cb'], lp['ssd_dtb'],
                         lp['ssd_alog'], lp['ssd_dsk'], lp['ssd_nw'], expand,
                         ctx['ssd'] if latent else None)
    ret_res = _ret_mixer(ret_in, bsz, seq_len, latent, layer, lp['ret_logit'], lp['ret_gn'],
                         ctx['ret'] if latent else None)
    o_att = _att_mixer(aq, ak, av, bsz, seq_len, latent,
                       ctx['att_k'] if latent else None, ctx['att_v'] if latent else None)
    o_mla = _mla_mixer(mq, ckv, kr, bsz, seq_len, latent,
                       ctx['mla_ckv'] if latent else None, ctx['mla_kr'] if latent else None,
                       lp['mla_qn'], mw['wq'], mw['wk'], mw['wv'], layer, tabs_mla)
    x2d = _merge(x2d, mod5, layer, latent, seq_len, lp['g_pre_mix'], lp['g_post_mix'],
                 (ssd_res[0], ret_res[0], o_att, o_mla), mw['w_merge'], lp['b_merge'], mw['w_br'], mw['w_out'])
    x2d = _ffn(x2d, mod5, layer, latent, seq_len, lp['g_pre_ffn'], lp['g_post_ffn'],
               mw['w_up'], mw['ffn_cw'], mw['ffn_cb'], mw['w_down'])
    cache = None
    if not latent:
        cache = dict(ssd=ssd_res[1], ret=ret_res[1],
                     att_k=ak.reshape(bsz, seq_len, ATT_KV_HEADS, ATT_HEAD_DIM),
                     att_v=av.reshape(bsz, seq_len, ATT_KV_HEADS, ATT_HEAD_DIM),
                     mla_ckv=ckv.reshape(bsz, seq_len, MLA_KV_RANK),
                     mla_krope=kr.reshape(bsz, seq_len, LANES)[:, :, 64:64 + MLA_ROPE])
    return x2d, cache


def kernel(x_prompt, x_sample, state_ssd, state_ret, cache_att_k, cache_att_v, cache_mla_ckv, cache_mla_krope, c, c_ctx, w_mod, b_mod, g_pre_mix, g_post_mix, g_pre_ffn, g_post_ffn, w_in, ssd_conv_w, ssd_conv_b, ssd_dt_bias, ssd_a_log, ssd_d, ssd_norm_w, ret_decay_logit, ret_gn_w, att_q_norm, att_k_norm, mla_q_norm, mla_w_uq, mla_kv_norm, mla_w_ukv, w_br_ssd, w_br_ret, w_br_att, w_br_mla, w_merge, b_merge, w_out, w_ffn_up, ffn_conv_w, ffn_conv_b, w_ffn_down):
    cvec = jnp.concatenate([c_ctx[None, :], c, jnp.zeros((SUBLANES - 1 - DEC_BATCH, D_MODEL), F32)], axis=0)
    mod5 = _modulation(cvec, w_mod, b_mod).reshape(DEPTH, SUBLANES, 6, 1, D_MODEL)

    lane_dims_64 = [d % 64 for d in range(LANES)]
    lane_dims_mla = [d - 64 if 64 <= d < 96 else -1 for d in range(LANES)]
    tabs64 = _rope_tables(DEC_SEQ, ATT_HEAD_DIM, lane_dims_64)
    tabs_mla = _rope_tables(DEC_SEQ, MLA_ROPE, lane_dims_mla)
    expand = _ssd_expand_matrix()

    xp = x_prompt.reshape(BATCH * SEQ, D_MODEL)
    xs = x_sample.reshape(DEC_BATCH * DEC_SEQ, D_MODEL)
    caches = []
    mw = _matmul_weights(w_in, mla_w_uq, mla_w_ukv, w_br_ssd, w_br_ret, w_br_att, w_br_mla, w_merge, w_out,
                         w_ffn_up, ffn_conv_w, ffn_conv_b, w_ffn_down)
    lp = _small_params(ssd_conv_w, ssd_conv_b, ssd_dt_bias, ssd_a_log, ssd_d, ssd_norm_w,
                       ret_decay_logit, ret_gn_w, att_q_norm, att_k_norm, mla_q_norm, mla_kv_norm,
                       b_merge, g_pre_mix, g_post_mix, g_pre_ffn, g_post_ffn)
    for i in range(DEPTH):
        xp, cache = _trunk_pass(xp, mod5, i, False, BATCH, SEQ, lp, mw, tabs64, tabs_mla, expand, None)
        caches.append(cache)
        ctx = dict(ssd=_ssd_state_to_pairs(state_ssd[:, i]), ret=_ret_state_to_lanes(state_ret[:, i]),
                   att_k=cache_att_k[:, i].reshape(DEC_BATCH, PAST_LEN, LANES),
                   att_v=cache_att_v[:, i].reshape(DEC_BATCH, PAST_LEN, LANES),
                   mla_ckv=cache_mla_ckv[:, i],
                   mla_kr=jnp.pad(cache_mla_krope[:, i], ((0, 0), (0, 0), (64, 32))))
        xs, _ = _trunk_pass(xs, mod5, i, True, DEC_BATCH, DEC_SEQ, lp, mw, tabs64, tabs_mla, expand, ctx)

    stack = lambda k: jnp.stack([cc[k] for cc in caches], axis=1)
    return (xp.reshape(BATCH, SEQ, D_MODEL), xs.reshape(DEC_BATCH, DEC_SEQ, D_MODEL),
            stack('ssd'), stack('ret'), stack('att_k'), stack('att_v'), stack('mla_ckv'), stack('mla_krope'))
```

```python
import functools

import numpy as np
import jax
import jax.numpy as jnp
from jax import lax
from jax.experimental import pallas as pl
from jax.experimental.pallas import tpu as pltpu

D_MODEL = 1024
BATCH = 32
SEQ = 256
DEPTH = 2
DEC_BATCH = 4
DEC_SEQ = 2048
PAST_LEN = 256
GRID_W = 64
ROPE_BASE = 10000.0
NORM_EPS = 1e-6
SCAN_CHUNK = 128
SSD_HEADS = 8
SSD_HEAD_DIM = 64
SSD_INNER = 512
SSD_GROUPS = 2
SSD_STATE = 64
SSD_CONV_CH = 768
RET_HEADS = 4
RET_QK = 64
RET_V = 128
ATT_HEADS = 8
ATT_KV_HEADS = 2
ATT_HEAD_DIM = 64
MLA_HEADS = 8
MLA_Q_RANK = 384
MLA_KV_RANK = 256
MLA_NOPE = 64
MLA_ROPE = 32
MLA_V = 64
D_FF = 2816

F32 = jnp.float32
BF16 = jnp.bfloat16
LANES = 128
SUBLANES = 8
VMEM_LIMIT = 56 * 1024 * 1024
NEG_BIG = -1e30

SSD_W = 1408
RET_W = 1536
ATT_MLA_W = 1536
TM_ROWS = 512
SSD_CHUNKS_PER_STEP = 4
SCAN_CHUNKS_PER_STEP = 8
RET_UNROLL = 4
SSD_UNROLL = 4
ATT_CTX_SEQS = 8
MLA_CTX_SEQS = 4
ATT_SUB_ROWS = 128
FF_HALF = D_FF


def _cparams(n_axes):
    return pltpu.CompilerParams(dimension_semantics=("arbitrary",) * n_axes,
                                vmem_limit_bytes=VMEM_LIMIT)


def _silu(x):
    return x * jax.nn.sigmoid(x)


def _softplus(x):
    return jnp.maximum(x, 0.0) + jnp.log1p(jnp.exp(-jnp.abs(x)))


def _rms(x, w):
    ms = jnp.mean(x * x, axis=-1, keepdims=True)
    return x * lax.rsqrt(ms + NORM_EPS) * w


def _bdot(a, b):
    return jnp.dot(a.astype(BF16), b.astype(BF16), preferred_element_type=F32)


def _bdot_nt(a, b):
    return lax.dot_general(a.astype(BF16), b.astype(BF16), (((1,), (1,)), ((), ())),
                           preferred_element_type=F32)


def _bdot_tn(a, b):
    return lax.dot_general(a.astype(BF16), b.astype(BF16), (((0,), (0,)), ((), ())),
                           preferred_element_type=F32)


def _split_dot(a, b_bf16):
    hi = a.astype(BF16)
    lo = (a - hi.astype(F32)).astype(BF16)
    return (jnp.dot(hi, b_bf16, preferred_element_type=F32)
            + jnp.dot(lo, b_bf16, preferred_element_type=F32))


def _split_dot_left(a_bf16, b):
    hi = b.astype(BF16)
    lo = (b - hi.astype(F32)).astype(BF16)
    return (jnp.dot(a_bf16, hi, preferred_element_type=F32)
            + jnp.dot(a_bf16, lo, preferred_element_type=F32))


def _lane_iota(shape):
    return lax.broadcasted_iota(jnp.int32, shape, len(shape) - 1)


def _row_iota(shape):
    return lax.broadcasted_iota(jnp.int32, shape, 0)


def _rope(xb, cos, sin_up, sin_dn, half):
    return (xb * cos + pltpu.roll(xb, half, 1) * sin_up
            + pltpu.roll(xb, LANES - half, 1) * sin_dn)


def _half_rms(xb, w_row):
    lo = _lane_iota(xb.shape) < 64
    sq = xb * xb
    s_lo = jnp.sum(jnp.where(lo, sq, 0.0), axis=-1, keepdims=True)
    s_hi = jnp.sum(jnp.where(lo, 0.0, sq), axis=-1, keepdims=True)
    ms = jnp.where(lo, s_lo, s_hi) * (1.0 / 64.0)
    return xb * lax.rsqrt(ms + NORM_EPS) * w_row


def _mod_kernel(c_ref, w_ref, b_ref, o_ref):
    c = c_ref[...]
    o_ref[...] = _bdot(_silu(c), w_ref[...]) + b_ref[...]


def _modulation(cvec, w_mod, b_mod):
    tn = 1536
    n_out = 6 * D_MODEL
    return pl.pallas_call(
        _mod_kernel,
        grid=(DEPTH, n_out // tn),
        in_specs=[pl.BlockSpec((SUBLANES, D_MODEL), lambda l, j: (0, 0)),
                  pl.BlockSpec((None, D_MODEL, tn), lambda l, j: (l, 0, j)),
                  pl.BlockSpec((None, 1, tn), lambda l, j: (l, 0, j))],
        out_specs=pl.BlockSpec((None, SUBLANES, tn), lambda l, j: (l, 0, j)),
        out_shape=jax.ShapeDtypeStruct((DEPTH, SUBLANES, n_out), F32),
        compiler_params=_cparams(2),
        name="modulation",
    )(cvec, w_mod, b_mod.reshape(DEPTH, 1, n_out))


def _mod_spec(layer, latent, seq_len, tm, k):
    if latent:
        assert seq_len % tm == 0
    tiles_per_seq = max(seq_len // tm, 1)

    def index_map(i):
        row = 1 + i // tiles_per_seq if latent else 0
        return (layer, row, k, 0, 0)

    return pl.BlockSpec((None, None, None, 1, D_MODEL), index_map)


def _layer_spec(shape2d, layer):
    return pl.BlockSpec((None,) + tuple(shape2d), lambda *_: (layer, 0, 0))


def _resident(block_shape, index_map):
    return pl.BlockSpec(block_shape, index_map, pipeline_mode=pl.Buffered(1))


def _inproj_kernel(latent, *refs):
    if latent:
        (x_ref, sh_ref, sc_ref, g_ref, ws_ref, wr_ref, wa_ref, qn_ref, kn_ref, cn_ref,
         c64_ref, su64_ref, sd64_ref, cm_ref, sum_ref, sdm_ref,
         ssd_ref, ret_ref, aq_ref, ak_ref, av_ref, mq_ref, ckv_ref, kr_ref) = refs
    else:
        (x_ref, sh_ref, sc_ref, g_ref, ws_ref, wr_ref, wa_ref, qn_ref, kn_ref, cn_ref,
         ssd_ref, ret_ref, aq_ref, ak_ref, av_ref, mq_ref, ckv_ref, kr_ref) = refs

    half = x_ref.shape[0] // 2
    for r in range(2):
        rows = slice(r * half, (r + 1) * half)
        h = _rms(x_ref[rows, :], g_ref[...]) * (1.0 + sc_ref[...]) + sh_ref[...]
        hb = h.astype(BF16)

        def proj(w_ref):
            return jnp.dot(hb, w_ref[...], preferred_element_type=F32)

        def rope64(xb):
            if not latent:
                return xb
            return _rope(xb, c64_ref[rows, :], su64_ref[rows, :], sd64_ref[rows, :], 16)

        pa = proj(wa_ref)
        for j in range(4):
            q = _half_rms(pa[:, j * LANES:(j + 1) * LANES], qn_ref[...])
            aq_ref[rows, j * LANES:(j + 1) * LANES] = rope64(q)
        ak_ref[rows, :] = rope64(_half_rms(pa[:, 512:640], kn_ref[...]))
        av_ref[rows, :] = pa[:, 640:768]
        mq_ref[rows, :] = pa[:, 768:768 + MLA_Q_RANK]
        ckv_ref[rows, :] = _rms(pa[:, 768 + MLA_Q_RANK:768 + MLA_Q_RANK + MLA_KV_RANK], cn_ref[...])
        kr = pa[:, 1408:1536]
        if latent:
            kr = _rope(kr, cm_ref[rows, :], sum_ref[rows, :], sdm_ref[rows, :], 8)
        kr_ref[rows, :] = kr

        pr = proj(wr_ref)
        for j in range(2):
            ret_ref[rows, j * LANES:(j + 1) * LANES] = rope64(pr[:, j * LANES:(j + 1) * LANES])
        for j in range(2, 4):
            ret_ref[rows, j * LANES:(j + 1) * LANES] = rope64(pr[:, j * LANES:(j + 1) * LANES] * 0.125)
        ret_ref[rows, 512:RET_W] = pr[:, 512:RET_W]

        ssd_ref[rows, :] = proj(ws_ref)


def _inproj(x2d, mod5, layer, latent, seq_len, g_pre, w_ssd, w_ret, w_am, qn, kn, cn, tabs):
    n = x2d.shape[0]
    tm = TM_ROWS
    tiles_per_seq = max(seq_len // tm, 1)
    row = lambda i: (i, 0)
    lyr3 = lambda i: (layer, 0, 0)
    in_specs = [pl.BlockSpec((tm, D_MODEL), row),
                _mod_spec(layer, latent, seq_len, tm, 0),
                _mod_spec(layer, latent, seq_len, tm, 1),
                _layer_spec((1, D_MODEL), layer),
                _resident((None, D_MODEL, SSD_W), lyr3),
                _resident((None, D_MODEL, RET_W), lyr3),
                _resident((None, D_MODEL, ATT_MLA_W), lyr3),
                _layer_spec((1, LANES), layer),
                _layer_spec((1, LANES), layer),
                _layer_spec((1, MLA_KV_RANK), layer)]
    args = [x2d, mod5, mod5, g_pre, w_ssd, w_ret, w_am, qn, kn, cn]
    if latent:
        tab = pl.BlockSpec((tm, LANES), lambda i: (i % tiles_per_seq, 0))
        in_specs += [tab] * 6
        args += list(tabs)
    widths = (SSD_W, RET_W, 512, LANES, LANES, MLA_Q_RANK, MLA_KV_RANK, LANES)
    return pl.pallas_call(
        functools.partial(_inproj_kernel, latent),
        grid=(n // tm,),
        in_specs=in_specs,
        out_specs=[pl.BlockSpec((tm, w), row) for w in widths],
        out_shape=[jax.ShapeDtypeStruct((n, w), F32) for w in widths],
        compiler_params=_cparams(1),
        name="inproj_lat" if latent else "inproj_ctx",
    )(*args)


def _ssd_kernel(latent, seq_len, n_prev, *refs):
    if latent:
        (in_ref, cw_ref, cb_ref, dtb_ref, alog_ref, dsk_ref, nw_ref, ex_ref, s0_ref,
         o_ref, xc_ref, dt_ref, cum_ref, sf_ref, sb_ref, cdb_ref, run_ref) = refs
        st_ref = prev_ref = None
    else:
        (in_ref, cw_ref, cb_ref, dtb_ref, alog_ref, dsk_ref, nw_ref, ex_ref) = refs[:8]
        prev_ref = refs[8] if n_prev else None
        (o_ref, st_ref, xc_ref, dt_ref, cum_ref, sf_ref, sb_ref, cdb_ref, run_ref) = refs[8 + bool(n_prev):]
        s0_ref = None
    nseq = in_ref.shape[0]
    for s in range(nseq):
        if prev_ref is not None:
            st_ref[s, 0:n_prev] = prev_ref[s]
        _ssd_sequence(latent, seq_len, nseq > 1, in_ref.at[s], cw_ref, cb_ref, dtb_ref, alog_ref, dsk_ref, nw_ref,
                      ex_ref, None if s0_ref is None else s0_ref.at[s], o_ref.at[s],
                      None if st_ref is None else st_ref.at[s, n_prev], xc_ref.at[s], dt_ref.at[s], cum_ref.at[s],
                      sf_ref.at[s], sb_ref.at[s], cdb_ref.at[s], run_ref.at[s])


def _ssd_sequence(latent, seq_len, unroll_all, in_ref, cw_ref, cb_ref, dtb_ref, alog_ref, dsk_ref, nw_ref, ex_ref,
                  s0_ref, o_ref, st_ref, xc_ref, dt_ref, cum_ref, sf_ref, sb_ref, cdb_ref, run_ref):
    T = SCAN_CHUNK
    nc = seq_len // T
    ii = _row_iota((T, T))
    jj = _lane_iota((T, T))
    tri_lo = (jj <= ii).astype(BF16)
    tri_up = (jj >= ii).astype(BF16)
    lane_c = _lane_iota((T, LANES))
    row_c = _row_iota((T, SSD_CONV_CH))
    fwd_lane = lane_c < SSD_HEADS
    a_row = -jnp.exp(alog_ref[...])
    expand = ex_ref[...]

    if latent:
        run_ref[...] = s0_ref[...]
    else:
        run_ref[...] = jnp.zeros(run_ref.shape, F32)

    def phase0(c, carry):
        r0 = pl.multiple_of(c * T, T)
        x_cur = in_ref[pl.ds(r0, T), 512:1280]
        prev_blk = in_ref[pl.ds(pl.multiple_of(jnp.maximum(r0 - SUBLANES, 0), SUBLANES), SUBLANES), 512:1280]
        next_blk = in_ref[pl.ds(pl.multiple_of(jnp.minimum(r0 + T, seq_len - SUBLANES), SUBLANES), SUBLANES), 512:1280]
        prev_row = jnp.where(c > 0, prev_blk[SUBLANES - 1:SUBLANES, :], 0.0)
        next_row = jnp.where(c < nc - 1, next_blk[0:1, :], 0.0)
        x_prev = jnp.where(row_c == 0, prev_row, pltpu.roll(x_cur, 1, 0))
        x_next = jnp.where(row_c == T - 1, next_row, pltpu.roll(x_cur, T - 1, 0))
        xc = _silu(cb_ref[...] + x_prev * cw_ref[0:1, :] + x_cur * cw_ref[1:2, :] + x_next * cw_ref[2:3, :])
        xc_ref[pl.ds(r0, T), :] = xc

        dt = _softplus(in_ref[pl.ds(r0, T), 1280:SSD_W] + dtb_ref[...])
        la = dt * a_row
        cum = jnp.where(fwd_lane, _split_dot_left(tri_lo, la), _split_dot_left(tri_up, la))
        tot = jnp.where(fwd_lane[0:1], cum[T - 1:T, :], cum[0:1, :])
        dt_ref[pl.ds(r0, T), :] = dt
        cum_ref[pl.ds(r0, T), :] = cum
        w = dt * jnp.exp(tot - cum)
        wcd = jnp.concatenate([w, jnp.broadcast_to(jnp.exp(tot), (SUBLANES, LANES))], axis=0)
        wcd_x = _split_dot(wcd, expand)
        w_x = wcd_x[0:T]
        cd_x = wcd_x[T:T + 1]
        cdb_ref[c] = jnp.broadcast_to(cd_x[:, 512:1024], (SUBLANES, 512))
        xs = xc[:, 0:512]
        bm = xc[:, 512:640].astype(BF16)
        for p in range(4):
            g = p // 2
            sl = slice(p * LANES, (p + 1) * LANES)
            vw = jnp.concatenate([xs[:, sl] * w_x[:, sl], xs[:, sl] * w_x[:, 512 + p * LANES:512 + (p + 1) * LANES]],
                                 axis=1)
            cs = _bdot_tn(bm, vw)[g * 64:(g + 1) * 64]
            s_run = run_ref[p]
            sf_ref[c, p] = s_run[:, 0:LANES]
            sb_ref[c, p] = cs[:, LANES:2 * LANES]
            new_f = s_run[:, 0:LANES] * cd_x[:, sl] + cs[:, 0:LANES]
            run_ref[p] = jnp.concatenate([new_f, s_run[:, LANES:2 * LANES]], axis=1)
        return carry

    lax.fori_loop(0, nc, phase0, 0, unroll=min(nc, SSD_UNROLL))

    def bwd_states(k, carry):
        c = nc - 1 - k
        cd = cdb_ref[c]
        for p in range(4):
            s_run = run_ref[p]
            cs_b = sb_ref[c, p]
            sb_ref[c, p] = s_run[:, LANES:2 * LANES]
            new_b = s_run[:, LANES:2 * LANES] * cd[0:1, p * LANES:(p + 1) * LANES] + cs_b
            run_ref[p] = jnp.concatenate([s_run[:, 0:LANES], new_b], axis=1)
        return carry

    lax.fori_loop(0, nc, bwd_states, 0, unroll=unroll_all)
    if not latent:
        for p in range(4):
            s_fin = run_ref[p]
            for d in range(2):
                for hh in range(2):
                    c0 = d * LANES + hh * 64
                    st_ref[d, 2 * p + hh] = s_fin[:, c0:c0 + 64]

    lower = jj <= ii
    upper = jj >= ii
    zeros64 = jnp.zeros((64, 2 * LANES), F32)

    def phase1(c, carry):
        r0 = pl.multiple_of(c * T, T)
        xc = xc_ref[pl.ds(r0, T), :]
        xs = xc[:, 0:512]
        bm = xc[:, 512:640].astype(BF16)
        cm = xc[:, 640:768]
        dt = dt_ref[pl.ds(r0, T), :]
        cum = cum_ref[pl.ds(r0, T), :]
        cum_t = cum.T
        dt_t = dt.T
        e_x = _split_dot(jnp.exp(cum), expand)
        gmat = [_bdot_nt(jnp.where((lane_c < 64) == (g == 0), cm, 0.0), bm) for g in range(2)]
        cmb = cm.astype(BF16)
        outs = []
        for p in range(4):
            g = p // 2
            sl = slice(p * LANES, (p + 1) * LANES)
            xs_p = xs[:, sl].astype(BF16)
            halves = []
            for h in (2 * p, 2 * p + 1):
                df = jnp.exp(jnp.where(lower, cum[:, h:h + 1] - cum_t[h:h + 1, :], NEG_BIG)) * dt_t[h:h + 1, :]
                hb = SSD_HEADS + h
                db = jnp.exp(jnp.where(upper, cum[:, hb:hb + 1] - cum_t[hb:hb + 1, :], NEG_BIG)) * dt_t[hb:hb + 1, :]
                m_h = (gmat[g] * (df + db)).astype(BF16)
                halves.append(jnp.dot(m_h, xs_p, preferred_element_type=F32))
            o_p = jnp.where(lane_c < 64, halves[0], halves[1])
            s_in = jnp.concatenate([sf_ref[c, p], sb_ref[c, p]], axis=1)
            s_pad = jnp.concatenate([s_in, zeros64] if g == 0 else [zeros64, s_in], axis=0)
            oi = jnp.dot(cmb, s_pad.astype(BF16), preferred_element_type=F32)
            o_p = o_p + oi[:, 0:LANES] * e_x[:, sl] + oi[:, LANES:2 * LANES] * e_x[:, 512 + p * LANES:512 + (p + 1) * LANES]
            outs.append(o_p)
        y = jnp.concatenate(outs, axis=1) + dsk_ref[...] * xs
        z = in_ref[pl.ds(r0, T), 0:512]
        o_ref[pl.ds(r0, T), :] = _rms(y * _silu(z), nw_ref[...])
        return carry

    lax.fori_loop(0, nc, phase1, 0, unroll=unroll_all)


def _ssd_mixer(ssd_in, bsz, seq_len, latent, layer, cw, cb, dtb, alog, dsk, nw, expand, s0, prev_states):
    nc = seq_len // SCAN_CHUNK
    n_prev = 0 if prev_states is None else prev_states.shape[1]
    nseq = max(1, SSD_CHUNKS_PER_STEP // nc)
    assert bsz % nseq == 0
    const2 = lambda b: (0, 0)
    per_b3 = lambda b: (b, 0, 0)
    per_b4 = lambda b: (b, 0, 0, 0)
    in_specs = [pl.BlockSpec((nseq, seq_len, SSD_W), per_b3),
                _layer_spec((3, SSD_CONV_CH), layer),
                _layer_spec((1, SSD_CONV_CH), layer),
                _layer_spec((1, LANES), layer),
                _layer_spec((1, LANES), layer),
                _layer_spec((1, SSD_INNER), layer),
                _layer_spec((1, SSD_INNER), layer),
                pl.BlockSpec((LANES, 1024), const2)]
    args = [ssd_in.reshape(bsz, seq_len, SSD_W), cw, cb, dtb, alog, dsk, nw, expand]
    out_specs = [pl.BlockSpec((nseq, seq_len, SSD_INNER), per_b3)]
    out_shape = [jax.ShapeDtypeStruct((bsz, seq_len, SSD_INNER), F32)]
    if latent:
        in_specs.append(pl.BlockSpec((nseq, 4, 64, 2 * LANES), per_b4))
        args.append(s0)
    else:
        st_tail = (2, SSD_HEADS, SSD_STATE, SSD_HEAD_DIM)
        per_b6 = lambda b: (b, 0, 0, 0, 0, 0)
        if n_prev:
            in_specs.append(pl.BlockSpec((nseq, n_prev) + st_tail, per_b6))
            args.append(prev_states)
        out_specs.append(pl.BlockSpec((nseq, n_prev + 1) + st_tail, per_b6))
        out_shape.append(jax.ShapeDtypeStruct((bsz, n_prev + 1) + st_tail, F32))
    scratch = [pltpu.VMEM((nseq, seq_len, SSD_CONV_CH), F32),
               pltpu.VMEM((nseq, seq_len, LANES), F32),
               pltpu.VMEM((nseq, seq_len, LANES), F32),
               pltpu.VMEM((nseq, nc, 4, 64, LANES), F32),
               pltpu.VMEM((nseq, nc, 4, 64, LANES), F32),
               pltpu.VMEM((nseq, nc, SUBLANES, 512), F32),
               pltpu.VMEM((nseq, 4, 64, 2 * LANES), F32)]
    res = pl.pallas_call(
        functools.partial(_ssd_kernel, latent, seq_len, n_prev),
        grid=(bsz // nseq,),
        in_specs=in_specs, out_specs=out_specs, out_shape=out_shape,
        scratch_shapes=scratch,
        compiler_params=_cparams(1),
        name="ssd_lat" if latent else "ssd_ctx",
    )(*args)
    return res


def _ret_kernel(latent, seq_len, n_prev, *refs):
    if latent:
        (in_ref, lg_ref, gn_ref, s0_ref,
         o_ref, dc_ref, we_ref, cd_ref, sf_ref, sb_ref, run_ref) = refs
        st_ref = prev_ref = None
    else:
        (in_ref, lg_ref, gn_ref) = refs[:3]
        prev_ref = refs[3] if n_prev else None
        (o_ref, st_ref, dc_ref, we_ref, cd_ref, sf_ref, sb_ref, run_ref) = refs[3 + bool(n_prev):]
        s0_ref = None
    T = SCAN_CHUNK

    @pl.when(pl.program_id(0) == 0)
    def _tables():
        la = -_softplus(-lg_ref[...])
        ii = _row_iota((T, T)).astype(F32)
        jj = _lane_iota((T, T)).astype(F32)
        for h in range(RET_HEADS):
            la_f = la[h:h + 1, :]
            la_b = la[RET_HEADS + h:RET_HEADS + h + 1, :]
            dc_ref[h] = (jnp.exp(jnp.where(jj <= ii, (ii - jj) * la_f, NEG_BIG))
                         + jnp.exp(jnp.where(jj >= ii, (jj - ii) * la_b, NEG_BIG)))
            we_ref[h, 0] = jnp.exp((T - 1.0 - ii) * la_f)
            we_ref[h, 1] = jnp.exp(ii * la_b)
            we_ref[h, 2] = jnp.exp((ii + 1.0) * la_f)
            we_ref[h, 3] = jnp.exp((T - ii) * la_b)
            cd_ref[h] = jnp.concatenate([jnp.broadcast_to(jnp.exp(T * la_f), (SUBLANES, LANES)),
                                         jnp.broadcast_to(jnp.exp(T * la_b), (SUBLANES, LANES))], axis=1)

    for s in range(in_ref.shape[0]):
        if prev_ref is not None:
            st_ref[s, 0:n_prev] = prev_ref[s]
        _ret_sequence(latent, seq_len, in_ref.at[s], gn_ref, None if s0_ref is None else s0_ref.at[s],
                      o_ref.at[s], None if st_ref is None else st_ref.at[s, n_prev], dc_ref, we_ref, cd_ref,
                      sf_ref.at[s], sb_ref.at[s], run_ref.at[s])


def _ret_sequence(latent, seq_len, in_ref, gn_ref, s0_ref, o_ref, st_ref, dc_ref, we_ref, cd_ref,
                  sf_ref, sb_ref, run_ref):
    T = SCAN_CHUNK
    nc = seq_len // T

    if latent:
        run_ref[...] = s0_ref[...]
    else:
        run_ref[...] = jnp.zeros(run_ref.shape, F32)

    def phase0(c, carry):
        r0 = pl.multiple_of(c * T, T)
        for h in range(RET_HEADS):
            blk = h // 2
            k_blk = in_ref[pl.ds(r0, T), 256 + blk * LANES:256 + (blk + 1) * LANES].astype(BF16)
            v_h = in_ref[pl.ds(r0, T), 512 + h * LANES:512 + (h + 1) * LANES]
            vw = jnp.concatenate([v_h * we_ref[h, 0], v_h * we_ref[h, 1]], axis=1)
            cs = _bdot_tn(k_blk, vw)[(h % 2) * 64:(h % 2 + 1) * 64]
            s_run = run_ref[h]
            sf_ref[c, h] = s_run[:, 0:LANES]
            sb_ref[c, h] = cs[:, LANES:2 * LANES]
            new_f = s_run[:, 0:LANES] * cd_ref[h, 0:1, 0:LANES] + cs[:, 0:LANES]
            run_ref[h] = jnp.concatenate([new_f, s_run[:, LANES:2 * LANES]], axis=1)
        return carry

    unroll = min(nc, RET_UNROLL)
    lax.fori_loop(0, nc, phase0, 0, unroll=unroll)

    def bwd_states(k, carry):
        c = nc - 1 - k
        for h in range(RET_HEADS):
            s_run = run_ref[h]
            cs_b = sb_ref[c, h]
            sb_ref[c, h] = s_run[:, LANES:2 * LANES]
            new_b = s_run[:, LANES:2 * LANES] * cd_ref[h, 0:1, LANES:2 * LANES] + cs_b
            run_ref[h] = jnp.concatenate([s_run[:, 0:LANES], new_b], axis=1)
        return carry

    lax.fori_loop(0, nc, bwd_states, 0, unroll=unroll)
    if not latent:
        for h in range(RET_HEADS):
            for d in range(2):
                st_ref[d, h] = run_ref[h, :, d * LANES:(d + 1) * LANES]

    lane_c = _lane_iota((T, LANES))
    zeros64 = jnp.zeros((64, 2 * LANES), F32)

    def phase1(c, carry):
        r0 = pl.multiple_of(c * T, T)
        for h in range(RET_HEADS):
            blk = h // 2
            q_blk = in_ref[pl.ds(r0, T), blk * LANES:(blk + 1) * LANES]
            k_blk = in_ref[pl.ds(r0, T), 256 + blk * LANES:256 + (blk + 1) * LANES].astype(BF16)
            v_h = in_ref[pl.ds(r0, T), 512 + h * LANES:512 + (h + 1) * LANES].astype(BF16)
            g_h = in_ref[pl.ds(r0, T), 1024 + h * LANES:1024 + (h + 1) * LANES]
            q_m = jnp.where((lane_c < 64) == (h % 2 == 0), q_blk, 0.0)
            m_h = (_bdot_nt(q_m, k_blk) * dc_ref[h]).astype(BF16)
            o = jnp.dot(m_h, v_h, preferred_element_type=F32)
            s_in = jnp.concatenate([sf_ref[c, h], sb_ref[c, h]], axis=1)
            s_pad = jnp.concatenate([s_in, zeros64] if h % 2 == 0 else [zeros64, s_in], axis=0)
            oi = _bdot(q_blk, s_pad)
            o = o + oi[:, 0:LANES] * we_ref[h, 2] + oi[:, LANES:2 * LANES] * we_ref[h, 3]
            oc = o - jnp.mean(o, axis=-1, keepdims=True)
            on = oc * lax.rsqrt(jnp.mean(oc * oc, axis=-1, keepdims=True) + NORM_EPS)
            o_ref[pl.ds(r0, T), h * LANES:(h + 1) * LANES] = (
                on * gn_ref[:, h * LANES:(h + 1) * LANES] * _silu(g_h))
        return carry

    lax.fori_loop(0, nc, phase1, 0, unroll=unroll)


def _ret_mixer(ret_in, bsz, seq_len, latent, layer, logit_rows, gn_w, s0, prev_states):
    nc = seq_len // SCAN_CHUNK
    T = SCAN_CHUNK
    n_prev = 0 if prev_states is None else prev_states.shape[1]
    nseq = max(1, SCAN_CHUNKS_PER_STEP // nc)
    assert bsz % nseq == 0
    per_b3 = lambda b: (b, 0, 0)
    per_b4 = lambda b: (b, 0, 0, 0)
    in_specs = [pl.BlockSpec((nseq, seq_len, RET_W), per_b3),
                _layer_spec((SUBLANES, LANES), layer),
                _layer_spec((1, 512), layer)]
    args = [ret_in.reshape(bsz, seq_len, RET_W), logit_rows, gn_w]
    out_specs = [pl.BlockSpec((nseq, seq_len, 512), per_b3)]
    out_shape = [jax.ShapeDtypeStruct((bsz, seq_len, 512), F32)]
    if latent:
        in_specs.append(pl.BlockSpec((nseq, RET_HEADS, 64, 2 * LANES), per_b4))
        args.append(s0)
    else:
        st_tail = (2, RET_HEADS, RET_QK, RET_V)
        per_b6 = lambda b: (b, 0, 0, 0, 0, 0)
        if n_prev:
            in_specs.append(pl.BlockSpec((nseq, n_prev) + st_tail, per_b6))
            args.append(prev_states)
        out_specs.append(pl.BlockSpec((nseq, n_prev + 1) + st_tail, per_b6))
        out_shape.append(jax.ShapeDtypeStruct((bsz, n_prev + 1) + st_tail, F32))
    scratch = [pltpu.VMEM((RET_HEADS, T, T), F32),
               pltpu.VMEM((RET_HEADS, 4, T, LANES), F32),
               pltpu.VMEM((RET_HEADS, SUBLANES, 2 * LANES), F32),
               pltpu.VMEM((nseq, nc, RET_HEADS, 64, LANES), F32),
               pltpu.VMEM((nseq, nc, RET_HEADS, 64, LANES), F32),
               pltpu.VMEM((nseq, RET_HEADS, 64, 2 * LANES), F32)]
    return pl.pallas_call(
        functools.partial(_ret_kernel, latent, seq_len, n_prev),
        grid=(bsz // nseq,),
        in_specs=in_specs, out_specs=out_specs, out_shape=out_shape,
        scratch_shapes=scratch,
        compiler_params=_cparams(1),
        name="ret_lat" if latent else "ret_ctx",
    )(*args)


LOG2_E = 1.4426950408889634


def _softmax_pv(s, v, scale=1.0):
    m = jnp.max(s, axis=-1, keepdims=True)
    p = jnp.exp2((s - m) * (scale * LOG2_E))
    l = jnp.sum(p, axis=-1, keepdims=True)
    return jnp.dot(p.astype(BF16), v, preferred_element_type=F32) / l


def _once_per_sequence(fill, single_step):
    if single_step:
        fill()
    else:
        pl.when(pl.program_id(1) == 0)(fill)


def _att_kernel(latent, tq, *refs):
    if latent:
        q_ref, kn_ref, vn_ref, kc_ref, vc_ref, o_ref, kb_ref, vb_ref = refs
    else:
        q_ref, kn_ref, vn_ref, o_ref, kb_ref, vb_ref = refs
        kc_ref = vc_ref = None
    for s in range(q_ref.shape[0]):
        _att_sequence(latent, tq, q_ref.at[s], kn_ref.at[s], vn_ref.at[s],
                      None if kc_ref is None else kc_ref.at[s], None if vc_ref is None else vc_ref.at[s],
                      o_ref.at[s], kb_ref.at[s], vb_ref.at[s])


def _att_sequence(latent, tq, q_ref, kn_ref, vn_ref, kc_ref, vc_ref, o_ref, kb_ref, vb_ref):
    def _fill():
        if latent:
            kb_ref[0:PAST_LEN, :] = kc_ref[...].astype(BF16)
            vb_ref[0:PAST_LEN, :] = vc_ref[...].astype(BF16)
            kb_ref[PAST_LEN:, :] = kn_ref[...].astype(BF16)
            vb_ref[PAST_LEN:, :] = vn_ref[...].astype(BF16)
        else:
            kb_ref[...] = kn_ref[...].astype(BF16)
            vb_ref[...] = vn_ref[...].astype(BF16)

    _once_per_sequence(_fill, single_step=(kn_ref.shape[0] == tq))

    sub = min(tq, ATT_SUB_ROWS)
    lane_q = _lane_iota((sub, LANES))
    for i in range(tq // sub):
        rows = slice(i * sub, (i + 1) * sub)
        acc = []
        for g in range(ATT_KV_HEADS):
            qs = jnp.concatenate(
                [jnp.where((lane_q < 64) == (g == 0), q_ref[rows, j * LANES:(j + 1) * LANES] * 0.125, 0.0).astype(BF16)
                 for j in range(4)], axis=0)
            s = lax.dot_general(qs, kb_ref[...], (((1,), (1,)), ((), ())), preferred_element_type=F32)
            acc.append(_softmax_pv(s, vb_ref[...]))
        for j in range(4):
            o_ref[rows, j * LANES:(j + 1) * LANES] = jnp.where(lane_q < 64, acc[0][j * sub:(j + 1) * sub],
                                                               acc[1][j * sub:(j + 1) * sub])


def _att_mixer(aq, ak, av, bsz, seq_len, latent, k_ctx, v_ctx):
    tq = 512 if latent else 256
    lk = seq_len + (PAST_LEN if latent else 0)
    nseq = 1 if latent else ATT_CTX_SEQS
    assert bsz % nseq == 0
    qmap = lambda b, i: (b, i, 0)
    bmap = lambda b, i: (b, 0, 0)
    in_specs = [pl.BlockSpec((nseq, tq, 512), qmap),
                pl.BlockSpec((nseq, seq_len, LANES), bmap),
                pl.BlockSpec((nseq, seq_len, LANES), bmap)]
    args = [aq.reshape(bsz, seq_len, 512), ak.reshape(bsz, seq_len, LANES), av.reshape(bsz, seq_len, LANES)]
    if latent:
        in_specs += [pl.BlockSpec((nseq, PAST_LEN, LANES), bmap)] * 2
        args += [k_ctx, v_ctx]
    return pl.pallas_call(
        functools.partial(_att_kernel, latent, tq),
        grid=(bsz // nseq, seq_len // tq),
        in_specs=in_specs,
        out_specs=pl.BlockSpec((nseq, tq, 512), qmap),
        out_shape=jax.ShapeDtypeStruct((bsz, seq_len, 512), F32),
        scratch_shapes=[pltpu.VMEM((nseq, lk, LANES), BF16), pltpu.VMEM((nseq, lk, LANES), BF16)],
        compiler_params=_cparams(2),
        name="att_lat" if latent else "att_ctx",
    )(*args)


def _mla_kernel(latent, tq, *refs):
    if latent:
        (mq_ref, cn_ref, rn_ref, cc_ref, rc_ref, qn_ref, wq_ref, wk_ref, wv_ref,
         cos_ref, su_ref, sd_ref, o_ref, kb_ref, vb_ref) = refs
    else:
        (mq_ref, cn_ref, rn_ref, qn_ref, wq_ref, wk_ref, wv_ref, o_ref, kb_ref, vb_ref) = refs
        cc_ref = rc_ref = cos_ref = su_ref = sd_ref = None
    for s in range(mq_ref.shape[0]):
        _mla_sequence(latent, tq, mq_ref.at[s], cn_ref.at[s], rn_ref.at[s],
                      None if cc_ref is None else cc_ref.at[s], None if rc_ref is None else rc_ref.at[s],
                      qn_ref, wq_ref, wk_ref, wv_ref, cos_ref, su_ref, sd_ref,
                      o_ref.at[s], kb_ref.at[s], vb_ref.at[s])


def _mla_sequence(latent, tq, mq_ref, cn_ref, rn_ref, cc_ref, rc_ref, qn_ref, wq_ref, wk_ref, wv_ref,
                  cos_ref, su_ref, sd_ref, o_ref, kb_ref, vb_ref):
    def fill(r0, ckv, kr):
        cb = ckv.astype(BF16)
        n = ckv.shape[0]
        kn = jnp.dot(cb, wk_ref[...], preferred_element_type=F32)
        for h in range(MLA_HEADS):
            kb_ref[r0:r0 + n, h * LANES:(h + 1) * LANES] = (kn[:, h * LANES:(h + 1) * LANES] + kr).astype(BF16)
        vb_ref[r0:r0 + n, :] = jnp.dot(cb, wv_ref[...], preferred_element_type=F32).astype(BF16)

    def _fill():
        if latent:
            fill(0, cc_ref[...], rc_ref[...])
            fill(PAST_LEN, cn_ref[...], rn_ref[...])
        else:
            fill(0, cn_ref[...], rn_ref[...])

    _once_per_sequence(_fill, single_step=(cn_ref.shape[0] == tq))

    q = _bdot(_rms(mq_ref[...], qn_ref[...]), wq_ref[...])
    lane_q = _lane_iota((tq, LANES))
    scale = (MLA_NOPE + MLA_ROPE) ** -0.5
    for p in range(4):
        halves = []
        for h in (2 * p, 2 * p + 1):
            qh = q[:, h * LANES:(h + 1) * LANES]
            if latent:
                qh = _rope(qh, cos_ref[...], su_ref[...], sd_ref[...], 8)
            s = _bdot_nt(qh, kb_ref[:, h * LANES:(h + 1) * LANES])
            halves.append(_softmax_pv(s, vb_ref[:, p * LANES:(p + 1) * LANES], scale))
        o_ref[:, p * LANES:(p + 1) * LANES] = jnp.where(lane_q < 64, halves[0], halves[1])


def _mla_mixer(mq, ckv, kr, bsz, seq_len, latent, ckv_ctx, kr_ctx, qn, wq, wk, wv, layer, tabs):
    tq = 512 if latent else 256
    lk = seq_len + (PAST_LEN if latent else 0)
    nseq = 1 if latent else MLA_CTX_SEQS
    assert bsz % nseq == 0
    qmap = lambda b, i: (b, i, 0)
    bmap = lambda b, i: (b, 0, 0)
    lyr3 = lambda b, i: (layer, 0, 0)
    in_specs = [pl.BlockSpec((nseq, tq, MLA_Q_RANK), qmap),
                pl.BlockSpec((nseq, seq_len, MLA_KV_RANK), bmap),
                pl.BlockSpec((nseq, seq_len, LANES), bmap)]
    args = [mq.reshape(bsz, seq_len, MLA_Q_RANK), ckv.reshape(bsz, seq_len, MLA_KV_RANK),
            kr.reshape(bsz, seq_len, LANES)]
    if latent:
        in_specs += [pl.BlockSpec((nseq, PAST_LEN, MLA_KV_RANK), bmap),
                     pl.BlockSpec((nseq, PAST_LEN, LANES), bmap)]
        args += [ckv_ctx, kr_ctx]
    in_specs += [_layer_spec((1, MLA_Q_RANK), layer),
                 _resident((None, MLA_Q_RANK, 1024), lyr3),
                 _resident((None, MLA_KV_RANK, 1024), lyr3),
                 _resident((None, MLA_KV_RANK, 512), lyr3)]
    args += [qn, wq, wk, wv]
    if latent:
        in_specs += [pl.BlockSpec((tq, LANES), lambda b, i: (i, 0))] * 3
        args += list(tabs)
    return pl.pallas_call(
        functools.partial(_mla_kernel, latent, tq),
        grid=(bsz // nseq, seq_len // tq),
        in_specs=in_specs,
        out_specs=pl.BlockSpec((nseq, tq, 512), qmap),
        out_shape=jax.ShapeDtypeStruct((bsz, seq_len, 512), F32),
        scratch_shapes=[pltpu.VMEM((nseq, lk, 1024), BF16), pltpu.VMEM((nseq, lk, 512), BF16)],
        compiler_params=_cparams(2),
        name="mla_lat" if latent else "mla_ctx",
    )(*args)


def _merge_kernel(x_ref, sh_ref, sc_ref, ga_ref, gpre_ref, gpost_ref, o1_ref, o2_ref, o3_ref, o4_ref,
                  wm_ref, bm_ref, wb_ref, wo_ref, y_ref):
    x = x_ref[...]
    hb = (_rms(x, gpre_ref[...]) * (1.0 + sc_ref[...]) + sh_ref[...]).astype(BF16)
    merged = None
    for k, o_ref in enumerate((o1_ref, o2_ref, o3_ref, o4_ref)):
        cols = slice(k * D_MODEL, (k + 1) * D_MODEL)
        gate = jax.nn.sigmoid(jnp.dot(hb, wm_ref[:, cols], preferred_element_type=F32) + bm_ref[:, cols])
        term = gate * _bdot(o_ref[...], wb_ref[k])
        merged = term if merged is None else merged + term
    m = _bdot(merged, wo_ref[...])
    y_ref[...] = x + ga_ref[...] * _rms(m, gpost_ref[...])


def _merge(x2d, mod5, layer, latent, seq_len, g_pre, g_post, branches, w_merge, b_merge, w_br, w_out):
    n = x2d.shape[0]
    tm = TM_ROWS
    row = lambda i: (i, 0)
    in_specs = ([pl.BlockSpec((tm, D_MODEL), row)]
                + [_mod_spec(layer, latent, seq_len, tm, k) for k in (0, 1, 2)]
                + [_layer_spec((1, D_MODEL), layer)] * 2
                + [pl.BlockSpec((tm, 512), row)] * 4
                + [_resident((None, D_MODEL, 4 * D_MODEL), lambda i: (layer, 0, 0)),
                   _layer_spec((1, 4 * D_MODEL), layer),
                   _resident((None, 4, 512, D_MODEL), lambda i: (layer, 0, 0, 0)),
                   _resident((None, D_MODEL, D_MODEL), lambda i: (layer, 0, 0))])
    return pl.pallas_call(
        _merge_kernel,
        grid=(n // tm,),
        in_specs=in_specs,
        out_specs=pl.BlockSpec((tm, D_MODEL), row),
        out_shape=jax.ShapeDtypeStruct((n, D_MODEL), F32),
        compiler_params=_cparams(1),
        name="merge_lat" if latent else "merge_ctx",
    )(x2d, mod5, mod5, mod5, g_pre, g_post, *[b.reshape(n, 512) for b in branches],
      w_merge, b_merge, w_br, w_out)


def _ffn_kernel(tm, seq_len, x_ref, xp_ref, xn_ref, sh_ref, sc_ref, gf_ref, gpre_ref, gpost_ref,
                wu_ref, cw_ref, cb_ref, wd_ref, y_ref):
    n_seq = max(tm // seq_len, 1)
    seg = tm // n_seq
    tiles_per_seq = seq_len // tm
    if tiles_per_seq > 1:
        i = pl.program_id(0)
        first = (i % tiles_per_seq) == 0
        last = (i % tiles_per_seq) == tiles_per_seq - 1
    else:
        first = last = True

    def nm(x):
        return _rms(x, gpre_ref[...]) * (1.0 + sc_ref[...]) + sh_ref[...]

    x = x_ref[...]
    h = nm(x)
    zero8 = jnp.zeros((SUBLANES, D_MODEL), F32)
    pieces = [jnp.where(first, 0.0, nm(xp_ref[...]))]
    for s in range(n_seq):
        pieces.append(h[s * seg:(s + 1) * seg])
        pieces.append(zero8 if s < n_seq - 1 else jnp.where(last, 0.0, nm(xn_ref[...])))
    hb = jnp.concatenate(pieces, axis=0).astype(BF16)
    rows = tm + SUBLANES * (n_seq + 1)

    def conv_cols(c0):
        cols = slice(c0, c0 + FF_HALF)
        p = jnp.dot(hb, wu_ref[:, cols], preferred_element_type=F32)
        w = cw_ref[:, cols]
        u = (pltpu.roll(p, 1, 0) * w[0:1] + p * w[1:2] + pltpu.roll(p, rows - 1, 0) * w[2:3])
        segs = [u[SUBLANES + s * (seg + SUBLANES):SUBLANES + s * (seg + SUBLANES) + seg] for s in range(n_seq)]
        return (segs[0] if n_seq == 1 else jnp.concatenate(segs, axis=0)) + cb_ref[:, cols]

    acc = None
    for j in range(D_FF // FF_HALF):
        up = conv_cols(j * FF_HALF)
        gate = conv_cols(D_FF + j * FF_HALF)
        part = _bdot(_silu(gate) * up, wd_ref[j * FF_HALF:(j + 1) * FF_HALF, :])
        acc = part if acc is None else acc + part
    y_ref[...] = x + gf_ref[...] * _rms(acc, gpost_ref[...])


def _ffn(x2d, mod5, layer, latent, seq_len, g_pre, g_post, w_up, conv_w, conv_b, w_down):
    n = x2d.shape[0]
    tm = TM_ROWS
    assert seq_len % tm == 0 or tm % seq_len == 0
    hb = tm // SUBLANES
    n_hb = n // SUBLANES
    row = lambda i: (i, 0)
    lyr3 = lambda i: (layer, 0, 0)
    in_specs = ([pl.BlockSpec((tm, D_MODEL), row),
                 pl.BlockSpec((SUBLANES, D_MODEL), lambda i: (jnp.maximum(i * hb - 1, 0), 0)),
                 pl.BlockSpec((SUBLANES, D_MODEL), lambda i: (jnp.minimum((i + 1) * hb, n_hb - 1), 0))]
                + [_mod_spec(layer, latent, seq_len, tm, k) for k in (3, 4, 5)]
                + [_layer_spec((1, D_MODEL), layer)] * 2
                + [_resident((None, D_MODEL, 2 * D_FF), lyr3),
                   _resident((None, 3, 2 * D_FF), lyr3),
                   _resident((None, 1, 2 * D_FF), lyr3),
                   _resident((None, D_FF, D_MODEL), lyr3)])
    return pl.pallas_call(
        functools.partial(_ffn_kernel, tm, seq_len),
        grid=(n // tm,),
        in_specs=in_specs,
        out_specs=pl.BlockSpec((tm, D_MODEL), row),
        out_shape=jax.ShapeDtypeStruct((n, D_MODEL), F32),
        compiler_params=_cparams(1),
        name="ffn_lat" if latent else "ffn_ctx",
    )(x2d, x2d, x2d, mod5, mod5, mod5, g_pre, g_post, w_up, conv_w, conv_b, w_down)


def _rope_tables(seq_len, head_dim, lane_of_dim):
    d_axis = head_dim // 2
    t = np.arange(seq_len)
    pos = np.stack([(t // GRID_W).astype(np.float32), (t % GRID_W).astype(np.float32)], axis=0)
    inv_freq = (np.float32(ROPE_BASE) ** (-np.arange(0, d_axis, 2, dtype=np.float32) / np.float32(d_axis))).astype(np.float32)
    cos = np.ones((seq_len, LANES), np.float32)
    s_up = np.zeros((seq_len, LANES), np.float32)
    s_dn = np.zeros((seq_len, LANES), np.float32)
    for lane, d in enumerate(lane_of_dim):
        if d < 0:
            continue
        axis, j = divmod(d, d_axis)
        second = j >= d_axis // 2
        ang = (pos[axis] * inv_freq[j % (d_axis // 2)]).astype(np.float32)
        cos[:, lane] = np.cos(ang)
        if second:
            s_up[:, lane] = np.sin(ang)
        else:
            s_dn[:, lane] = -np.sin(ang)
    return jnp.asarray(cos), jnp.asarray(s_up), jnp.asarray(s_dn)


def _ssd_expand_matrix():
    e = np.zeros((LANES, 1024), np.float32)
    for d in range(2):
        for h in range(SSD_HEADS):
            e[d * SSD_HEADS + h, d * 512 + h * 64:d * 512 + (h + 1) * 64] = 1.0
    return jnp.asarray(e, dtype=BF16)


def _matmul_weights(w_in, mla_w_uq, mla_w_ukv, w_br_ssd, w_br_ret, w_br_att, w_br_mla, w_merge, w_out,
                    w_ffn_up, ffn_conv_w, ffn_conv_b, w_ffn_down):
    off = np.cumsum([0, 512, 768, 16, 256, 256, 512, 512, 512, 128, 128, 384, 288])
    col = lambda k: w_in[:, :, off[k]:off[k + 1]]
    zeros = lambda n: jnp.zeros((DEPTH, D_MODEL, n), F32)
    w_ssd = jnp.pad(w_in[:, :, :off[3]], ((0, 0), (0, 0), (0, SSD_W - off[3]))).astype(BF16)
    w_ret = w_in[:, :, off[3]:off[7]].astype(BF16)
    aq = col(7).reshape(DEPTH, D_MODEL, ATT_KV_HEADS, 4, ATT_HEAD_DIM).swapaxes(2, 3).reshape(DEPTH, D_MODEL, 512)
    mckv = col(11)
    w_am = jnp.concatenate(
        [aq, col(8), col(9), col(10), mckv[:, :, :MLA_KV_RANK], zeros(64), mckv[:, :, MLA_KV_RANK:], zeros(32)],
        axis=2).astype(BF16)

    uq = mla_w_uq.reshape(DEPTH, MLA_Q_RANK, MLA_HEADS, MLA_NOPE + MLA_ROPE)
    wq = jnp.pad(uq, ((0, 0), (0, 0), (0, 0), (0, 32))).reshape(DEPTH, MLA_Q_RANK, 1024).astype(BF16)
    ukv = mla_w_ukv.reshape(DEPTH, MLA_KV_RANK, MLA_HEADS, MLA_NOPE + MLA_V)
    wk = jnp.pad(ukv[..., :MLA_NOPE], ((0, 0), (0, 0), (0, 0), (0, 64))).reshape(DEPTH, MLA_KV_RANK, 1024).astype(BF16)
    wv = ukv[..., MLA_NOPE:].reshape(DEPTH, MLA_KV_RANK, 512).astype(BF16)

    w_att = (w_br_att.reshape(DEPTH, ATT_KV_HEADS, 4, ATT_HEAD_DIM, D_MODEL).swapaxes(1, 2)
             .reshape(DEPTH, 512, D_MODEL))
    w_br = jnp.stack([w_br_ssd, w_br_ret, w_att, w_br_mla], axis=1).astype(BF16)

    return dict(
        w_ssd=w_ssd, w_ret=w_ret, w_am=w_am, wq=wq, wk=wk, wv=wv, w_br=w_br,
        w_merge=w_merge.astype(BF16), w_out=w_out.astype(BF16),
        w_up=w_ffn_up.astype(BF16), ffn_cw=ffn_conv_w,
        ffn_cb=ffn_conv_b.reshape(DEPTH, 1, 2 * D_FF),
        w_down=w_ffn_down.astype(BF16),
    )


def _small_params(ssd_conv_w, ssd_conv_b, ssd_dt_bias, ssd_a_log, ssd_d, ssd_norm_w,
                  ret_decay_logit, ret_gn_w, att_q_norm, att_k_norm, mla_q_norm, mla_kv_norm,
                  b_merge, g_pre_mix, g_post_mix, g_pre_ffn, g_post_ffn):
    row = lambda v: v.reshape(DEPTH, 1, -1)
    pad_row = lambda v: jnp.pad(row(v), ((0, 0), (0, 0), (0, LANES - v[0].size)))
    return dict(
        qn=row(jnp.tile(att_q_norm, (1, 2))), kn=row(jnp.tile(att_k_norm, (1, 2))),
        cn=row(mla_kv_norm),
        ssd_cw=ssd_conv_w, ssd_cb=row(ssd_conv_b),
        ssd_dtb=pad_row(ssd_dt_bias), ssd_alog=pad_row(ssd_a_log),
        ssd_dsk=row(jnp.repeat(ssd_d, SSD_HEAD_DIM, axis=1)),
        ssd_nw=row(ssd_norm_w),
        ret_logit=jnp.broadcast_to(ret_decay_logit.reshape(DEPTH, SUBLANES, 1), (DEPTH, SUBLANES, LANES)),
        ret_gn=row(ret_gn_w),
        mla_qn=row(mla_q_norm),
        b_merge=row(b_merge),
        g_pre_mix=row(g_pre_mix), g_post_mix=row(g_post_mix),
        g_pre_ffn=row(g_pre_ffn), g_post_ffn=row(g_post_ffn),
    )


def _ssd_state_to_pairs(s):
    b = s.shape[0]
    s = s.reshape(b, 2, 4, 2, 64, 64).transpose(0, 2, 4, 1, 3, 5)
    return s.reshape(b, 4, 64, 256)


def _ret_state_to_lanes(s):
    b = s.shape[0]
    return s.transpose(0, 2, 3, 1, 4).reshape(b, RET_HEADS, 64, 256)


def _trunk_pass(x2d, mod5, layer, latent, bsz, seq_len, lp, mw, tabs64, tabs_mla, expand, ctx, prev_cache):
    (ssd_in, ret_in, aq, ak, av, mq, ckv, kr) = _inproj(
        x2d, mod5, layer, latent, seq_len, lp['g_pre_mix'], mw['w_ssd'], mw['w_ret'], mw['w_am'],
        lp['qn'], lp['kn'], lp['cn'], (tabs64 + tabs_mla) if latent else None)
    ssd_res = _ssd_mixer(ssd_in, bsz, seq_len, latent, layer, lp['ssd_cw'], lp['ssd_cb'], lp['ssd_dtb'],
                         lp['ssd_alog'], lp['ssd_dsk'], lp['ssd_nw'], expand,
                         ctx['ssd'] if latent else None, prev_cache['ssd'] if prev_cache else None)
    ret_res = _ret_mixer(ret_in, bsz, seq_len, latent, layer, lp['ret_logit'], lp['ret_gn'],
                         ctx['ret'] if latent else None, prev_cache['ret'] if prev_cache else None)
    o_att = _att_mixer(aq, ak, av, bsz, seq_len, latent,
                       ctx['att_k'] if latent else None, ctx['att_v'] if latent else None)
    o_mla = _mla_mixer(mq, ckv, kr, bsz, seq_len, latent,
                       ctx['mla_ckv'] if latent else None, ctx['mla_kr'] if latent else None,
                       lp['mla_qn'], mw['wq'], mw['wk'], mw['wv'], layer, tabs_mla)
    x2d = _merge(x2d, mod5, layer, latent, seq_len, lp['g_pre_mix'], lp['g_post_mix'],
                 (ssd_res[0], ret_res[0], o_att, o_mla), mw['w_merge'], lp['b_merge'], mw['w_br'], mw['w_out'])
    x2d = _ffn(x2d, mod5, layer, latent, seq_len, lp['g_pre_ffn'], lp['g_post_ffn'],
               mw['w_up'], mw['ffn_cw'], mw['ffn_cb'], mw['w_down'])
    cache = None
    if not latent:
        cache = dict(ssd=ssd_res[1], ret=ret_res[1],
                     att_k=ak.reshape(bsz, seq_len, ATT_KV_HEADS, ATT_HEAD_DIM),
                     att_v=av.reshape(bsz, seq_len, ATT_KV_HEADS, ATT_HEAD_DIM),
                     mla_ckv=ckv.reshape(bsz, seq_len, MLA_KV_RANK),
                     mla_krope=kr.reshape(bsz, seq_len, LANES)[:, :, 64:64 + MLA_ROPE])
    return x2d, cache


def kernel(x_prompt, x_sample, state_ssd, state_ret, cache_att_k, cache_att_v, cache_mla_ckv, cache_mla_krope, c, c_ctx, w_mod, b_mod, g_pre_mix, g_post_mix, g_pre_ffn, g_post_ffn, w_in, ssd_conv_w, ssd_conv_b, ssd_dt_bias, ssd_a_log, ssd_d, ssd_norm_w, ret_decay_logit, ret_gn_w, att_q_norm, att_k_norm, mla_q_norm, mla_w_uq, mla_kv_norm, mla_w_ukv, w_br_ssd, w_br_ret, w_br_att, w_br_mla, w_merge, b_merge, w_out, w_ffn_up, ffn_conv_w, ffn_conv_b, w_ffn_down):
    cvec = jnp.concatenate([c_ctx[None, :], c, jnp.zeros((SUBLANES - 1 - DEC_BATCH, D_MODEL), F32)], axis=0)
    mod5 = _modulation(cvec, w_mod, b_mod).reshape(DEPTH, SUBLANES, 6, 1, D_MODEL)

    lane_dims_64 = [d % 64 for d in range(LANES)]
    lane_dims_mla = [d - 64 if 64 <= d < 96 else -1 for d in range(LANES)]
    tabs64 = _rope_tables(DEC_SEQ, ATT_HEAD_DIM, lane_dims_64)
    tabs_mla = _rope_tables(DEC_SEQ, MLA_ROPE, lane_dims_mla)
    expand = _ssd_expand_matrix()

    xp = x_prompt.reshape(BATCH * SEQ, D_MODEL)
    xs = x_sample.reshape(DEC_BATCH * DEC_SEQ, D_MODEL)
    caches = []
    mw = _matmul_weights(w_in, mla_w_uq, mla_w_ukv, w_br_ssd, w_br_ret, w_br_att, w_br_mla, w_merge, w_out,
                         w_ffn_up, ffn_conv_w, ffn_conv_b, w_ffn_down)
    lp = _small_params(ssd_conv_w, ssd_conv_b, ssd_dt_bias, ssd_a_log, ssd_d, ssd_norm_w,
                       ret_decay_logit, ret_gn_w, att_q_norm, att_k_norm, mla_q_norm, mla_kv_norm,
                       b_merge, g_pre_mix, g_post_mix, g_pre_ffn, g_post_ffn)
    for i in range(DEPTH):
        xp, cache = _trunk_pass(xp, mod5, i, False, BATCH, SEQ, lp, mw, tabs64, tabs_mla, expand, None,
                                caches[-1] if caches else None)
        caches.append(cache)
        ctx = dict(ssd=_ssd_state_to_pairs(state_ssd[:, i]), ret=_ret_state_to_lanes(state_ret[:, i]),
                   att_k=cache_att_k[:, i].reshape(DEC_BATCH, PAST_LEN, LANES),
                   att_v=cache_att_v[:, i].reshape(DEC_BATCH, PAST_LEN, LANES),
                   mla_ckv=cache_mla_ckv[:, i],
                   mla_kr=jnp.pad(cache_mla_krope[:, i], ((0, 0), (0, 0), (64, 32))))
        xs, _ = _trunk_pass(xs, mod5, i, True, DEC_BATCH, DEC_SEQ, lp, mw, tabs64, tabs_mla, expand, ctx, None)

    stack = lambda k: jnp.stack([cc[k] for cc in caches], axis=1)
    return (xp.reshape(BATCH, SEQ, D_MODEL), xs.reshape(DEC_BATCH, DEC_SEQ, D_MODEL),
            caches[-1]['ssd'], caches[-1]['ret'],
            stack('att_k'), stack('att_v'), stack('mla_ckv'), stack('mla_krope'))
```

```python
import functools

import numpy as np
import jax
import jax.numpy as jnp
from jax import lax
from jax.experimental import pallas as pl
from jax.experimental.pallas import tpu as pltpu

D_MODEL = 1024
BATCH = 32
SEQ = 256
DEPTH = 2
DEC_BATCH = 4
DEC_SEQ = 2048
PAST_LEN = 256
GRID_W = 64
ROPE_BASE = 10000.0
NORM_EPS = 1e-6
SCAN_CHUNK = 128
SSD_HEADS = 8
SSD_HEAD_DIM = 64
SSD_INNER = 512
SSD_GROUPS = 2
SSD_STATE = 64
SSD_CONV_CH = 768
RET_HEADS = 4
RET_QK = 64
RET_V = 128
ATT_HEADS = 8
ATT_KV_HEADS = 2
ATT_HEAD_DIM = 64
MLA_HEADS = 8
MLA_Q_RANK = 384
MLA_KV_RANK = 256
MLA_NOPE = 64
MLA_ROPE = 32
MLA_V = 64
D_FF = 2816

F32 = jnp.float32
BF16 = jnp.bfloat16
LANES = 128
SUBLANES = 8
VMEM_LIMIT = 56 * 1024 * 1024
NEG_BIG = -1e30

SSD_W = 1408
RET_W = 1536
ATT_MLA_W = 1536
TM_ROWS = 512
SSD_CHUNKS_PER_STEP = 4
SCAN_CHUNKS_PER_STEP = 8
RET_UNROLL = 8
SSD_UNROLL = 4
ATT_CTX_SEQS = 8
MLA_CTX_SEQS = 4
ATT_SUB_ROWS = 128
FF_HALF = D_FF


def _cparams(n_axes):
    return pltpu.CompilerParams(dimension_semantics=("arbitrary",) * n_axes,
                                vmem_limit_bytes=VMEM_LIMIT)


def _silu(x):
    return x * jax.nn.sigmoid(x)


def _softplus(x):
    return jnp.maximum(x, 0.0) + jnp.log1p(jnp.exp(-jnp.abs(x)))


def _rms(x, w):
    ms = jnp.mean(x * x, axis=-1, keepdims=True)
    return x * lax.rsqrt(ms + NORM_EPS) * w


def _bdot(a, b):
    return jnp.dot(a.astype(BF16), b.astype(BF16), preferred_element_type=F32)


def _bdot_nt(a, b):
    return lax.dot_general(a.astype(BF16), b.astype(BF16), (((1,), (1,)), ((), ())),
                           preferred_element_type=F32)


def _bdot_tn(a, b):
    return lax.dot_general(a.astype(BF16), b.astype(BF16), (((0,), (0,)), ((), ())),
                           preferred_element_type=F32)


def _split_dot(a, b_bf16):
    hi = a.astype(BF16)
    lo = (a - hi.astype(F32)).astype(BF16)
    return (jnp.dot(hi, b_bf16, preferred_element_type=F32)
            + jnp.dot(lo, b_bf16, preferred_element_type=F32))


def _split_dot_left(a_bf16, b):
    hi = b.astype(BF16)
    lo = (b - hi.astype(F32)).astype(BF16)
    return (jnp.dot(a_bf16, hi, preferred_element_type=F32)
            + jnp.dot(a_bf16, lo, preferred_element_type=F32))


def _lane_iota(shape):
    return lax.broadcasted_iota(jnp.int32, shape, len(shape) - 1)


def _row_iota(shape):
    return lax.broadcasted_iota(jnp.int32, shape, 0)


def _rope(xb, cos, sin_up, sin_dn, half):
    return (xb * cos + pltpu.roll(xb, half, 1) * sin_up
            + pltpu.roll(xb, LANES - half, 1) * sin_dn)


def _half_rms(xb, w_row):
    lo = _lane_iota(xb.shape) < 64
    sq = xb * xb
    s_lo = jnp.sum(jnp.where(lo, sq, 0.0), axis=-1, keepdims=True)
    s_hi = jnp.sum(jnp.where(lo, 0.0, sq), axis=-1, keepdims=True)
    ms = jnp.where(lo, s_lo, s_hi) * (1.0 / 64.0)
    return xb * lax.rsqrt(ms + NORM_EPS) * w_row


def _mod_kernel(c_ref, w_ref, b_ref, o_ref):
    c = c_ref[...]
    o_ref[...] = _bdot(_silu(c), w_ref[...]) + b_ref[...]


def _modulation(cvec, w_mod, b_mod):
    tn = 1536
    n_out = 6 * D_MODEL
    return pl.pallas_call(
        _mod_kernel,
        grid=(DEPTH, n_out // tn),
        in_specs=[pl.BlockSpec((SUBLANES, D_MODEL), lambda l, j: (0, 0)),
                  pl.BlockSpec((None, D_MODEL, tn), lambda l, j: (l, 0, j)),
                  pl.BlockSpec((None, 1, tn), lambda l, j: (l, 0, j))],
        out_specs=pl.BlockSpec((None, SUBLANES, tn), lambda l, j: (l, 0, j)),
        out_shape=jax.ShapeDtypeStruct((DEPTH, SUBLANES, n_out), F32),
        compiler_params=_cparams(2),
        name="modulation",
    )(cvec, w_mod, b_mod.reshape(DEPTH, 1, n_out))


def _mod_spec(layer, latent, seq_len, tm, k):
    if latent:
        assert seq_len % tm == 0
    tiles_per_seq = max(seq_len // tm, 1)

    def index_map(i):
        row = 1 + i // tiles_per_seq if latent else 0
        return (layer, row, k, 0, 0)

    return pl.BlockSpec((None, None, None, 1, D_MODEL), index_map)


def _layer_spec(shape2d, layer):
    return pl.BlockSpec((None,) + tuple(shape2d), lambda *_: (layer, 0, 0))


def _resident(block_shape, index_map):
    return pl.BlockSpec(block_shape, index_map, pipeline_mode=pl.Buffered(1))


def _inproj_kernel(latent, *refs):
    if latent:
        (x_ref, sh_ref, sc_ref, g_ref, ws_ref, wr_ref, wa_ref, qn_ref, kn_ref, cn_ref,
         c64_ref, su64_ref, sd64_ref, cm_ref, sum_ref, sdm_ref,
         ssd_ref, ret_ref, aq_ref, ak_ref, av_ref, mq_ref, ckv_ref, kr_ref) = refs
    else:
        (x_ref, sh_ref, sc_ref, g_ref, ws_ref, wr_ref, wa_ref, qn_ref, kn_ref, cn_ref,
         ssd_ref, ret_ref, aq_ref, ak_ref, av_ref, mq_ref, ckv_ref, kr_ref) = refs

    half = x_ref.shape[0] // 2
    for r in range(2):
        rows = slice(r * half, (r + 1) * half)
        h = _rms(x_ref[rows, :], g_ref[...]) * (1.0 + sc_ref[...]) + sh_ref[...]
        hb = h.astype(BF16)

        def proj(w_ref):
            return jnp.dot(hb, w_ref[...], preferred_element_type=F32)

        def rope64(xb):
            if not latent:
                return xb
            return _rope(xb, c64_ref[rows, :], su64_ref[rows, :], sd64_ref[rows, :], 16)

        pa = proj(wa_ref)
        for j in range(4):
            q = _half_rms(pa[:, j * LANES:(j + 1) * LANES], qn_ref[...])
            aq_ref[rows, j * LANES:(j + 1) * LANES] = rope64(q).astype(BF16)
        ak_ref[rows, :] = rope64(_half_rms(pa[:, 512:640], kn_ref[...]))
        av_ref[rows, :] = pa[:, 640:768]
        mq_ref[rows, :] = pa[:, 768:768 + MLA_Q_RANK]
        ckv_ref[rows, :] = _rms(pa[:, 768 + MLA_Q_RANK:768 + MLA_Q_RANK + MLA_KV_RANK], cn_ref[...])
        kr = pa[:, 1408:1536]
        if latent:
            kr = _rope(kr, cm_ref[rows, :], sum_ref[rows, :], sdm_ref[rows, :], 8)
        kr_ref[rows, :] = kr

        pr = proj(wr_ref)
        for j in range(2):
            ret_ref[rows, j * LANES:(j + 1) * LANES] = rope64(pr[:, j * LANES:(j + 1) * LANES])
        for j in range(2, 4):
            ret_ref[rows, j * LANES:(j + 1) * LANES] = rope64(pr[:, j * LANES:(j + 1) * LANES] * 0.125)
        ret_ref[rows, 512:RET_W] = pr[:, 512:RET_W]

        ssd_ref[rows, :] = proj(ws_ref)


def _inproj(x2d, mod5, layer, latent, seq_len, g_pre, w_ssd, w_ret, w_am, qn, kn, cn, tabs):
    n = x2d.shape[0]
    tm = TM_ROWS
    tiles_per_seq = max(seq_len // tm, 1)
    row = lambda i: (i, 0)
    lyr3 = lambda i: (layer, 0, 0)
    in_specs = [pl.BlockSpec((tm, D_MODEL), row),
                _mod_spec(layer, latent, seq_len, tm, 0),
                _mod_spec(layer, latent, seq_len, tm, 1),
                _layer_spec((1, D_MODEL), layer),
                _resident((None, D_MODEL, SSD_W), lyr3),
                _resident((None, D_MODEL, RET_W), lyr3),
                _resident((None, D_MODEL, ATT_MLA_W), lyr3),
                _layer_spec((1, LANES), layer),
                _layer_spec((1, LANES), layer),
                _layer_spec((1, MLA_KV_RANK), layer)]
    args = [x2d, mod5, mod5, g_pre, w_ssd, w_ret, w_am, qn, kn, cn]
    if latent:
        tab = pl.BlockSpec((tm, LANES), lambda i: (i % tiles_per_seq, 0))
        in_specs += [tab] * 6
        args += list(tabs)
    widths = (SSD_W, RET_W, 512, LANES, LANES, MLA_Q_RANK, MLA_KV_RANK, LANES)
    return pl.pallas_call(
        functools.partial(_inproj_kernel, latent),
        grid=(n // tm,),
        in_specs=in_specs,
        out_specs=[pl.BlockSpec((tm, w), row) for w in widths],
        out_shape=[jax.ShapeDtypeStruct((n, w), BF16 if k == 2 else F32) for k, w in enumerate(widths)],
        compiler_params=_cparams(1),
        name="inproj_lat" if latent else "inproj_ctx",
    )(*args)


def _ssd_kernel(latent, seq_len, n_prev, *refs):
    if latent:
        (in_ref, cw_ref, cb_ref, dtb_ref, alog_ref, dsk_ref, nw_ref, ex_ref, s0_ref,
         o_ref, xc_ref, dt_ref, cum_ref, sf_ref, sb_ref, cdb_ref, run_ref) = refs
        st_ref = prev_ref = None
    else:
        (in_ref, cw_ref, cb_ref, dtb_ref, alog_ref, dsk_ref, nw_ref, ex_ref) = refs[:8]
        prev_ref = refs[8] if n_prev else None
        (o_ref, st_ref, xc_ref, dt_ref, cum_ref, sf_ref, sb_ref, cdb_ref, run_ref) = refs[8 + bool(n_prev):]
        s0_ref = None
    nseq = in_ref.shape[0]
    for s in range(nseq):
        if prev_ref is not None:
            st_ref[s, 0:n_prev] = prev_ref[s]
        _ssd_sequence(latent, seq_len, nseq > 1, in_ref.at[s], cw_ref, cb_ref, dtb_ref, alog_ref, dsk_ref, nw_ref,
                      ex_ref, None if s0_ref is None else s0_ref.at[s], o_ref.at[s],
                      None if st_ref is None else st_ref.at[s, n_prev], xc_ref.at[s], dt_ref.at[s], cum_ref.at[s],
                      sf_ref.at[s], sb_ref.at[s], cdb_ref.at[s], run_ref.at[s])


def _ssd_sequence(latent, seq_len, unroll_all, in_ref, cw_ref, cb_ref, dtb_ref, alog_ref, dsk_ref, nw_ref, ex_ref,
                  s0_ref, o_ref, st_ref, xc_ref, dt_ref, cum_ref, sf_ref, sb_ref, cdb_ref, run_ref):
    T = SCAN_CHUNK
    nc = seq_len // T
    ii = _row_iota((T, T))
    jj = _lane_iota((T, T))
    tri_lo = (jj <= ii).astype(BF16)
    tri_up = (jj >= ii).astype(BF16)
    lane_c = _lane_iota((T, LANES))
    row_c = _row_iota((T, SSD_CONV_CH))
    fwd_lane = lane_c < SSD_HEADS
    a_row = -jnp.exp(alog_ref[...])
    expand = ex_ref[...]

    if latent:
        run_ref[...] = s0_ref[...]
    else:
        run_ref[...] = jnp.zeros(run_ref.shape, F32)

    def phase0(c, carry):
        r0 = pl.multiple_of(c * T, T)
        x_cur = in_ref[pl.ds(r0, T), 512:1280]
        prev_blk = in_ref[pl.ds(pl.multiple_of(jnp.maximum(r0 - SUBLANES, 0), SUBLANES), SUBLANES), 512:1280]
        next_blk = in_ref[pl.ds(pl.multiple_of(jnp.minimum(r0 + T, seq_len - SUBLANES), SUBLANES), SUBLANES), 512:1280]
        prev_row = jnp.where(c > 0, prev_blk[SUBLANES - 1:SUBLANES, :], 0.0)
        next_row = jnp.where(c < nc - 1, next_blk[0:1, :], 0.0)
        x_prev = jnp.where(row_c == 0, prev_row, pltpu.roll(x_cur, 1, 0))
        x_next = jnp.where(row_c == T - 1, next_row, pltpu.roll(x_cur, T - 1, 0))
        xc = _silu(cb_ref[...] + x_prev * cw_ref[0:1, :] + x_cur * cw_ref[1:2, :] + x_next * cw_ref[2:3, :])
        xc_ref[pl.ds(r0, T), :] = xc

        dt = _softplus(in_ref[pl.ds(r0, T), 1280:SSD_W] + dtb_ref[...])
        la = dt * a_row
        cum = jnp.where(fwd_lane, _split_dot_left(tri_lo, la), _split_dot_left(tri_up, la))
        tot = jnp.where(fwd_lane[0:1], cum[T - 1:T, :], cum[0:1, :])
        dt_ref[pl.ds(r0, T), :] = dt
        cum_ref[pl.ds(r0, T), :] = cum
        w = dt * jnp.exp(tot - cum)
        wcd = jnp.concatenate([w, jnp.broadcast_to(jnp.exp(tot), (SUBLANES, LANES))], axis=0)
        wcd_x = _split_dot(wcd, expand)
        w_x = wcd_x[0:T]
        cd_x = wcd_x[T:T + 1]
        cdb_ref[c] = jnp.broadcast_to(cd_x[:, 512:1024], (SUBLANES, 512))
        xs = xc[:, 0:512]
        bm = xc[:, 512:640].astype(BF16)
        for p in range(4):
            g = p // 2
            sl = slice(p * LANES, (p + 1) * LANES)
            vw = jnp.concatenate([xs[:, sl] * w_x[:, sl], xs[:, sl] * w_x[:, 512 + p * LANES:512 + (p + 1) * LANES]],
                                 axis=1)
            cs = _bdot_tn(bm, vw)[g * 64:(g + 1) * 64]
            s_run = run_ref[p]
            sf_ref[c, p] = s_run[:, 0:LANES]
            sb_ref[c, p] = cs[:, LANES:2 * LANES]
            new_f = s_run[:, 0:LANES] * cd_x[:, sl] + cs[:, 0:LANES]
            run_ref[p] = jnp.concatenate([new_f, s_run[:, LANES:2 * LANES]], axis=1)
        return carry

    lax.fori_loop(0, nc, phase0, 0, unroll=min(nc, SSD_UNROLL))

    def bwd_states(k, carry):
        c = nc - 1 - k
        cd = cdb_ref[c]
        for p in range(4):
            s_run = run_ref[p]
            cs_b = sb_ref[c, p]
            sb_ref[c, p] = s_run[:, LANES:2 * LANES]
            new_b = s_run[:, LANES:2 * LANES] * cd[0:1, p * LANES:(p + 1) * LANES] + cs_b
            run_ref[p] = jnp.concatenate([s_run[:, 0:LANES], new_b], axis=1)
        return carry

    lax.fori_loop(0, nc, bwd_states, 0, unroll=unroll_all)
    if not latent:
        for p in range(4):
            s_fin = run_ref[p]
            for d in range(2):
                for hh in range(2):
                    c0 = d * LANES + hh * 64
                    st_ref[d, 2 * p + hh] = s_fin[:, c0:c0 + 64]

    lower = jj <= ii
    upper = jj >= ii
    zeros64 = jnp.zeros((64, 2 * LANES), F32)

    def phase1(c, carry):
        r0 = pl.multiple_of(c * T, T)
        xc = xc_ref[pl.ds(r0, T), :]
        xs = xc[:, 0:512]
        bm = xc[:, 512:640].astype(BF16)
        cm = xc[:, 640:768]
        dt = dt_ref[pl.ds(r0, T), :]
        cum = cum_ref[pl.ds(r0, T), :]
        cum_t = cum.T
        dt_t = dt.T
        e_x = _split_dot(jnp.exp(cum), expand)
        gmat = [_bdot_nt(jnp.where((lane_c < 64) == (g == 0), cm, 0.0), bm) for g in range(2)]
        cmb = cm.astype(BF16)
        outs = []
        for p in range(4):
            g = p // 2
            sl = slice(p * LANES, (p + 1) * LANES)
            xs_p = xs[:, sl].astype(BF16)
            halves = []
            for h in (2 * p, 2 * p + 1):
                df = jnp.exp(jnp.where(lower, cum[:, h:h + 1] - cum_t[h:h + 1, :], NEG_BIG)) * dt_t[h:h + 1, :]
                hb = SSD_HEADS + h
                db = jnp.exp(jnp.where(upper, cum[:, hb:hb + 1] - cum_t[hb:hb + 1, :], NEG_BIG)) * dt_t[hb:hb + 1, :]
                m_h = (gmat[g] * (df + db)).astype(BF16)
                halves.append(jnp.dot(m_h, xs_p, preferred_element_type=F32))
            o_p = jnp.where(lane_c < 64, halves[0], halves[1])
            s_in = jnp.concatenate([sf_ref[c, p], sb_ref[c, p]], axis=1)
            s_pad = jnp.concatenate([s_in, zeros64] if g == 0 else [zeros64, s_in], axis=0)
            oi = jnp.dot(cmb, s_pad.astype(BF16), preferred_element_type=F32)
            o_p = o_p + oi[:, 0:LANES] * e_x[:, sl] + oi[:, LANES:2 * LANES] * e_x[:, 512 + p * LANES:512 + (p + 1) * LANES]
            outs.append(o_p)
        y = jnp.concatenate(outs, axis=1) + dsk_ref[...] * xs
        z = in_ref[pl.ds(r0, T), 0:512]
        o_ref[pl.ds(r0, T), :] = _rms(y * _silu(z), nw_ref[...])
        return carry

    lax.fori_loop(0, nc, phase1, 0, unroll=unroll_all)


def _ssd_mixer(ssd_in, bsz, seq_len, latent, layer, cw, cb, dtb, alog, dsk, nw, expand, s0, prev_states):
    nc = seq_len // SCAN_CHUNK
    n_prev = 0 if prev_states is None else prev_states.shape[1]
    nseq = max(1, SSD_CHUNKS_PER_STEP // nc)
    assert bsz % nseq == 0
    const2 = lambda b: (0, 0)
    per_b3 = lambda b: (b, 0, 0)
    per_b4 = lambda b: (b, 0, 0, 0)
    in_specs = [pl.BlockSpec((nseq, seq_len, SSD_W), per_b3),
                _layer_spec((3, SSD_CONV_CH), layer),
                _layer_spec((1, SSD_CONV_CH), layer),
                _layer_spec((1, LANES), layer),
                _layer_spec((1, LANES), layer),
                _layer_spec((1, SSD_INNER), layer),
                _layer_spec((1, SSD_INNER), layer),
                pl.BlockSpec((LANES, 1024), const2)]
    args = [ssd_in.reshape(bsz, seq_len, SSD_W), cw, cb, dtb, alog, dsk, nw, expand]
    out_specs = [pl.BlockSpec((nseq, seq_len, SSD_INNER), per_b3)]
    out_shape = [jax.ShapeDtypeStruct((bsz, seq_len, SSD_INNER), F32)]
    if latent:
        in_specs.append(pl.BlockSpec((nseq, 4, 64, 2 * LANES), per_b4))
        args.append(s0)
    else:
        st_tail = (2, SSD_HEADS, SSD_STATE, SSD_HEAD_DIM)
        per_b6 = lambda b: (b, 0, 0, 0, 0, 0)
        if n_prev:
            in_specs.append(pl.BlockSpec((nseq, n_prev) + st_tail, per_b6))
            args.append(prev_states)
        out_specs.append(pl.BlockSpec((nseq, n_prev + 1) + st_tail, per_b6))
        out_shape.append(jax.ShapeDtypeStruct((bsz, n_prev + 1) + st_tail, F32))
    scratch = [pltpu.VMEM((nseq, seq_len, SSD_CONV_CH), F32),
               pltpu.VMEM((nseq, seq_len, LANES), F32),
               pltpu.VMEM((nseq, seq_len, LANES), F32),
               pltpu.VMEM((nseq, nc, 4, 64, LANES), F32),
               pltpu.VMEM((nseq, nc, 4, 64, LANES), F32),
               pltpu.VMEM((nseq, nc, SUBLANES, 512), F32),
               pltpu.VMEM((nseq, 4, 64, 2 * LANES), F32)]
    res = pl.pallas_call(
        functools.partial(_ssd_kernel, latent, seq_len, n_prev),
        grid=(bsz // nseq,),
        in_specs=in_specs, out_specs=out_specs, out_shape=out_shape,
        scratch_shapes=scratch,
        compiler_params=_cparams(1),
        name="ssd_lat" if latent else "ssd_ctx",
    )(*args)
    return res


def _ret_kernel(latent, seq_len, n_prev, *refs):
    if latent:
        (in_ref, lg_ref, gn_ref, s0_ref,
         o_ref, dc_ref, we_ref, cd_ref, sf_ref, sb_ref, run_ref) = refs
        st_ref = prev_ref = None
    else:
        (in_ref, lg_ref, gn_ref) = refs[:3]
        prev_ref = refs[3] if n_prev else None
        (o_ref, st_ref, dc_ref, we_ref, cd_ref, sf_ref, sb_ref, run_ref) = refs[3 + bool(n_prev):]
        s0_ref = None
    T = SCAN_CHUNK

    @pl.when(pl.program_id(0) == 0)
    def _tables():
        la = -_softplus(-lg_ref[...])
        ii = _row_iota((T, T)).astype(F32)
        jj = _lane_iota((T, T)).astype(F32)
        for h in range(RET_HEADS):
            la_f = la[h:h + 1, :]
            la_b = la[RET_HEADS + h:RET_HEADS + h + 1, :]
            dc_ref[h] = (jnp.exp(jnp.where(jj <= ii, (ii - jj) * la_f, NEG_BIG))
                         + jnp.exp(jnp.where(jj >= ii, (jj - ii) * la_b, NEG_BIG)))
            we_ref[h, 0] = jnp.exp((T - 1.0 - ii) * la_f)
            we_ref[h, 1] = jnp.exp(ii * la_b)
            we_ref[h, 2] = jnp.exp((ii + 1.0) * la_f)
            we_ref[h, 3] = jnp.exp((T - ii) * la_b)
            cd_ref[h] = jnp.concatenate([jnp.broadcast_to(jnp.exp(T * la_f), (SUBLANES, LANES)),
                                         jnp.broadcast_to(jnp.exp(T * la_b), (SUBLANES, LANES))], axis=1)

    for s in range(in_ref.shape[0]):
        if prev_ref is not None:
            st_ref[s, 0:n_prev] = prev_ref[s]
        _ret_sequence(latent, seq_len, in_ref.at[s], gn_ref, None if s0_ref is None else s0_ref.at[s],
                      o_ref.at[s], None if st_ref is None else st_ref.at[s, n_prev], dc_ref, we_ref, cd_ref,
                      sf_ref.at[s], sb_ref.at[s], run_ref.at[s])


def _ret_sequence(latent, seq_len, in_ref, gn_ref, s0_ref, o_ref, st_ref, dc_ref, we_ref, cd_ref,
                  sf_ref, sb_ref, run_ref):
    T = SCAN_CHUNK
    nc = seq_len // T

    if latent:
        run_ref[...] = s0_ref[...]
    else:
        run_ref[...] = jnp.zeros(run_ref.shape, F32)

    def phase0(c, carry):
        r0 = pl.multiple_of(c * T, T)
        for h in range(RET_HEADS):
            blk = h // 2
            k_blk = in_ref[pl.ds(r0, T), 256 + blk * LANES:256 + (blk + 1) * LANES].astype(BF16)
            v_h = in_ref[pl.ds(r0, T), 512 + h * LANES:512 + (h + 1) * LANES]
            vw = jnp.concatenate([v_h * we_ref[h, 0], v_h * we_ref[h, 1]], axis=1)
            cs = _bdot_tn(k_blk, vw)[(h % 2) * 64:(h % 2 + 1) * 64]
            s_run = run_ref[h]
            sf_ref[c, h] = s_run[:, 0:LANES]
            sb_ref[c, h] = cs[:, LANES:2 * LANES]
            new_f = s_run[:, 0:LANES] * cd_ref[h, 0:1, 0:LANES] + cs[:, 0:LANES]
            run_ref[h] = jnp.concatenate([new_f, s_run[:, LANES:2 * LANES]], axis=1)
        return carry

    unroll = min(nc, RET_UNROLL)
    lax.fori_loop(0, nc, phase0, 0, unroll=unroll)

    def bwd_states(k, carry):
        c = nc - 1 - k
        for h in range(RET_HEADS):
            s_run = run_ref[h]
            cs_b = sb_ref[c, h]
            sb_ref[c, h] = s_run[:, LANES:2 * LANES]
            new_b = s_run[:, LANES:2 * LANES] * cd_ref[h, 0:1, LANES:2 * LANES] + cs_b
            run_ref[h] = jnp.concatenate([s_run[:, 0:LANES], new_b], axis=1)
        return carry

    lax.fori_loop(0, nc, bwd_states, 0, unroll=unroll)
    if not latent:
        for h in range(RET_HEADS):
            for d in range(2):
                st_ref[d, h] = run_ref[h, :, d * LANES:(d + 1) * LANES]

    lane_c = _lane_iota((T, LANES))
    zeros64 = jnp.zeros((64, 2 * LANES), F32)

    def phase1(c, carry):
        r0 = pl.multiple_of(c * T, T)
        for h in range(RET_HEADS):
            blk = h // 2
            q_blk = in_ref[pl.ds(r0, T), blk * LANES:(blk + 1) * LANES]
            k_blk = in_ref[pl.ds(r0, T), 256 + blk * LANES:256 + (blk + 1) * LANES].astype(BF16)
            v_h = in_ref[pl.ds(r0, T), 512 + h * LANES:512 + (h + 1) * LANES].astype(BF16)
            g_h = in_ref[pl.ds(r0, T), 1024 + h * LANES:1024 + (h + 1) * LANES]
            q_m = jnp.where((lane_c < 64) == (h % 2 == 0), q_blk, 0.0)
            m_h = (_bdot_nt(q_m, k_blk) * dc_ref[h]).astype(BF16)
            o = jnp.dot(m_h, v_h, preferred_element_type=F32)
            s_in = jnp.concatenate([sf_ref[c, h], sb_ref[c, h]], axis=1)
            s_pad = jnp.concatenate([s_in, zeros64] if h % 2 == 0 else [zeros64, s_in], axis=0)
            oi = _bdot(q_blk, s_pad)
            o = o + oi[:, 0:LANES] * we_ref[h, 2] + oi[:, LANES:2 * LANES] * we_ref[h, 3]
            oc = o - jnp.mean(o, axis=-1, keepdims=True)
            on = oc * lax.rsqrt(jnp.mean(oc * oc, axis=-1, keepdims=True) + NORM_EPS)
            o_ref[pl.ds(r0, T), h * LANES:(h + 1) * LANES] = (
                on * gn_ref[:, h * LANES:(h + 1) * LANES] * _silu(g_h))
        return carry

    lax.fori_loop(0, nc, phase1, 0, unroll=unroll)


def _ret_mixer(ret_in, bsz, seq_len, latent, layer, logit_rows, gn_w, s0, prev_states):
    nc = seq_len // SCAN_CHUNK
    T = SCAN_CHUNK
    n_prev = 0 if prev_states is None else prev_states.shape[1]
    nseq = max(1, SCAN_CHUNKS_PER_STEP // nc)
    assert bsz % nseq == 0
    per_b3 = lambda b: (b, 0, 0)
    per_b4 = lambda b: (b, 0, 0, 0)
    in_specs = [pl.BlockSpec((nseq, seq_len, RET_W), per_b3),
                _layer_spec((SUBLANES, LANES), layer),
                _layer_spec((1, 512), layer)]
    args = [ret_in.reshape(bsz, seq_len, RET_W), logit_rows, gn_w]
    out_specs = [pl.BlockSpec((nseq, seq_len, 512), per_b3)]
    out_shape = [jax.ShapeDtypeStruct((bsz, seq_len, 512), F32)]
    if latent:
        in_specs.append(pl.BlockSpec((nseq, RET_HEADS, 64, 2 * LANES), per_b4))
        args.append(s0)
    else:
        st_tail = (2, RET_HEADS, RET_QK, RET_V)
        per_b6 = lambda b: (b, 0, 0, 0, 0, 0)
        if n_prev:
            in_specs.append(pl.BlockSpec((nseq, n_prev) + st_tail, per_b6))
            args.append(prev_states)
        out_specs.append(pl.BlockSpec((nseq, n_prev + 1) + st_tail, per_b6))
        out_shape.append(jax.ShapeDtypeStruct((bsz, n_prev + 1) + st_tail, F32))
    scratch = [pltpu.VMEM((RET_HEADS, T, T), F32),
               pltpu.VMEM((RET_HEADS, 4, T, LANES), F32),
               pltpu.VMEM((RET_HEADS, SUBLANES, 2 * LANES), F32),
               pltpu.VMEM((nseq, nc, RET_HEADS, 64, LANES), F32),
               pltpu.VMEM((nseq, nc, RET_HEADS, 64, LANES), F32),
               pltpu.VMEM((nseq, RET_HEADS, 64, 2 * LANES), F32)]
    return pl.pallas_call(
        functools.partial(_ret_kernel, latent, seq_len, n_prev),
        grid=(bsz // nseq,),
        in_specs=in_specs, out_specs=out_specs, out_shape=out_shape,
        scratch_shapes=scratch,
        compiler_params=_cparams(1),
        name="ret_lat" if latent else "ret_ctx",
    )(*args)


LOG2_E = 1.4426950408889634


def _softmax_pv(s, v, scale=1.0):
    m = jnp.max(s, axis=-1, keepdims=True)
    p = jnp.exp2((s - m) * (scale * LOG2_E))
    l = jnp.sum(p, axis=-1, keepdims=True)
    return jnp.dot(p.astype(BF16), v, preferred_element_type=F32) / l


def _once_per_sequence(fill, single_step):
    if single_step:
        fill()
    else:
        pl.when(pl.program_id(1) == 0)(fill)


def _att_kernel(latent, tq, *refs):
    if latent:
        q_ref, kn_ref, vn_ref, kc_ref, vc_ref, o_ref, kb_ref, vb_ref = refs
    else:
        q_ref, kn_ref, vn_ref, o_ref, kb_ref, vb_ref = refs
        kc_ref = vc_ref = None
    for s in range(q_ref.shape[0]):
        _att_sequence(latent, tq, q_ref.at[s], kn_ref.at[s], vn_ref.at[s],
                      None if kc_ref is None else kc_ref.at[s], None if vc_ref is None else vc_ref.at[s],
                      o_ref.at[s], kb_ref.at[s], vb_ref.at[s])


def _att_sequence(latent, tq, q_ref, kn_ref, vn_ref, kc_ref, vc_ref, o_ref, kb_ref, vb_ref):
    def _fill():
        if latent:
            kb_ref[0:PAST_LEN, :] = kc_ref[...].astype(BF16)
            vb_ref[0:PAST_LEN, :] = vc_ref[...].astype(BF16)
            kb_ref[PAST_LEN:, :] = kn_ref[...].astype(BF16)
            vb_ref[PAST_LEN:, :] = vn_ref[...].astype(BF16)
        else:
            kb_ref[...] = kn_ref[...].astype(BF16)
            vb_ref[...] = vn_ref[...].astype(BF16)

    _once_per_sequence(_fill, single_step=(kn_ref.shape[0] == tq))

    sub = min(tq, ATT_SUB_ROWS)
    lane_q = _lane_iota((sub, LANES))
    for i in range(tq // sub):
        rows = slice(i * sub, (i + 1) * sub)
        acc = []
        for g in range(ATT_KV_HEADS):
            qs = jnp.concatenate(
                [jnp.where((lane_q < 64) == (g == 0), q_ref[rows, j * LANES:(j + 1) * LANES] * 0.125, 0.0).astype(BF16)
                 for j in range(4)], axis=0)
            s = lax.dot_general(qs, kb_ref[...], (((1,), (1,)), ((), ())), preferred_element_type=F32)
            acc.append(_softmax_pv(s, vb_ref[...]))
        for j in range(4):
            o_ref[rows, j * LANES:(j + 1) * LANES] = jnp.where(lane_q < 64, acc[0][j * sub:(j + 1) * sub],
                                                               acc[1][j * sub:(j + 1) * sub])


def _att_mixer(aq, ak, av, bsz, seq_len, latent, k_ctx, v_ctx):
    tq = 512 if latent else 256
    lk = seq_len + (PAST_LEN if latent else 0)
    nseq = 1 if latent else ATT_CTX_SEQS
    assert bsz % nseq == 0
    qmap = lambda b, i: (b, i, 0)
    bmap = lambda b, i: (b, 0, 0)
    in_specs = [pl.BlockSpec((nseq, tq, 512), qmap),
                pl.BlockSpec((nseq, seq_len, LANES), bmap),
                pl.BlockSpec((nseq, seq_len, LANES), bmap)]
    args = [aq.reshape(bsz, seq_len, 512), ak.reshape(bsz, seq_len, LANES), av.reshape(bsz, seq_len, LANES)]
    if latent:
        in_specs += [pl.BlockSpec((nseq, PAST_LEN, LANES), bmap)] * 2
        args += [k_ctx, v_ctx]
    return pl.pallas_call(
        functools.partial(_att_kernel, latent, tq),
        grid=(bsz // nseq, seq_len // tq),
        in_specs=in_specs,
        out_specs=pl.BlockSpec((nseq, tq, 512), qmap),
        out_shape=jax.ShapeDtypeStruct((bsz, seq_len, 512), F32),
        scratch_shapes=[pltpu.VMEM((nseq, lk, LANES), BF16), pltpu.VMEM((nseq, lk, LANES), BF16)],
        compiler_params=_cparams(2),
        name="att_lat" if latent else "att_ctx",
    )(*args)


def _mla_kernel(latent, tq, *refs):
    if latent:
        (mq_ref, cn_ref, rn_ref, cc_ref, rc_ref, qn_ref, wq_ref, wk_ref, wv_ref,
         cos_ref, su_ref, sd_ref, o_ref, kb_ref, vb_ref) = refs
    else:
        (mq_ref, cn_ref, rn_ref, qn_ref, wq_ref, wk_ref, wv_ref, o_ref, kb_ref, vb_ref) = refs
        cc_ref = rc_ref = cos_ref = su_ref = sd_ref = None
    for s in range(mq_ref.shape[0]):
        _mla_sequence(latent, tq, mq_ref.at[s], cn_ref.at[s], rn_ref.at[s],
                      None if cc_ref is None else cc_ref.at[s], None if rc_ref is None else rc_ref.at[s],
                      qn_ref, wq_ref, wk_ref, wv_ref, cos_ref, su_ref, sd_ref,
                      o_ref.at[s], kb_ref.at[s], vb_ref.at[s])


def _mla_sequence(latent, tq, mq_ref, cn_ref, rn_ref, cc_ref, rc_ref, qn_ref, wq_ref, wk_ref, wv_ref,
                  cos_ref, su_ref, sd_ref, o_ref, kb_ref, vb_ref):
    def fill(r0, ckv, kr):
        cb = ckv.astype(BF16)
        n = ckv.shape[0]
        kn = jnp.dot(cb, wk_ref[...], preferred_element_type=F32)
        for h in range(MLA_HEADS):
            kb_ref[r0:r0 + n, h * LANES:(h + 1) * LANES] = (kn[:, h * LANES:(h + 1) * LANES] + kr).astype(BF16)
        vb_ref[r0:r0 + n, :] = jnp.dot(cb, wv_ref[...], preferred_element_type=F32).astype(BF16)

    def _fill():
        if latent:
            fill(0, cc_ref[...], rc_ref[...])
            fill(PAST_LEN, cn_ref[...], rn_ref[...])
        else:
            fill(0, cn_ref[...], rn_ref[...])

    _once_per_sequence(_fill, single_step=(cn_ref.shape[0] == tq))

    q = _bdot(_rms(mq_ref[...], qn_ref[...]), wq_ref[...])
    lane_q = _lane_iota((tq, LANES))
    scale = (MLA_NOPE + MLA_ROPE) ** -0.5
    for p in range(4):
        halves = []
        for h in (2 * p, 2 * p + 1):
            qh = q[:, h * LANES:(h + 1) * LANES]
            if latent:
                qh = _rope(qh, cos_ref[...], su_ref[...], sd_ref[...], 8)
            s = _bdot_nt(qh, kb_ref[:, h * LANES:(h + 1) * LANES])
            halves.append(_softmax_pv(s, vb_ref[:, p * LANES:(p + 1) * LANES], scale))
        o_ref[:, p * LANES:(p + 1) * LANES] = jnp.where(lane_q < 64, halves[0], halves[1])


def _mla_mixer(mq, ckv, kr, bsz, seq_len, latent, ckv_ctx, kr_ctx, qn, wq, wk, wv, layer, tabs):
    tq = 512 if latent else 256
    lk = seq_len + (PAST_LEN if latent else 0)
    nseq = 1 if latent else MLA_CTX_SEQS
    assert bsz % nseq == 0
    qmap = lambda b, i: (b, i, 0)
    bmap = lambda b, i: (b, 0, 0)
    lyr3 = lambda b, i: (layer, 0, 0)
    in_specs = [pl.BlockSpec((nseq, tq, MLA_Q_RANK), qmap),
                pl.BlockSpec((nseq, seq_len, MLA_KV_RANK), bmap),
                pl.BlockSpec((nseq, seq_len, LANES), bmap)]
    args = [mq.reshape(bsz, seq_len, MLA_Q_RANK), ckv.reshape(bsz, seq_len, MLA_KV_RANK),
            kr.reshape(bsz, seq_len, LANES)]
    if latent:
        in_specs += [pl.BlockSpec((nseq, PAST_LEN, MLA_KV_RANK), bmap),
                     pl.BlockSpec((nseq, PAST_LEN, LANES), bmap)]
        args += [ckv_ctx, kr_ctx]
    in_specs += [_layer_spec((1, MLA_Q_RANK), layer),
                 _resident((None, MLA_Q_RANK, 1024), lyr3),
                 _resident((None, MLA_KV_RANK, 1024), lyr3),
                 _resident((None, MLA_KV_RANK, 512), lyr3)]
    args += [qn, wq, wk, wv]
    if latent:
        in_specs += [pl.BlockSpec((tq, LANES), lambda b, i: (i, 0))] * 3
        args += list(tabs)
    return pl.pallas_call(
        functools.partial(_mla_kernel, latent, tq),
        grid=(bsz // nseq, seq_len // tq),
        in_specs=in_specs,
        out_specs=pl.BlockSpec((nseq, tq, 512), qmap),
        out_shape=jax.ShapeDtypeStruct((bsz, seq_len, 512), F32),
        scratch_shapes=[pltpu.VMEM((nseq, lk, 1024), BF16), pltpu.VMEM((nseq, lk, 512), BF16)],
        compiler_params=_cparams(2),
        name="mla_lat" if latent else "mla_ctx",
    )(*args)


def _merge_kernel(x_ref, sh_ref, sc_ref, ga_ref, gpre_ref, gpost_ref, o1_ref, o2_ref, o3_ref, o4_ref,
                  wm_ref, bm_ref, wb_ref, wo_ref, y_ref):
    x = x_ref[...]
    hb = (_rms(x, gpre_ref[...]) * (1.0 + sc_ref[...]) + sh_ref[...]).astype(BF16)
    merged = None
    for k, o_ref in enumerate((o1_ref, o2_ref, o3_ref, o4_ref)):
        cols = slice(k * D_MODEL, (k + 1) * D_MODEL)
        gate = jax.nn.sigmoid(jnp.dot(hb, wm_ref[:, cols], preferred_element_type=F32) + bm_ref[:, cols])
        term = gate * _bdot(o_ref[...], wb_ref[k])
        merged = term if merged is None else merged + term
    m = _bdot(merged, wo_ref[...])
    y_ref[...] = x + ga_ref[...] * _rms(m, gpost_ref[...])


def _merge(x2d, mod5, layer, latent, seq_len, g_pre, g_post, branches, w_merge, b_merge, w_br, w_out):
    n = x2d.shape[0]
    tm = TM_ROWS
    row = lambda i: (i, 0)
    in_specs = ([pl.BlockSpec((tm, D_MODEL), row)]
                + [_mod_spec(layer, latent, seq_len, tm, k) for k in (0, 1, 2)]
                + [_layer_spec((1, D_MODEL), layer)] * 2
                + [pl.BlockSpec((tm, 512), row)] * 4
                + [_resident((None, D_MODEL, 4 * D_MODEL), lambda i: (layer, 0, 0)),
                   _layer_spec((1, 4 * D_MODEL), layer),
                   _resident((None, 4, 512, D_MODEL), lambda i: (layer, 0, 0, 0)),
                   _resident((None, D_MODEL, D_MODEL), lambda i: (layer, 0, 0))])
    return pl.pallas_call(
        _merge_kernel,
        grid=(n // tm,),
        in_specs=in_specs,
        out_specs=pl.BlockSpec((tm, D_MODEL), row),
        out_shape=jax.ShapeDtypeStruct((n, D_MODEL), F32),
        compiler_params=_cparams(1),
        name="merge_lat" if latent else "merge_ctx",
    )(x2d, mod5, mod5, mod5, g_pre, g_post, *[b.reshape(n, 512) for b in branches],
      w_merge, b_merge, w_br, w_out)


def _ffn_kernel(tm, seq_len, x_ref, xp_ref, xn_ref, sh_ref, sc_ref, gf_ref, gpre_ref, gpost_ref,
                wu_ref, cw_ref, cb_ref, wd_ref, y_ref):
    n_seq = max(tm // seq_len, 1)
    seg = tm // n_seq
    tiles_per_seq = seq_len // tm
    if tiles_per_seq > 1:
        i = pl.program_id(0)
        first = (i % tiles_per_seq) == 0
        last = (i % tiles_per_seq) == tiles_per_seq - 1
    else:
        first = last = True

    def nm(x):
        return _rms(x, gpre_ref[...]) * (1.0 + sc_ref[...]) + sh_ref[...]

    x = x_ref[...]
    h = nm(x)
    zero8 = jnp.zeros((SUBLANES, D_MODEL), F32)
    pieces = [jnp.where(first, 0.0, nm(xp_ref[...]))]
    for s in range(n_seq):
        pieces.append(h[s * seg:(s + 1) * seg])
        pieces.append(zero8 if s < n_seq - 1 else jnp.where(last, 0.0, nm(xn_ref[...])))
    hb = jnp.concatenate(pieces, axis=0).astype(BF16)
    rows = tm + SUBLANES * (n_seq + 1)

    def conv_cols(c0):
        cols = slice(c0, c0 + FF_HALF)
        p = jnp.dot(hb, wu_ref[:, cols], preferred_element_type=F32)
        w = cw_ref[:, cols]
        u = (pltpu.roll(p, 1, 0) * w[0:1] + p * w[1:2] + pltpu.roll(p, rows - 1, 0) * w[2:3])
        segs = [u[SUBLANES + s * (seg + SUBLANES):SUBLANES + s * (seg + SUBLANES) + seg] for s in range(n_seq)]
        return (segs[0] if n_seq == 1 else jnp.concatenate(segs, axis=0)) + cb_ref[:, cols]

    acc = None
    for j in range(D_FF // FF_HALF):
        up = conv_cols(j * FF_HALF)
        gate = conv_cols(D_FF + j * FF_HALF)
        part = _bdot(_silu(gate) * up, wd_ref[j * FF_HALF:(j + 1) * FF_HALF, :])
        acc = part if acc is None else acc + part
    y_ref[...] = x + gf_ref[...] * _rms(acc, gpost_ref[...])


def _ffn(x2d, mod5, layer, latent, seq_len, g_pre, g_post, w_up, conv_w, conv_b, w_down):
    n = x2d.shape[0]
    tm = TM_ROWS
    assert seq_len % tm == 0 or tm % seq_len == 0
    hb = tm // SUBLANES
    n_hb = n // SUBLANES
    row = lambda i: (i, 0)
    lyr3 = lambda i: (layer, 0, 0)
    in_specs = ([pl.BlockSpec((tm, D_MODEL), row),
                 pl.BlockSpec((SUBLANES, D_MODEL), lambda i: (jnp.maximum(i * hb - 1, 0), 0)),
                 pl.BlockSpec((SUBLANES, D_MODEL), lambda i: (jnp.minimum((i + 1) * hb, n_hb - 1), 0))]
                + [_mod_spec(layer, latent, seq_len, tm, k) for k in (3, 4, 5)]
                + [_layer_spec((1, D_MODEL), layer)] * 2
                + [_resident((None, D_MODEL, 2 * D_FF), lyr3),
                   _resident((None, 3, 2 * D_FF), lyr3),
                   _resident((None, 1, 2 * D_FF), lyr3),
                   _resident((None, D_FF, D_MODEL), lyr3)])
    return pl.pallas_call(
        functools.partial(_ffn_kernel, tm, seq_len),
        grid=(n // tm,),
        in_specs=in_specs,
        out_specs=pl.BlockSpec((tm, D_MODEL), row),
        out_shape=jax.ShapeDtypeStruct((n, D_MODEL), F32),
        compiler_params=_cparams(1),
        name="ffn_lat" if latent else "ffn_ctx",
    )(x2d, x2d, x2d, mod5, mod5, mod5, g_pre, g_post, w_up, conv_w, conv_b, w_down)


def _rope_tables(seq_len, head_dim, lane_of_dim):
    d_axis = head_dim // 2
    t = np.arange(seq_len)
    pos = np.stack([(t // GRID_W).astype(np.float32), (t % GRID_W).astype(np.float32)], axis=0)
    inv_freq = (np.float32(ROPE_BASE) ** (-np.arange(0, d_axis, 2, dtype=np.float32) / np.float32(d_axis))).astype(np.float32)
    cos = np.ones((seq_len, LANES), np.float32)
    s_up = np.zeros((seq_len, LANES), np.float32)
    s_dn = np.zeros((seq_len, LANES), np.float32)
    for lane, d in enumerate(lane_of_dim):
        if d < 0:
            continue
        axis, j = divmod(d, d_axis)
        second = j >= d_axis // 2
        ang = (pos[axis] * inv_freq[j % (d_axis // 2)]).astype(np.float32)
        cos[:, lane] = np.cos(ang)
        if second:
            s_up[:, lane] = np.sin(ang)
        else:
            s_dn[:, lane] = -np.sin(ang)
    return jnp.asarray(cos), jnp.asarray(s_up), jnp.asarray(s_dn)


def _ssd_expand_matrix():
    e = np.zeros((LANES, 1024), np.float32)
    for d in range(2):
        for h in range(SSD_HEADS):
            e[d * SSD_HEADS + h, d * 512 + h * 64:d * 512 + (h + 1) * 64] = 1.0
    return jnp.asarray(e, dtype=BF16)


def _matmul_weights(w_in, mla_w_uq, mla_w_ukv, w_br_ssd, w_br_ret, w_br_att, w_br_mla, w_merge, w_out,
                    w_ffn_up, ffn_conv_w, ffn_conv_b, w_ffn_down):
    off = np.cumsum([0, 512, 768, 16, 256, 256, 512, 512, 512, 128, 128, 384, 288])
    col = lambda k: w_in[:, :, off[k]:off[k + 1]]
    zeros = lambda n: jnp.zeros((DEPTH, D_MODEL, n), F32)
    w_ssd = jnp.pad(w_in[:, :, :off[3]], ((0, 0), (0, 0), (0, SSD_W - off[3]))).astype(BF16)
    w_ret = w_in[:, :, off[3]:off[7]].astype(BF16)
    aq = col(7).reshape(DEPTH, D_MODEL, ATT_KV_HEADS, 4, ATT_HEAD_DIM).swapaxes(2, 3).reshape(DEPTH, D_MODEL, 512)
    mckv = col(11)
    w_am = jnp.concatenate(
        [aq, col(8), col(9), col(10), mckv[:, :, :MLA_KV_RANK], zeros(64), mckv[:, :, MLA_KV_RANK:], zeros(32)],
        axis=2).astype(BF16)

    uq = mla_w_uq.reshape(DEPTH, MLA_Q_RANK, MLA_HEADS, MLA_NOPE + MLA_ROPE)
    wq = jnp.pad(uq, ((0, 0), (0, 0), (0, 0), (0, 32))).reshape(DEPTH, MLA_Q_RANK, 1024).astype(BF16)
    ukv = mla_w_ukv.reshape(DEPTH, MLA_KV_RANK, MLA_HEADS, MLA_NOPE + MLA_V)
    wk = jnp.pad(ukv[..., :MLA_NOPE], ((0, 0), (0, 0), (0, 0), (0, 64))).reshape(DEPTH, MLA_KV_RANK, 1024).astype(BF16)
    wv = ukv[..., MLA_NOPE:].reshape(DEPTH, MLA_KV_RANK, 512).astype(BF16)

    w_att = (w_br_att.reshape(DEPTH, ATT_KV_HEADS, 4, ATT_HEAD_DIM, D_MODEL).swapaxes(1, 2)
             .reshape(DEPTH, 512, D_MODEL))
    w_br = jnp.stack([w_br_ssd, w_br_ret, w_att, w_br_mla], axis=1).astype(BF16)

    return dict(
        w_ssd=w_ssd, w_ret=w_ret, w_am=w_am, wq=wq, wk=wk, wv=wv, w_br=w_br,
        w_merge=w_merge.astype(BF16), w_out=w_out.astype(BF16),
        w_up=w_ffn_up.astype(BF16), ffn_cw=ffn_conv_w,
        ffn_cb=ffn_conv_b.reshape(DEPTH, 1, 2 * D_FF),
        w_down=w_ffn_down.astype(BF16),
    )


def _small_params(ssd_conv_w, ssd_conv_b, ssd_dt_bias, ssd_a_log, ssd_d, ssd_norm_w,
                  ret_decay_logit, ret_gn_w, att_q_norm, att_k_norm, mla_q_norm, mla_kv_norm,
                  b_merge, g_pre_mix, g_post_mix, g_pre_ffn, g_post_ffn):
    row = lambda v: v.reshape(DEPTH, 1, -1)
    pad_row = lambda v: jnp.pad(row(v), ((0, 0), (0, 0), (0, LANES - v[0].size)))
    return dict(
        qn=row(jnp.tile(att_q_norm, (1, 2))), kn=row(jnp.tile(att_k_norm, (1, 2))),
        cn=row(mla_kv_norm),
        ssd_cw=ssd_conv_w, ssd_cb=row(ssd_conv_b),
        ssd_dtb=pad_row(ssd_dt_bias), ssd_alog=pad_row(ssd_a_log),
        ssd_dsk=row(jnp.repeat(ssd_d, SSD_HEAD_DIM, axis=1)),
        ssd_nw=row(ssd_norm_w),
        ret_logit=jnp.broadcast_to(ret_decay_logit.reshape(DEPTH, SUBLANES, 1), (DEPTH, SUBLANES, LANES)),
        ret_gn=row(ret_gn_w),
        mla_qn=row(mla_q_norm),
        b_merge=row(b_merge),
        g_pre_mix=row(g_pre_mix), g_post_mix=row(g_post_mix),
        g_pre_ffn=row(g_pre_ffn), g_post_ffn=row(g_post_ffn),
    )


def _ssd_state_to_pairs(s):
    b = s.shape[0]
    s = s.reshape(b, 2, 4, 2, 64, 64).transpose(0, 2, 4, 1, 3, 5)
    return s.reshape(b, 4, 64, 256)


def _ret_state_to_lanes(s):
    b = s.shape[0]
    return s.transpose(0, 2, 3, 1, 4).reshape(b, RET_HEADS, 64, 256)


def _trunk_pass(x2d, mod5, layer, latent, bsz, seq_len, lp, mw, tabs64, tabs_mla, expand, ctx, prev_cache):
    (ssd_in, ret_in, aq, ak, av, mq, ckv, kr) = _inproj(
        x2d, mod5, layer, latent, seq_len, lp['g_pre_mix'], mw['w_ssd'], mw['w_ret'], mw['w_am'],
        lp['qn'], lp['kn'], lp['cn'], (tabs64 + tabs_mla) if latent else None)
    ssd_res = _ssd_mixer(ssd_in, bsz, seq_len, latent, layer, lp['ssd_cw'], lp['ssd_cb'], lp['ssd_dtb'],
                         lp['ssd_alog'], lp['ssd_dsk'], lp['ssd_nw'], expand,
                         ctx['ssd'] if latent else None, prev_cache['ssd'] if prev_cache else None)
    ret_res = _ret_mixer(ret_in, bsz, seq_len, latent, layer, lp['ret_logit'], lp['ret_gn'],
                         ctx['ret'] if latent else None, prev_cache['ret'] if prev_cache else None)
    o_att = _att_mixer(aq, ak, av, bsz, seq_len, latent,
                       ctx['att_k'] if latent else None, ctx['att_v'] if latent else None)
    o_mla = _mla_mixer(mq, ckv, kr, bsz, seq_len, latent,
                       ctx['mla_ckv'] if latent else None, ctx['mla_kr'] if latent else None,
                       lp['mla_qn'], mw['wq'], mw['wk'], mw['wv'], layer, tabs_mla)
    x2d = _merge(x2d, mod5, layer, latent, seq_len, lp['g_pre_mix'], lp['g_post_mix'],
                 (ssd_res[0], ret_res[0], o_att, o_mla), mw['w_merge'], lp['b_merge'], mw['w_br'], mw['w_out'])
    x2d = _ffn(x2d, mod5, layer, latent, seq_len, lp['g_pre_ffn'], lp['g_post_ffn'],
               mw['w_up'], mw['ffn_cw'], mw['ffn_cb'], mw['w_down'])
    cache = None
    if not latent:
        cache = dict(ssd=ssd_res[1], ret=ret_res[1],
                     att_k=ak.reshape(bsz, seq_len, ATT_KV_HEADS, ATT_HEAD_DIM),
                     att_v=av.reshape(bsz, seq_len, ATT_KV_HEADS, ATT_HEAD_DIM),
                     mla_ckv=ckv.reshape(bsz, seq_len, MLA_KV_RANK),
                     mla_krope=kr.reshape(bsz, seq_len, LANES)[:, :, 64:64 + MLA_ROPE])
    return x2d, cache


def kernel(x_prompt, x_sample, state_ssd, state_ret, cache_att_k, cache_att_v, cache_mla_ckv, cache_mla_krope, c, c_ctx, w_mod, b_mod, g_pre_mix, g_post_mix, g_pre_ffn, g_post_ffn, w_in, ssd_conv_w, ssd_conv_b, ssd_dt_bias, ssd_a_log, ssd_d, ssd_norm_w, ret_decay_logit, ret_gn_w, att_q_norm, att_k_norm, mla_q_norm, mla_w_uq, mla_kv_norm, mla_w_ukv, w_br_ssd, w_br_ret, w_br_att, w_br_mla, w_merge, b_merge, w_out, w_ffn_up, ffn_conv_w, ffn_conv_b, w_ffn_down):
    cvec = jnp.concatenate([c_ctx[None, :], c, jnp.zeros((SUBLANES - 1 - DEC_BATCH, D_MODEL), F32)], axis=0)
    mod5 = _modulation(cvec, w_mod, b_mod).reshape(DEPTH, SUBLANES, 6, 1, D_MODEL)

    lane_dims_64 = [d % 64 for d in range(LANES)]
    lane_dims_mla = [d - 64 if 64 <= d < 96 else -1 for d in range(LANES)]
    tabs64 = _rope_tables(DEC_SEQ, ATT_HEAD_DIM, lane_dims_64)
    tabs_mla = _rope_tables(DEC_SEQ, MLA_ROPE, lane_dims_mla)
    expand = _ssd_expand_matrix()

    xp = x_prompt.reshape(BATCH * SEQ, D_MODEL)
    xs = x_sample.reshape(DEC_BATCH * DEC_SEQ, D_MODEL)
    caches = []
    mw = _matmul_weights(w_in, mla_w_uq, mla_w_ukv, w_br_ssd, w_br_ret, w_br_att, w_br_mla, w_merge, w_out,
                         w_ffn_up, ffn_conv_w, ffn_conv_b, w_ffn_down)
    lp = _small_params(ssd_conv_w, ssd_conv_b, ssd_dt_bias, ssd_a_log, ssd_d, ssd_norm_w,
                       ret_decay_logit, ret_gn_w, att_q_norm, att_k_norm, mla_q_norm, mla_kv_norm,
                       b_merge, g_pre_mix, g_post_mix, g_pre_ffn, g_post_ffn)
    for i in range(DEPTH):
        xp, cache = _trunk_pass(xp, mod5, i, False, BATCH, SEQ, lp, mw, tabs64, tabs_mla, expand, None,
                                caches[-1] if caches else None)
        caches.append(cache)
        ctx = dict(ssd=_ssd_state_to_pairs(state_ssd[:, i]), ret=_ret_state_to_lanes(state_ret[:, i]),
                   att_k=cache_att_k[:, i].reshape(DEC_BATCH, PAST_LEN, LANES),
                   att_v=cache_att_v[:, i].reshape(DEC_BATCH, PAST_LEN, LANES),
                   mla_ckv=cache_mla_ckv[:, i],
                   mla_kr=jnp.pad(cache_mla_krope[:, i], ((0, 0), (0, 0), (64, 32))))
        xs, _ = _trunk_pass(xs, mod5, i, True, DEC_BATCH, DEC_SEQ, lp, mw, tabs64, tabs_mla, expand, ctx, None)

    stack = lambda k: jnp.stack([cc[k] for cc in caches], axis=1)
    return (xp.reshape(BATCH, SEQ, D_MODEL), xs.reshape(DEC_BATCH, DEC_SEQ, D_MODEL),
            caches[-1]['ssd'], caches[-1]['ret'],
            stack('att_k'), stack('att_v'), stack('mla_ckv'), stack('mla_krope'))
```
